```python
import math
import functools
import jax
import jax.numpy as jnp
from jax import lax
import numpy as np

D_MODEL = 1024
BATCH = 4
SEQ = 4096
DEPTH = 4
DEC_BATCH = 128
DEC_SEQ = 8
PAST_LEN = 2048
PAGE_SIZE = 128

ATT_WIDTH = D_MODEL // 2
POOL_WIDTH = D_MODEL - ATT_WIDTH
HEAD_DIM = 64
N_HEADS = ATT_WIDTH // HEAD_DIM
N_KV = 2
GROUP = N_HEADS // N_KV
CMP_LEN = 32
CMP_STRIDE = 16
SEL_BLOCK = 64
N_SELECT = 16
WINDOW = 512
N_BRANCH = 3
POOL_WINDOWS = (2, 4, 8, 16)
N_POOL_GROUPS = len(POOL_WINDOWS)
POOL_GROUP_DIM = POOL_WIDTH // N_POOL_GROUPS
POOL_STATE = max(POOL_WINDOWS) - 1
N_BUCKETS = 32
MAX_DISTANCE = 128
D_FF = -(-8 * D_MODEL // (3 * 256)) * 256
Q_BLOCK = 128
EPS = 1e-6
NEG_INF = -1e30
Q_COLS = N_HEADS * HEAD_DIM
KV_COLS = 2 * N_KV * HEAD_DIM
GATE_COLS = N_HEADS * N_BRANCH
IN_COLS = Q_COLS + N_BRANCH * KV_COLS + GATE_COLS + POOL_WIDTH

kernel_name = "hymba_nsa_pool_decoder_step"


def rms_norm(x, g):
    xf = x.astype(jnp.float32)
    y = xf * lax.rsqrt(jnp.mean(xf * xf, axis=-1, keepdims=True) + EPS)
    return (y * g.astype(jnp.float32)).astype(x.dtype)


def rel_bucket(dist):
    n = jnp.maximum(dist, 0)
    max_exact = N_BUCKETS // 2
    nf = jnp.maximum(n, 1).astype(jnp.float32)
    large = max_exact + (jnp.log(nf / max_exact) / math.log(MAX_DISTANCE / max_exact)
                         * (N_BUCKETS - max_exact)).astype(jnp.int32)
    return jnp.where(n < max_exact, n, jnp.minimum(large, N_BUCKETS - 1))


def masked_softmax(s, mask):
    s = jnp.where(mask, s.astype(jnp.float32), NEG_INF)
    p = jax.nn.softmax(s, axis=-1)
    return jnp.where(mask, p, 0.0)


def compress_blocks(kv, cmp_pos, w_cmp):
    B, Tk = kv.shape[:2]
    ratio = CMP_LEN // CMP_STRIDE
    n_sub = Tk // CMP_STRIDE
    n_cmp = n_sub - ratio + 1
    sub = kv.reshape(B, n_sub, CMP_STRIDE, 2, N_KV, HEAD_DIM)
    pe = cmp_pos.reshape(ratio, CMP_STRIDE, 2, HEAD_DIM)
    w = w_cmp.reshape(2, ratio, CMP_STRIDE, HEAD_DIM, HEAD_DIM)
    out = 0
    for r in range(ratio):
        part = jnp.einsum('bsljgd,jlde->bsjge', sub + pe[r][None, None, :, :, None, :], w[:, r])
        out = out + part[:, r:r + n_cmp]
    return out


def block_importance(p_cmp, n_sel):
    ratio = CMP_LEN // CMP_STRIDE
    spb = SEL_BLOCK // CMP_STRIDE
    pad = [(0, 0)] * (p_cmp.ndim - 1) + [(ratio - 1, ratio - 1)]
    P = jnp.pad(p_cmp, pad)
    return sum(P[..., o:o + spb * n_sel:spb] for o in range(spb + ratio - 1))


def gather_blocks(blocks, idx):
    return jax.vmap(jax.vmap(lambda bl, ix: bl[ix]))(blocks, idx)


def nsa_block(q, gates, q_pos, comp, sel_blocks, wkv, w_pos, rel_bias):
    B, Tq = q.shape[:2]
    qg = q.reshape(B, Tq, N_KV, GROUP, HEAD_DIM)
    bias_hd = rel_bias.astype(jnp.float32).reshape(N_BUCKETS, N_KV, GROUP)

    n_cmp = comp.shape[1]
    c_end = CMP_STRIDE * jnp.arange(n_cmp) + CMP_LEN - 1
    dist = q_pos[:, None] - c_end[None, :]
    s = jnp.einsum('bqgrd,bcgd->bgrqc', qg, comp[:, :, 0]).astype(jnp.float32)
    s = s + bias_hd[rel_bucket(dist)].transpose(2, 3, 0, 1)
    p_cmp = masked_softmax(s, dist >= 0)
    o_cmp = jnp.einsum('bgrqc,bcgd->bqgrd', p_cmp.astype(comp.dtype), comp[:, :, 1])

    n_sel = sel_blocks.shape[2]
    imp = block_importance(p_cmp.sum(axis=2), n_sel)
    cur = q_pos // SEL_BLOCK
    j = jnp.arange(n_sel)[None, :]
    forced = (j == 0) | (j == cur[:, None]) | (j == cur[:, None] - 1)
    score = jnp.where(forced, jnp.inf, imp)
    score = jnp.where(j > cur[:, None], -jnp.inf, score)
    _, idx = lax.top_k(score, min(N_SELECT, n_sel))
    kv_g = gather_blocks(sel_blocks, idx)
    k_pos = idx[..., None] * SEL_BLOCK + jnp.arange(SEL_BLOCK)
    dist = q_pos[:, None, None] - k_pos
    s = jnp.einsum('bqgrd,bgqksd->bgrqks', qg, kv_g[..., 0, :]).astype(jnp.float32)
    g_idx = jnp.arange(N_KV)[None, :, None, None, None]
    bias = bias_hd.transpose(1, 0, 2)[g_idx, rel_bucket(dist)]
    s = s + bias.transpose(0, 1, 5, 2, 3, 4)
    shp = s.shape
    mask = (dist >= 0).reshape(B, N_KV, 1, Tq, -1)
    p = masked_softmax(s.reshape(*shp[:4], -1), mask).reshape(shp)
    o_sel = jnp.einsum('bgrqks,bgqksd->bqgrd', p.astype(kv_g.dtype), kv_g[..., 1, :])

    dist = q_pos[:, None] - w_pos[None, :]
    mask = (dist >= 0) & (dist < WINDOW) & (w_pos[None, :] >= 0)
    s = jnp.einsum('bqgrd,bwgd->bgrqw', qg, wkv[:, :, 0]).astype(jnp.float32)
    s = s + bias_hd[rel_bucket(dist)].transpose(2, 3, 0, 1)
    p = masked_softmax(s, mask)
    o_win = jnp.einsum('bgrqw,bwgd->bqgrd', p.astype(wkv.dtype), wkv[:, :, 1])

    g = gates.reshape(B, Tq, N_KV, GROUP, N_BRANCH)
    o = g[..., 0:1] * o_cmp + g[..., 1:2] * o_sel + g[..., 2:3] * o_win
    return o.reshape(B, Tq, Q_COLS)


def nsa_prompt(q, gates, kv_cmp, kv_sel, kv_win, cmp_pos, w_cmp, rel_bias):
    B, T = q.shape[:2]
    comp = compress_blocks(kv_cmp, cmp_pos, w_cmp)
    sel_blocks = kv_sel.reshape(B, T // SEL_BLOCK, SEL_BLOCK, 2, N_KV, HEAD_DIM).transpose(0, 4, 1, 2, 3, 5)
    win_pad = jnp.pad(kv_win, ((0, 0), (WINDOW, 0), (0, 0), (0, 0), (0, 0)))

    def one_block(s0):
        q_pos = s0 + jnp.arange(Q_BLOCK)
        qb = lax.dynamic_slice_in_dim(q, s0, Q_BLOCK, axis=1)
        gb = lax.dynamic_slice_in_dim(gates, s0, Q_BLOCK, axis=1)
        wkv = lax.dynamic_slice_in_dim(win_pad, s0, WINDOW + Q_BLOCK, axis=1)
        w_pos = s0 - WINDOW + jnp.arange(WINDOW + Q_BLOCK)
        return nsa_block(qb, gb, q_pos, comp, sel_blocks, wkv, w_pos, rel_bias)

    out = lax.map(one_block, jnp.arange(T // Q_BLOCK) * Q_BLOCK)
    return out.transpose(1, 0, 2, 3).reshape(B, T, Q_COLS)


def nsa_sample(q, gates, kv_cmp, kv_sel, kv_win, cache_cmp, cache_sel, win_buf, page_table,
               cmp_pos, w_cmp, rel_bias):
    B, Tq = q.shape[:2]
    past = page_table.shape[1] * PAGE_SIZE

    def with_past(cache, new):
        rows = cache[page_table].reshape(B, past, 2, N_KV, HEAD_DIM)
        full = jnp.concatenate([rows, new], axis=1)
        t_pad = -(-full.shape[1] // SEL_BLOCK) * SEL_BLOCK
        return jnp.pad(full, ((0, 0), (0, t_pad - full.shape[1]), (0, 0), (0, 0), (0, 0)))

    full_cmp = with_past(cache_cmp, kv_cmp)
    full_sel = with_past(cache_sel, kv_sel)
    comp = compress_blocks(full_cmp, cmp_pos, w_cmp)
    n_sel = full_sel.shape[1] // SEL_BLOCK
    sel_blocks = full_sel.reshape(B, n_sel, SEL_BLOCK, 2, N_KV, HEAD_DIM).transpose(0, 4, 1, 2, 3, 5)
    wb = win_buf.shape[1]
    wkv = jnp.concatenate([win_buf, kv_win], axis=1)
    w_pos = past - wb + jnp.arange(wb + Tq)
    q_pos = past + jnp.arange(Tq)
    return nsa_block(q, gates, q_pos, comp, sel_blocks, wkv, w_pos, rel_bias)


def pool_mix(u_prev, u, pos0, w_pool, pool_scale):
    B, T, C = u.shape
    ext = jnp.concatenate([u_prev, u], axis=1)
    cs = jnp.pad(jnp.cumsum(ext.astype(jnp.float32), axis=1), ((0, 0), (1, 0), (0, 0)))
    pos = pos0 + jnp.arange(T)
    end = cs[:, POOL_STATE + 1:]
    means = []
    for k, w in enumerate(POOL_WINDOWS):
        ch = slice(k * POOL_GROUP_DIM, (k + 1) * POOL_GROUP_DIM)
        win_sum = end[..., ch] - cs[:, POOL_STATE + 1 - w:POOL_STATE + 1 - w + T, ch]
        cnt = jnp.minimum(pos + 1, w).astype(jnp.float32)[None, :, None]
        means.append(win_sum / cnt)
    pooled = (jnp.concatenate(means, axis=-1) - u.astype(jnp.float32)).astype(u.dtype)
    pooled = pooled.reshape(B, T, N_POOL_GROUPS, POOL_GROUP_DIM)
    out = jnp.einsum('btkc,kcd->btkd', pooled, w_pool).reshape(B, T, C) * pool_scale
    return out, ext[:, -POOL_STATE:]


def trunk_layer(x, nsa_fn, pool_prev, pos0, norm_mix, norm_ffn, w_in, w_out, w_pool, pool_scale,
                w_ffn_in, w_ffn_out):
    B, T, _ = x.shape
    h = rms_norm(x, norm_mix)
    z = h @ w_in
    q = z[..., :Q_COLS].reshape(B, T, N_HEADS, HEAD_DIM) * HEAD_DIM ** -0.5
    kvs = [z[..., Q_COLS + i * KV_COLS:Q_COLS + (i + 1) * KV_COLS].reshape(B, T, 2, N_KV, HEAD_DIM)
           for i in range(N_BRANCH)]
    off = Q_COLS + N_BRANCH * KV_COLS
    gates = jax.nn.sigmoid(z[..., off:off + GATE_COLS]).reshape(B, T, N_HEADS, N_BRANCH)
    u = z[..., off + GATE_COLS:]
    o_att = nsa_fn(q, gates, kvs[0], kvs[1], kvs[2])
    o_pool, pool_state = pool_mix(pool_prev, u, pos0, w_pool, pool_scale)
    x = x + jnp.concatenate([o_att, o_pool], axis=-1) @ w_out
    h = rms_norm(x, norm_ffn)
    gate, up = jnp.split(h @ w_ffn_in, 2, axis=-1)
    x = x + (jax.nn.silu(gate) * up) @ w_ffn_out
    return x, kvs, pool_state


def setup_inputs(seed: int = 0) -> dict:
    key = jax.random.key(seed)
    ks = jax.random.split(key, 20)
    n_pages = PAST_LEN // PAGE_SIZE
    n_used = DEC_BATCH * n_pages
    n_phys = n_used + -(-n_used // 4)
    wb = min(WINDOW, PAST_LEN)

    def nrm(k, shape, scale):
        return scale * jax.random.normal(k, shape, jnp.float32)

    page_table = jax.random.permutation(ks[0], n_phys)[:n_used].reshape(DEC_BATCH, n_pages).astype(jnp.int32)
    return {
        'x_prompt': nrm(ks[1], (BATCH, SEQ, D_MODEL), 1.0),
        'x_sample': nrm(ks[2], (DEC_BATCH, DEC_SEQ, D_MODEL), 1.0),
        'cache_cmp': nrm(ks[3], (DEPTH, n_phys, PAGE_SIZE, 2, N_KV, HEAD_DIM), 1.0),
        'cache_sel': nrm(ks[4], (DEPTH, n_phys, PAGE_SIZE, 2, N_KV, HEAD_DIM), 1.0),
        'state_win': nrm(ks[5], (DEPTH, DEC_BATCH, wb, 2, N_KV, HEAD_DIM), 1.0),
        'state_pool': nrm(ks[6], (DEPTH, DEC_BATCH, POOL_STATE, POOL_WIDTH), 1.0),
        'page_table': page_table,
        'rel_bias': nrm(ks[7], (N_BUCKETS, N_HEADS), 0.5),
        'norm_mix': 1.0 + nrm(ks[8], (DEPTH, D_MODEL), 0.05),
        'norm_ffn': 1.0 + nrm(ks[9], (DEPTH, D_MODEL), 0.05),
        'norm_final': 1.0 + nrm(ks[10], (D_MODEL,), 0.05),
        'w_in': nrm(ks[11], (DEPTH, D_MODEL, IN_COLS), D_MODEL ** -0.5),
        'w_out': nrm(ks[12], (DEPTH, ATT_WIDTH + POOL_WIDTH, D_MODEL), D_MODEL ** -0.5),
        'cmp_pos': nrm(ks[13], (DEPTH, CMP_LEN, 2, HEAD_DIM), 0.1),
        'w_cmp': nrm(ks[14], (DEPTH, 2, CMP_LEN, HEAD_DIM, HEAD_DIM), (CMP_LEN * HEAD_DIM) ** -0.5),
        'w_pool': nrm(ks[15], (DEPTH, N_POOL_GROUPS, POOL_GROUP_DIM, POOL_GROUP_DIM), POOL_GROUP_DIM ** -0.5),
        'pool_scale': 1.0 + nrm(ks[16], (DEPTH, POOL_WIDTH), 0.05),
        'w_ffn_in': nrm(ks[17], (DEPTH, D_MODEL, 2 * D_FF), D_MODEL ** -0.5),
        'w_ffn_out': nrm(ks[18], (DEPTH, D_FF, D_MODEL), D_FF ** -0.5),
    }


def reference(x_prompt, x_sample, cache_cmp, cache_sel, state_win, state_pool, page_table, rel_bias,
              norm_mix, norm_ffn, norm_final, w_in, w_out, cmp_pos, w_cmp, w_pool, pool_scale,
              w_ffn_in, w_ffn_out):
    past = page_table.shape[1] * PAGE_SIZE
    wb = state_win.shape[2]
    xp, xs = x_prompt, x_sample
    pool_zero = jnp.zeros((x_prompt.shape[0], POOL_STATE, POOL_WIDTH), x_prompt.dtype)
    p_cmp, p_sel, p_win, p_pool = [], [], [], []
    s_cmp, s_sel, s_win, s_pool = [], [], [], []
    for l in range(DEPTH):
        lw = (norm_mix[l], norm_ffn[l], w_in[l], w_out[l], w_pool[l], pool_scale[l], w_ffn_in[l], w_ffn_out[l])
        prompt_nsa = functools.partial(nsa_prompt, cmp_pos=cmp_pos[l], w_cmp=w_cmp[l], rel_bias=rel_bias)
        xp, kvp, pool_p = trunk_layer(xp, prompt_nsa, pool_zero, 0, *lw)
        p_cmp.append(kvp[0])
        p_sel.append(kvp[1])
        p_win.append(jnp.pad(kvp[2], ((0, 0), (wb, 0), (0, 0), (0, 0), (0, 0)))[:, -wb:])
        p_pool.append(pool_p)
        sample_nsa = functools.partial(nsa_sample, cache_cmp=cache_cmp[l], cache_sel=cache_sel[l],
                                       win_buf=state_win[l], page_table=page_table,
                                       cmp_pos=cmp_pos[l], w_cmp=w_cmp[l], rel_bias=rel_bias)
        xs, kvs, pool_s = trunk_layer(xs, sample_nsa, state_pool[l], past, *lw)
        s_cmp.append(kvs[0])
        s_sel.append(kvs[1])
        s_win.append(jnp.concatenate([state_win[l], kvs[2]], axis=1)[:, -wb:])
        s_pool.append(pool_s)
    y_prompt = rms_norm(xp, norm_final)
    y_sample = rms_norm(xs, norm_final)
    return (y_prompt, y_sample,
            jnp.stack(p_cmp), jnp.stack(p_sel), jnp.stack(p_win), jnp.stack(p_pool),
            jnp.stack(s_cmp), jnp.stack(s_sel), jnp.stack(s_win), jnp.stack(s_pool))
```

```python
import functools
import math

import numpy as np
import jax
import jax.numpy as jnp
from jax import lax
from jax.experimental import pallas as pl
from jax.experimental.pallas import tpu as pltpu

D_MODEL = 1024
HEAD_DIM = 64
N_HEADS = 8
N_KV = 2
GROUP = N_HEADS // N_KV
ATT_WIDTH = N_HEADS * HEAD_DIM
POOL_WIDTH = D_MODEL - ATT_WIDTH
KV_COLS = 2 * N_KV * HEAD_DIM
KV_LANES = N_KV * HEAD_DIM
CMP_LEN = 32
CMP_STRIDE = 16
CMP_RATIO = CMP_LEN // CMP_STRIDE
SEL_BLOCK = 64
N_SELECT = 16
WINDOW = 512
N_BRANCH = 3
GATE_COLS = N_HEADS * N_BRANCH
POOL_WINDOWS = (2, 4, 8, 16)
POOL_GROUP_DIM = POOL_WIDTH // len(POOL_WINDOWS)
POOL_STATE = max(POOL_WINDOWS) - 1
POOL_HALO = 16
N_BUCKETS = 32
MAX_DISTANCE = 128
PAGE_SIZE = 128
EPS = 1e-6
NEG = -1e30
VALID_MIN = -1e29

LANE = 128
TQ = 128
FAR_TILE = 512
ROW_TILE = 512
VMEM_LIMIT = 52 * 1024 * 1024

F32 = jnp.float32
BF16 = jnp.bfloat16
NT_DIMS = (((1,), (1,)), ((), ()))


def _bucket_np(dist):
    n = np.maximum(dist, 0)
    max_exact = N_BUCKETS // 2
    nf = np.maximum(n, 1).astype(np.float32)
    large = max_exact + (np.log(nf / max_exact) / math.log(MAX_DISTANCE / max_exact)
                         * (N_BUCKETS - max_exact)).astype(np.int32)
    return np.where(n < max_exact, n, np.minimum(large, N_BUCKETS - 1)).astype(np.int32)


def _bucket_or_masked(dist, valid):
    return np.where(valid, _bucket_np(dist), -1).astype(np.int32)


FAR_DIST = TQ + 1


def _check_far_bucket(max_dist):
    assert (_bucket_np(np.arange(FAR_DIST, max_dist + 1)) == N_BUCKETS - 1).all()


def _bias_tile_kernel(bucket_ref, rb_ref, out_ref):
    bucket = bucket_ref[...]
    acc = jnp.zeros(bucket.shape, F32)
    for b in range(N_BUCKETS):
        acc = acc + jnp.where(bucket == b, rb_ref[b:b + 1, :], 0.0)
    acc = acc - rb_ref[N_BUCKETS - 1:N_BUCKETS, :]
    out_ref[...] = jnp.where(bucket < 0, NEG, acc)


def _bias_tiles(bucket_np, rb_cols):
    rows, cols = bucket_np.shape
    return pl.pallas_call(
        _bias_tile_kernel,
        out_shape=jax.ShapeDtypeStruct((rows, cols), F32),
    )(jnp.asarray(bucket_np), rb_cols)


DD0 = 0
DC0 = 2 * TQ
DC_ROWS = 24
DW0 = DC0 + 32
PROMPT_DELTA_ROWS = DW0 + TQ


def _prompt_bucket_table():
    i = np.arange(TQ)[None, :]
    kk = np.arange(2 * TQ)[:, None]
    d = i + TQ - kk
    dd = _bucket_or_masked(d, d >= 0)
    cc = np.arange(32)[:, None] - 16
    d = i - CMP_STRIDE * cc - (CMP_LEN - 1)
    dc = _bucket_or_masked(d, (d >= 0) & (cc < 8))
    j = np.arange(TQ)[:, None]
    d = i + WINDOW - j
    dw = _bucket_or_masked(d, (d >= 0) & (d < WINDOW))
    tab = np.concatenate([dd, dc, dw], axis=0)
    return np.tile(tab, (1, N_HEADS))


def _sample_bucket_table(past, tq, wb):
    col = np.arange(LANE)
    t = (col % tq)[None, :]
    colok = (col < N_HEADS * tq)[None, :]
    kk = np.arange(LANE)[:, None]
    d = LANE + t - kk
    dlast = _bucket_or_masked(d, colok & (d >= 0))
    d = t - kk
    dnew = _bucket_or_masked(d, colok & (d >= 0) & (kk < tq))
    d = wb + t - kk
    dedge = _bucket_or_masked(d, colok & (d >= 0) & (d < WINDOW))
    nsub = past // CMP_STRIDE
    c = np.arange(nsub)[:, None]
    d = past + t - CMP_STRIDE * c - (CMP_LEN - 1)
    dcmp = _bucket_or_masked(d, colok & (d >= 0) & (c < nsub - CMP_RATIO + 1))
    return np.concatenate([dlast, dnew, dedge, dcmp], axis=0)


IN_Q = 0
IN_KV = ATT_WIDTH
IN_U = IN_KV + N_BRANCH * KV_COLS
IN_G = IN_U + POOL_WIDTH
IN_COLS_PAD = IN_G + LANE


def _rms(x, g):
    return x * lax.rsqrt(jnp.mean(x * x, axis=-1, keepdims=True) + EPS) * g


def _sigmoid(x):
    return 1.0 / (1.0 + jnp.exp(-x))


def _inproj_kernel(x_ref, g_ref, w_ref, q_ref, kvc_ref, kvs_ref, kvw_ref, kvsb_ref, kvwb_ref, u_ref, gate_ref):
    h = _rms(x_ref[...], g_ref[...]).astype(BF16)
    z = jnp.dot(h, w_ref[...], preferred_element_type=F32)
    q_ref[...] = z[:, IN_Q:IN_KV].astype(BF16)
    kvc_ref[...] = z[:, IN_KV:IN_KV + KV_COLS]
    kvs = z[:, IN_KV + KV_COLS:IN_KV + 2 * KV_COLS]
    kvw = z[:, IN_KV + 2 * KV_COLS:IN_U]
    kvs_ref[...] = kvs
    kvw_ref[...] = kvw
    kvsb_ref[...] = kvs.astype(BF16)
    kvwb_ref[...] = kvw.astype(BF16)
    u_ref[...] = z[:, IN_U:IN_G]
    gate_ref[...] = _sigmoid(z[:, IN_G:IN_COLS_PAD])


def _inproj(x, g, w):
    n = x.shape[0]
    tm = min(ROW_TILE, n)
    row = lambda c: pl.BlockSpec((tm, c), lambda i: (i, 0))
    full = lambda a: pl.BlockSpec(a.shape, lambda i: (0,) * a.ndim)
    outs = [(ATT_WIDTH, BF16), (KV_COLS, F32), (KV_COLS, F32), (KV_COLS, F32), (KV_COLS, BF16), (KV_COLS, BF16),
            (POOL_WIDTH, F32), (LANE, F32)]
    return pl.pallas_call(
        _inproj_kernel,
        grid=(n // tm,),
        in_specs=[row(D_MODEL), full(g), full(w)],
        out_specs=[row(c) for c, _ in outs],
        out_shape=[jax.ShapeDtypeStruct((n, c), dt) for c, dt in outs],
        compiler_params=pltpu.CompilerParams(dimension_semantics=("parallel",), vmem_limit_bytes=VMEM_LIMIT),
    )(x, g, w)


def _compress_tokens(x_of, w_ref, pe_ref, n_sub):
    outs = []
    for j in range(2):
        parts = [jnp.zeros((n_sub, KV_LANES), F32) for _ in range(CMP_RATIO)]
        for l in range(CMP_STRIDE):
            x = x_of(l, j)
            for r in range(CMP_RATIO):
                row = (j * CMP_RATIO + r) * CMP_STRIDE + l
                a = (x + pe_ref[row:row + 1, :]).astype(BF16)
                parts[r] = parts[r] + jnp.dot(a, w_ref[j, r, l], preferred_element_type=F32)
        comp = parts[0]
        for r in range(1, CMP_RATIO):
            comp = comp + pltpu.roll(parts[r], n_sub - r, axis=0)
        outs.append(comp)
    return outs


def _compress_kernel(x_ref, w_ref, pe_ref, ck_ref, cvt_ref):
    n_sub = x_ref.shape[1]
    x_of = lambda l, j: x_ref[0, :, l * KV_COLS + j * KV_LANES:l * KV_COLS + (j + 1) * KV_LANES]
    ck, cv = _compress_tokens(x_of, w_ref, pe_ref, n_sub)
    ck_ref[0] = ck.astype(BF16)
    cvt_ref[0] = cv.T.astype(BF16)


def _compress_prompt(kvc, w_bd, pe_rows):
    b, t, _ = kvc.shape
    n_sub = t // CMP_STRIDE
    x = kvc.reshape(b, n_sub, CMP_STRIDE * KV_COLS)
    full = lambda a: pl.BlockSpec(a.shape, lambda i: (0,) * a.ndim)
    return pl.pallas_call(
        _compress_kernel,
        grid=(b,),
        in_specs=[pl.BlockSpec((1, n_sub, CMP_STRIDE * KV_COLS), lambda i: (i, 0, 0)), full(w_bd), full(pe_rows)],
        out_specs=[pl.BlockSpec((1, n_sub, KV_LANES), lambda i: (i, 0, 0)),
                   pl.BlockSpec((1, KV_LANES, n_sub), lambda i: (i, 0, 0))],
        out_shape=[jax.ShapeDtypeStruct((b, n_sub, KV_LANES), BF16), jax.ShapeDtypeStruct((b, KV_LANES, n_sub), BF16)],
        compiler_params=pltpu.CompilerParams(dimension_semantics=("parallel",), vmem_limit_bytes=VMEM_LIMIT),
    )(x, w_bd, pe_rows)


def _split_dot(m, p):
    hi = p.astype(BF16)
    lo = (p - hi.astype(F32)).astype(BF16)
    return jnp.dot(m, hi, preferred_element_type=F32) + jnp.dot(m, lo, preferred_element_type=F32)


def _pool_mix(ext_ref, u, pos, wpool_ref, pscale_ref, rows):
    outs = []
    for k, w in enumerate(POOL_WINDOWS):
        ln = slice(k * POOL_GROUP_DIM, (k + 1) * POOL_GROUP_DIM)
        ws = ext_ref[POOL_HALO:POOL_HALO + rows, ln]
        for s in range(1, w):
            ws = ws + ext_ref[POOL_HALO - s:POOL_HALO - s + rows, ln]
        cnt = jnp.minimum(pos + 1, w).astype(F32)
        pooled = (ws / cnt - u[:, ln]).astype(BF16)
        o = jnp.dot(pooled, wpool_ref[k], preferred_element_type=F32) * pscale_ref[:, ln]
        outs.append(o.astype(BF16))
    return outs


def _prompt_attn_kernel(q_ref, gate_ref, u_ref, uh_ref, pst_ref, ksel_ref, vsel_ref, kwin_ref, vwin_ref,
                        ck_ref, cvt_ref, delta_ref, mimp_ref, wpool_ref, pscale_ref,
                        mix_ref,
                        vst_s, vwt_s, sc_s, sel_s, self_s, acc_s, m_s, l_s, ext_s):
    qb = pl.program_id(1)
    t_len = ksel_ref.shape[1]
    n_cmp = ck_ref.shape[1]
    n_sel = t_len // SEL_BLOCK
    cols = N_HEADS * TQ
    blk_per_q = TQ // SEL_BLOCK
    blk_per_far = FAR_TILE // SEL_BLOCK

    @pl.when(qb == 0)
    def _():
        for c in range(t_len // FAR_TILE):
            rows = slice(c * FAR_TILE, (c + 1) * FAR_TILE)
            vst_s[:, rows] = vsel_ref[0, rows, :].astype(F32).T.astype(BF16)
            vwt_s[:, rows] = vwin_ref[0, rows, :].astype(F32).T.astype(BF16)

    q = q_ref[0].astype(F32)
    zeros = jnp.zeros((TQ, HEAD_DIM), F32)
    blocks = []
    for h in range(N_HEADS):
        qh = q[:, h * HEAD_DIM:(h + 1) * HEAD_DIM]
        halves = [zeros] * N_KV
        halves[h // GROUP] = qh
        blocks.append(jnp.concatenate(halves, axis=1))
    qbd = jnp.concatenate(blocks, axis=0).astype(BF16)

    def scores(k_rows):
        return lax.dot_general(k_rows, qbd, NT_DIMS, preferred_element_type=F32)

    sc = scores(ck_ref[0])
    crow = lax.broadcasted_iota(jnp.int32, (n_cmp, cols), 0)
    sc = jnp.where(crow >= 8 * qb + 8, NEG, sc)
    sc_s[0:16, :] = jnp.zeros((16, cols), F32)
    sc_s[16 + n_cmp:16 + n_cmp + 8, :] = jnp.zeros((8, cols), F32)
    sc_s[16:16 + n_cmp, :] = sc
    near = pl.ds(pl.multiple_of(8 * qb, 8), DC_ROWS)
    sc_s[near, :] = sc_s[near, :] + delta_ref[DC0:DC0 + DC_ROWS, :]
    sc = sc_s[16:16 + n_cmp, :]
    m_c = jnp.max(sc, axis=0, keepdims=True)
    p = jnp.exp(sc - m_c)
    l_c = jnp.sum(p, axis=0, keepdims=True)
    pn = p * jnp.where(m_c > VALID_MIN, 1.0 / l_c, 0.0)
    o_cmp = jnp.dot(cvt_ref[0], pn.astype(BF16), preferred_element_type=F32)

    jidx = lax.broadcasted_iota(jnp.int32, (n_sel, TQ), 0)
    qi = lax.broadcasted_iota(jnp.int32, (n_sel, TQ), 1)
    cur = blk_per_q * qb + qi // SEL_BLOCK
    forced = (jidx == 0) | (jidx == cur) | (jidx == cur - 1)
    sel_g = []
    for g in range(N_KV):
        ps = pn[:, (g * GROUP) * TQ:(g * GROUP + 1) * TQ]
        for r in range(1, GROUP):
            ps = ps + pn[:, (g * GROUP + r) * TQ:(g * GROUP + r + 1) * TQ]
        imp = _split_dot(mimp_ref[...], ps)
        score = jnp.where(forced, jnp.inf, imp)
        score = jnp.where(jidx > cur, -jnp.inf, score)
        cnt = jnp.zeros((n_sel, TQ), jnp.int32)
        for jp in range(n_sel):
            row = score[jp:jp + 1, :]
            cnt = cnt + jnp.where(jidx > jp, jnp.where(row >= score, 1, 0), jnp.where(row > score, 1, 0))
        sel = (cnt < min(N_SELECT, n_sel)) & (jidx <= cur)
        sel_g.append(jnp.where(sel, 1.0, 0.0))
    sel_full = jnp.concatenate([sel_g[h // GROUP] for h in range(N_HEADS)], axis=1)
    prev_blk = blk_per_q * jnp.maximum(qb - 1, 0)
    for j in range(n_sel):
        row = sel_full[j:j + 1, :]
        sel_s[j] = row
        self_s[j] = jnp.where(j < prev_blk, row, 0.0)

    m_s[...] = jnp.full((1, cols), NEG, F32)
    l_s[...] = jnp.zeros((1, cols), F32)
    acc_s[...] = jnp.zeros((KV_LANES, cols), F32)

    def online_update(s, vt):
        m_old = m_s[...]
        m_new = jnp.maximum(m_old, jnp.max(s, axis=0, keepdims=True))
        alpha = jnp.exp(m_old - m_new)
        p = jnp.exp(s - m_new)
        l_s[...] = alpha * l_s[...] + jnp.sum(p, axis=0, keepdims=True)
        acc_s[...] = alpha * acc_s[...] + jnp.dot(vt, p.astype(BF16), preferred_element_type=F32)
        m_s[...] = m_new

    def masked(s, mask3):
        nb = mask3.shape[0]
        s3 = s.reshape(nb, SEL_BLOCK, cols)
        return jnp.where(mask3 > 0.5, s3, NEG).reshape(nb * SEL_BLOCK, cols)

    prev_key = TQ * jnp.maximum(qb - 1, 0)
    n_far = (prev_key + FAR_TILE - 1) // FAR_TILE

    def far_body(t, carry):
        k0 = pl.multiple_of(t * FAR_TILE, FAR_TILE)
        s = scores(ksel_ref[0, pl.ds(k0, FAR_TILE), :])
        s = masked(s, self_s[pl.ds(t * blk_per_far, blk_per_far)])
        online_update(s, vst_s[:, pl.ds(k0, FAR_TILE)])
        return carry

    lax.fori_loop(0, n_far, far_body, 0)

    k_prev = pl.multiple_of(prev_key, TQ)
    k_diag = pl.multiple_of(qb * TQ, TQ)
    no_prev = jnp.where(qb == 0, NEG, 0.0)
    d_prev = delta_ref[DD0:DD0 + TQ, :]
    d_diag = delta_ref[DD0 + TQ:DD0 + 2 * TQ, :]
    s_prev = masked(scores(ksel_ref[0, pl.ds(k_prev, TQ), :]) + d_prev + no_prev, sel_s[pl.ds(prev_blk, blk_per_q)])
    s_diag = masked(scores(ksel_ref[0, pl.ds(k_diag, TQ), :]) + d_diag, sel_s[pl.ds(blk_per_q * qb, blk_per_q)])
    vt_near = jnp.concatenate([vst_s[:, pl.ds(k_prev, TQ)], vst_s[:, pl.ds(k_diag, TQ)]], axis=1)
    online_update(jnp.concatenate([s_prev, s_diag], axis=0), vt_near)
    l_sel = l_s[...]
    o_sel = acc_s[...]

    n_wc = WINDOW // TQ + 1
    s_w, k_w = [], []
    for c in range(n_wc):
        pos0 = (qb - (n_wc - 1) + c) * TQ
        k0 = pl.multiple_of(jnp.maximum(pos0, 0), TQ)
        s = scores(kwin_ref[0, pl.ds(k0, TQ), :])
        if c == 0:
            s = s + delta_ref[DW0:DW0 + TQ, :]
        elif c == n_wc - 2:
            s = s + d_prev
        elif c == n_wc - 1:
            s = s + d_diag
        if c < n_wc - 1:
            s = s + jnp.where(pos0 < 0, NEG, 0.0)
        s_w.append(s)
        k_w.append(k0)
    m_w = s_w[0].max(axis=0, keepdims=True)
    for s in s_w[1:]:
        m_w = jnp.maximum(m_w, s.max(axis=0, keepdims=True))
    l_win = jnp.zeros((1, cols), F32)
    o_win = jnp.zeros((KV_LANES, cols), F32)
    for s, k0 in zip(s_w, k_w):
        p = jnp.exp(s - m_w)
        l_win = l_win + jnp.sum(p, axis=0, keepdims=True)
        o_win = o_win + jnp.dot(vwt_s[:, pl.ds(k0, TQ)], p.astype(BF16), preferred_element_type=F32)

    gt = gate_ref[0].T
    gate_row = lambda x: jnp.concatenate([gt[N_BRANCH * h + x:N_BRANCH * h + x + 1, :] for h in range(N_HEADS)], axis=1)
    o_t = o_cmp * gate_row(0) + o_sel * (gate_row(1) / l_sel) + o_win * (gate_row(2) / l_win)
    pieces = []
    for h in range(N_HEADS):
        g = h // GROUP
        pieces.append(o_t[g * HEAD_DIM:(g + 1) * HEAD_DIM, h * TQ:(h + 1) * TQ])
    mix_ref[0, :, 0:ATT_WIDTH] = jnp.concatenate(pieces, axis=0).T.astype(BF16)

    u = u_ref[0]
    ext_s[0:POOL_HALO, :] = jnp.where(qb == 0, pst_ref[0], uh_ref[0])
    ext_s[POOL_HALO:POOL_HALO + TQ, :] = u
    pos = qb * TQ + lax.broadcasted_iota(jnp.int32, (TQ, 1), 0)
    for k, o in enumerate(_pool_mix(ext_s, u, pos, wpool_ref, pscale_ref, TQ)):
        mix_ref[0, :, ATT_WIDTH + k * POOL_GROUP_DIM:ATT_WIDTH + (k + 1) * POOL_GROUP_DIM] = o


def _prompt_attention(q, gates, u, pool_prev, kvsb, kvwb, ck, cvt, delta, mimp, wpool, pscale):
    b, t, _ = q.shape
    assert t % FAR_TILE == 0 and t >= WINDOW + TQ
    n_sel = t // SEL_BLOCK
    n_cmp = ck.shape[1]
    cols = N_HEADS * TQ
    halo_per_q = TQ // POOL_HALO
    full = lambda a: pl.BlockSpec(a.shape, lambda i, j: (0,) * a.ndim)
    qblk = lambda c: pl.BlockSpec((1, TQ, c), lambda i, j: (i, j, 0))
    seq = lambda c: pl.BlockSpec((1, t, KV_LANES), lambda i, j: (i, 0, c))
    per_b = lambda a: pl.BlockSpec((1,) + a.shape[1:], lambda i, j: (i,) + (0,) * (a.ndim - 1))
    return pl.pallas_call(
        _prompt_attn_kernel,
        grid=(b, t // TQ),
        in_specs=[qblk(ATT_WIDTH), qblk(LANE), qblk(POOL_WIDTH),
                  pl.BlockSpec((1, POOL_HALO, POOL_WIDTH), lambda i, j: (i, jnp.maximum(j * halo_per_q - 1, 0), 0)),
                  per_b(pool_prev), seq(0), seq(1), seq(0), seq(1), per_b(ck), per_b(cvt),
                  full(delta), full(mimp), full(wpool), full(pscale)],
        out_specs=pl.BlockSpec((1, TQ, D_MODEL), lambda i, j: (i, j, 0)),
        out_shape=jax.ShapeDtypeStruct((b, t, D_MODEL), BF16),
        scratch_shapes=[pltpu.VMEM((KV_LANES, t), BF16), pltpu.VMEM((KV_LANES, t), BF16),
                        pltpu.VMEM((16 + n_cmp + 8, cols), F32),
                        pltpu.VMEM((n_sel, 1, cols), F32), pltpu.VMEM((n_sel, 1, cols), F32),
                        pltpu.VMEM((KV_LANES, cols), F32), pltpu.VMEM((1, cols), F32), pltpu.VMEM((1, cols), F32),
                        pltpu.VMEM((POOL_HALO + TQ, POOL_WIDTH), F32)],
        compiler_params=pltpu.CompilerParams(dimension_semantics=("parallel", "arbitrary"),
                                             vmem_limit_bytes=VMEM_LIMIT),
    )(q, gates, u, u, pool_prev, kvsb, kvsb, kvwb, kvwb, ck, cvt, delta, mimp, wpool, pscale)


SD_LAST = 0
SD_NEW = LANE
SD_EDGE = 2 * LANE
SD_CMP = 3 * LANE


def _sample_attn_kernel(pt_ref, q_ref, gate_ref, u_ref, kvs_ref, kvw_ref, swin_ref, spool_ref,
                        ccache_ref, scache_ref, wcmp_ref, pe_ref, delta_ref, mimp_ref, expand_ref,
                        wpool_ref, pscale_ref,
                        mix_ref, nwin_ref, npool_ref,
                        cbuf, sbuf, sem, ext_s, *, layer):
    b = pl.program_id(0)
    nb = pl.num_programs(0)
    n_pages = pt_ref.shape[1]
    past = n_pages * PAGE_SIZE
    n_sub = past // CMP_STRIDE
    sub_per_page = PAGE_SIZE // CMP_STRIDE
    tq = q_ref.shape[1]
    rows = N_HEADS * tq
    wb = swin_ref.shape[2]
    n_sel = past // SEL_BLOCK + 1
    slot = lax.rem(b, 2)

    def page_copies(seq, slt):
        cps = []
        for p in range(n_pages):
            pg = pt_ref[seq, p]
            cps.append(pltpu.make_async_copy(ccache_ref.at[layer, pg],
                                             cbuf.at[slt, pl.ds(p * sub_per_page, sub_per_page), :], sem.at[0, slt]))
            cps.append(pltpu.make_async_copy(scache_ref.at[layer, pg],
                                             sbuf.at[slt, pl.ds(p * PAGE_SIZE, PAGE_SIZE), :], sem.at[1, slt]))
        return cps

    @pl.when(b == 0)
    def _():
        for cp in page_copies(0, 0):
            cp.start()

    @pl.when(b + 1 < nb)
    def _():
        for cp in page_copies(b + 1, 1 - slot):
            cp.start()

    for cp in page_copies(b, slot):
        cp.wait()

    q = q_ref[0].astype(F32)
    zeros = jnp.zeros((tq, HEAD_DIM), F32)
    blocks = []
    for h in range(N_HEADS):
        halves = [zeros] * N_KV
        halves[h // GROUP] = q[:, h * HEAD_DIM:(h + 1) * HEAD_DIM]
        blocks.append(jnp.concatenate(halves, axis=1))
    qbd = jnp.concatenate(blocks, axis=0).astype(BF16)

    def scores(k_rows):
        return lax.dot_general(qbd, k_rows, NT_DIMS, preferred_element_type=F32)

    def pad_new(x):
        return jnp.concatenate([x, jnp.zeros((LANE - tq, KV_LANES), F32)], axis=0).astype(BF16)

    d_last = delta_ref[:, SD_LAST:SD_LAST + LANE]
    d_new = delta_ref[:, SD_NEW:SD_NEW + LANE]

    x_of = lambda l, j: cbuf[slot, :, l * KV_COLS + j * KV_LANES:l * KV_COLS + (j + 1) * KV_LANES]
    ck, cv = _compress_tokens(x_of, wcmp_ref, pe_ref, n_sub)
    sc = scores(ck.astype(BF16)) + delta_ref[:, SD_CMP:SD_CMP + n_sub]
    m_c = jnp.max(sc, axis=1, keepdims=True)
    p = jnp.exp(sc - m_c)
    l_c = jnp.sum(p, axis=1, keepdims=True)
    pn = p * jnp.where(m_c > VALID_MIN, 1.0 / l_c, 0.0)
    o_cmp = jnp.dot(pn.astype(BF16), cv.astype(BF16), preferred_element_type=F32)

    jidx = lax.broadcasted_iota(jnp.int32, (tq, LANE), 1)
    cur = (past + lax.broadcasted_iota(jnp.int32, (tq, LANE), 0)) // SEL_BLOCK
    forced = (jidx == 0) | (jidx == cur) | (jidx == cur - 1)
    sel_g = []
    for g in range(N_KV):
        ps = pn[g * GROUP * tq:(g * GROUP + 1) * tq, :]
        for r in range(1, GROUP):
            ps = ps + pn[(g * GROUP + r) * tq:(g * GROUP + r + 1) * tq, :]
        hi = ps.astype(BF16)
        lo = (ps - hi.astype(F32)).astype(BF16)
        imp = (jnp.dot(hi, mimp_ref[...], preferred_element_type=F32)
               + jnp.dot(lo, mimp_ref[...], preferred_element_type=F32))
        score = jnp.where(forced, jnp.inf, imp)
        score = jnp.where((jidx > cur) | (jidx >= n_sel), -jnp.inf, score)
        cnt = jnp.zeros((tq, LANE), jnp.int32)
        for jp in range(n_sel):
            col = score[:, jp:jp + 1]
            cnt = cnt + jnp.where(jidx > jp, jnp.where(col >= score, 1, 0), jnp.where(col > score, 1, 0))
        sel = (cnt < min(N_SELECT, n_sel)) & (jidx <= cur) & (jidx < n_sel)
        sel_g.append(jnp.where(sel, 1.0, 0.0))
    sel_rows = jnp.concatenate([sel_g[h // GROUP] for h in range(N_HEADS)], axis=0).astype(BF16)
    mask = jnp.dot(sel_rows, expand_ref[...], preferred_element_type=F32)

    def softmax_pv(parts):
        m = parts[0][0].max(axis=1, keepdims=True)
        for s, _ in parts[1:]:
            m = jnp.maximum(m, s.max(axis=1, keepdims=True))
        l = jnp.zeros((rows, 1), F32)
        o = jnp.zeros((rows, KV_LANES), F32)
        for s, v in parts:
            p = jnp.exp(s - m)
            l = l + jnp.sum(p, axis=1, keepdims=True)
            o = o + jnp.dot(p.astype(BF16), v, preferred_element_type=F32)
        return o, l

    kv_past = sbuf[slot]
    s_far = scores(kv_past[0:past - LANE, 0:KV_LANES].astype(BF16))
    s_last = scores(kv_past[past - LANE:past, 0:KV_LANES].astype(BF16)) + d_last
    s_past = jnp.where(mask > 0.5, jnp.concatenate([s_far, s_last], axis=1), NEG)
    kvs_new = kvs_ref[0]
    s_new = scores(pad_new(kvs_new[:, 0:KV_LANES])) + d_new
    o_sel, l_sel = softmax_pv([(s_past, kv_past[:, KV_LANES:KV_COLS].astype(BF16)),
                               (s_new, pad_new(kvs_new[:, KV_LANES:KV_COLS]))])

    win = swin_ref[0, 0]
    kvw_new = kvw_ref[0]
    s_first = scores(win[0:LANE, 0:KV_LANES].astype(BF16)) + delta_ref[:, SD_EDGE:SD_EDGE + LANE]
    s_mid = scores(win[LANE:wb - LANE, 0:KV_LANES].astype(BF16))
    s_lastw = scores(win[wb - LANE:wb, 0:KV_LANES].astype(BF16)) + d_last
    s_wnew = scores(pad_new(kvw_new[:, 0:KV_LANES])) + d_new
    o_win, l_win = softmax_pv([(jnp.concatenate([s_first, s_mid, s_lastw], axis=1), win[:, KV_LANES:KV_COLS].astype(BF16)),
                               (s_wnew, pad_new(kvw_new[:, KV_LANES:KV_COLS]))])
    nwin_ref[0, 0:wb - tq, :] = win[tq:wb, :]
    nwin_ref[0, wb - tq:wb, :] = kvw_new

    gates = gate_ref[0]
    gate_col = lambda x: jnp.concatenate([gates[:, N_BRANCH * h + x:N_BRANCH * h + x + 1] for h in range(N_HEADS)], axis=0)
    o = o_cmp * gate_col(0) + o_sel * (gate_col(1) / l_sel) + o_win * (gate_col(2) / l_win)
    pieces = []
    for h in range(N_HEADS):
        g = h // GROUP
        pieces.append(o[h * tq:(h + 1) * tq, g * HEAD_DIM:(g + 1) * HEAD_DIM])
    mix_ref[0, :, 0:ATT_WIDTH] = jnp.concatenate(pieces, axis=1).astype(BF16)

    u = u_ref[0]
    ext_s[0:1, :] = jnp.zeros((1, POOL_WIDTH), F32)
    ext_s[1:POOL_HALO, :] = spool_ref[0, 0]
    ext_s[POOL_HALO:POOL_HALO + tq, :] = u
    pos = past + lax.broadcasted_iota(jnp.int32, (tq, 1), 0)
    for k, o_k in enumerate(_pool_mix(ext_s, u, pos, wpool_ref, pscale_ref, tq)):
        mix_ref[0, :, ATT_WIDTH + k * POOL_GROUP_DIM:ATT_WIDTH + (k + 1) * POOL_GROUP_DIM] = o_k
    npool_ref[0] = ext_s[POOL_HALO + tq - POOL_STATE:POOL_HALO + tq, :]


def _sample_attention(layer, page_table, q, gates, u, kvs, kvw, state_win, state_pool, ccache, scache,
                      wcmp, pe_rows, delta, mimp, expand, wpool, pscale):
    nb, tq, _ = q.shape
    n_pages = page_table.shape[1]
    past = n_pages * PAGE_SIZE
    n_sub = past // CMP_STRIDE
    wb = state_win.shape[2]
    assert wb == WINDOW and wb >= 3 * LANE and tq <= CMP_STRIDE and POOL_HALO + tq - POOL_STATE >= 0
    assert past // SEL_BLOCK + 1 <= LANE and past % LANE == 0 and n_sub <= LANE
    full = lambda a: pl.BlockSpec(a.shape, lambda i, pt: (0,) * a.ndim)
    per_b = lambda c: pl.BlockSpec((1, tq, c), lambda i, pt: (i, 0, 0))
    layer_b = lambda a: pl.BlockSpec((1, 1) + a.shape[2:], lambda i, pt: (layer, i) + (0,) * (a.ndim - 2))
    hbm = pl.BlockSpec(memory_space=pl.ANY)
    grid_spec = pltpu.PrefetchScalarGridSpec(
        num_scalar_prefetch=1,
        grid=(nb,),
        in_specs=[per_b(ATT_WIDTH), per_b(LANE), per_b(POOL_WIDTH), per_b(KV_COLS), per_b(KV_COLS),
                  layer_b(state_win), layer_b(state_pool), hbm, hbm,
                  full(wcmp), full(pe_rows), full(delta), full(mimp), full(expand), full(wpool), full(pscale)],
        out_specs=[pl.BlockSpec((1, tq, D_MODEL), lambda i, pt: (i, 0, 0)),
                   pl.BlockSpec((1, wb, KV_COLS), lambda i, pt: (i, 0, 0)),
                   pl.BlockSpec((1, POOL_STATE, POOL_WIDTH), lambda i, pt: (i, 0, 0))],
        scratch_shapes=[pltpu.VMEM((2, n_sub, CMP_STRIDE * KV_COLS), F32), pltpu.VMEM((2, past, KV_COLS), F32),
                        pltpu.SemaphoreType.DMA((2, 2)), pltpu.VMEM((POOL_HALO + tq, POOL_WIDTH), F32)],
    )
    return pl.pallas_call(
        functools.partial(_sample_attn_kernel, layer=layer),
        grid_spec=grid_spec,
        out_shape=[jax.ShapeDtypeStruct((nb, tq, D_MODEL), BF16),
                   jax.ShapeDtypeStruct((nb, wb, KV_COLS), F32),
                   jax.ShapeDtypeStruct((nb, POOL_STATE, POOL_WIDTH), F32)],
        compiler_params=pltpu.CompilerParams(dimension_semantics=("arbitrary",), vmem_limit_bytes=VMEM_LIMIT),
    )(page_table, q, gates, u, kvs, kvw, state_win, state_pool, ccache, scache,
      wcmp, pe_rows, delta, mimp, expand, wpool, pscale)


def _ffn_kernel(x_ref, mix_ref, wout_ref, gn_ref, wg_ref, wu_ref, wd_ref, gf_ref, o_ref, xmid_s, h_s, acc_s, *,
                final_norm):
    c = pl.program_id(1)

    @pl.when(c == 0)
    def _():
        xm = x_ref[...] + jnp.dot(mix_ref[...], wout_ref[...], preferred_element_type=F32)
        xmid_s[...] = xm
        h_s[...] = _rms(xm, gn_ref[...]).astype(BF16)
        acc_s[...] = jnp.zeros(acc_s.shape, F32)

    h = h_s[...]
    gate = jnp.dot(h, wg_ref[...], preferred_element_type=F32)
    up = jnp.dot(h, wu_ref[...], preferred_element_type=F32)
    act = (gate * _sigmoid(gate) * up).astype(BF16)
    acc_s[...] += jnp.dot(act, wd_ref[...], preferred_element_type=F32)

    @pl.when(c == pl.num_programs(1) - 1)
    def _():
        y = xmid_s[...] + acc_s[...]
        o_ref[...] = _rms(y, gf_ref[...]) if final_norm else y


def _ffn(x, mix, wout, gn, wffn_in, wffn_out, gf, final_norm):
    n = x.shape[0]
    tm = min(ROW_TILE, n)
    d_ff = wffn_out.shape[0]
    n_chunk = 2
    tf = d_ff // n_chunk
    assert tf % LANE == 0 and n % tm == 0
    full = lambda a: pl.BlockSpec(a.shape, lambda i, c: (0,) * a.ndim)
    row = lambda cdim: pl.BlockSpec((tm, cdim), lambda i, c: (i, 0))
    return pl.pallas_call(
        functools.partial(_ffn_kernel, final_norm=final_norm),
        grid=(n // tm, n_chunk),
        in_specs=[row(D_MODEL), row(D_MODEL), full(wout), full(gn),
                  pl.BlockSpec((D_MODEL, tf), lambda i, c: (0, c)),
                  pl.BlockSpec((D_MODEL, tf), lambda i, c: (0, n_chunk + c)),
                  pl.BlockSpec((tf, D_MODEL), lambda i, c: (c, 0)), full(gf)],
        out_specs=row(D_MODEL),
        out_shape=jax.ShapeDtypeStruct((n, D_MODEL), F32),
        scratch_shapes=[pltpu.VMEM((tm, D_MODEL), F32), pltpu.VMEM((tm, D_MODEL), BF16), pltpu.VMEM((tm, D_MODEL), F32)],
        compiler_params=pltpu.CompilerParams(dimension_semantics=("parallel", "arbitrary"),
                                             vmem_limit_bytes=VMEM_LIMIT),
    )(x, mix, wout, gn, wffn_in, wffn_in, wffn_out, gf)


def _importance_matrix(n_sel, n_cmp_rows, n_cmp):
    spb = SEL_BLOCK // CMP_STRIDE
    j = np.arange(n_sel)[:, None]
    c = np.arange(n_cmp_rows)[None, :]
    return ((c >= spb * j - (CMP_RATIO - 1)) & (c <= spb * j + spb - 1) & (c < n_cmp)).astype(np.float32)


def kernel(x_prompt, x_sample, cache_cmp, cache_sel, state_win, state_pool, page_table, rel_bias, norm_mix, norm_ffn,
           norm_final, w_in, w_out, cmp_pos, w_cmp, w_pool, pool_scale, w_ffn_in, w_ffn_out):
    bp, t, _ = x_prompt.shape
    bs, tq, _ = x_sample.shape
    depth = w_in.shape[0]
    n_pages = page_table.shape[1]
    past = n_pages * PAGE_SIZE
    n_phys = cache_cmp.shape[1]
    wb = state_win.shape[2]
    _check_far_bucket(max(t, past + tq) + WINDOW)

    rb_prompt = jnp.repeat(rel_bias.astype(F32), TQ, axis=1)
    delta_p = _bias_tiles(_prompt_bucket_table(), rb_prompt)
    rb_sample = jnp.pad(jnp.repeat(rel_bias.astype(F32), tq, axis=1), ((0, 0), (0, LANE - N_HEADS * tq)))
    delta_s = _bias_tiles(_sample_bucket_table(past, tq, wb), rb_sample).T[:N_HEADS * tq]

    n_sub_p = t // CMP_STRIDE
    mimp_p = jnp.asarray(_importance_matrix(t // SEL_BLOCK, n_sub_p, n_sub_p - CMP_RATIO + 1), BF16)
    n_sub_s = past // CMP_STRIDE
    mimp_s = jnp.asarray(_importance_matrix(LANE, n_sub_s, n_sub_s - CMP_RATIO + 1).T, BF16)
    expand = jnp.asarray((np.arange(LANE)[:, None] == np.arange(past)[None, :] // SEL_BLOCK).astype(np.float32), BF16)

    pool_zero = jnp.zeros((bp, POOL_HALO, POOL_WIDTH), F32)
    ccache = cache_cmp.reshape(depth, n_phys, PAGE_SIZE // CMP_STRIDE, CMP_STRIDE * KV_COLS)
    scache = cache_sel.reshape(depth, n_phys, PAGE_SIZE, KV_COLS)
    swin = state_win.reshape(depth, bs, wb, KV_COLS)

    xp = x_prompt.reshape(bp * t, D_MODEL)
    xs = x_sample.reshape(bs * tq, D_MODEL)
    outs = {k: [] for k in ("p_cmp", "p_sel", "p_win", "p_pool", "s_cmp", "s_sel", "s_win", "s_pool")}
    kv_shape = lambda b, n: (b, n, 2, N_KV, HEAD_DIM)
    off_g = ATT_WIDTH + N_BRANCH * KV_COLS
    for l in range(depth):
        wl = w_in[l]
        w_all = jnp.concatenate([wl[:, :ATT_WIDTH] * HEAD_DIM ** -0.5, wl[:, ATT_WIDTH:off_g],
                                 wl[:, off_g + GATE_COLS:], wl[:, off_g:off_g + GATE_COLS],
                                 jnp.zeros((D_MODEL, LANE - GATE_COLS), F32)], axis=1).astype(BF16)
        g_mix = norm_mix[l].reshape(1, D_MODEL)
        g_ffn = norm_ffn[l].reshape(1, D_MODEL)
        g_fin = norm_final.reshape(1, D_MODEL)
        wc = w_cmp[l].reshape(2, CMP_RATIO, CMP_STRIDE, HEAD_DIM, HEAD_DIM)
        zero = jnp.zeros_like(wc)
        w_bd = jnp.concatenate([jnp.concatenate([wc, zero], axis=-1), jnp.concatenate([zero, wc], axis=-1)],
                               axis=-2).astype(BF16)
        pe = cmp_pos[l].reshape(CMP_RATIO, CMP_STRIDE, 2, HEAD_DIM).transpose(2, 0, 1, 3)
        pe_rows = jnp.tile(pe.reshape(2 * CMP_RATIO * CMP_STRIDE, HEAD_DIM), (1, N_KV)).astype(F32)
        wpool = w_pool[l].astype(BF16)
        pscale = pool_scale[l].reshape(1, POOL_WIDTH)
        wout = w_out[l].astype(BF16)
        wffn_in = w_ffn_in[l].astype(BF16)
        wffn_out = w_ffn_out[l].astype(BF16)
        last = l == depth - 1

        q, kvc, kvs, kvw, kvsb, kvwb, u, gates = _inproj(xp, g_mix, w_all)
        ck, cvt = _compress_prompt(kvc.reshape(bp, t, KV_COLS), w_bd, pe_rows)
        mix = _prompt_attention(q.reshape(bp, t, ATT_WIDTH), gates.reshape(bp, t, LANE), u.reshape(bp, t, POOL_WIDTH),
                                pool_zero, kvsb.reshape(bp, t, KV_COLS), kvwb.reshape(bp, t, KV_COLS), ck, cvt,
                                delta_p, mimp_p, wpool, pscale)
        xp = _ffn(xp, mix.reshape(bp * t, D_MODEL), wout, g_ffn, wffn_in, wffn_out, g_fin, last)
        outs["p_cmp"].append(kvc.reshape(kv_shape(bp, t)))
        outs["p_sel"].append(kvs.reshape(kv_shape(bp, t)))
        outs["p_win"].append(kvw.reshape(bp, t, KV_COLS)[:, t - wb:].reshape(kv_shape(bp, wb)))
        outs["p_pool"].append(u.reshape(bp, t, POOL_WIDTH)[:, t - POOL_STATE:])

        q, kvc, kvs, kvw, _, _, u, gates = _inproj(xs, g_mix, w_all)
        mix, nwin, npool = _sample_attention(
            l, page_table, q.reshape(bs, tq, ATT_WIDTH), gates.reshape(bs, tq, LANE), u.reshape(bs, tq, POOL_WIDTH),
            kvs.reshape(bs, tq, KV_COLS), kvw.reshape(bs, tq, KV_COLS), swin, state_pool, ccache, scache,
            w_bd, pe_rows, delta_s, mimp_s, expand, wpool, pscale)
        xs = _ffn(xs, mix.reshape(bs * tq, D_MODEL), wout, g_ffn, wffn_in, wffn_out, g_fin, last)
        outs["s_cmp"].append(kvc.reshape(kv_shape(bs, tq)))
        outs["s_sel"].append(kvs.reshape(kv_shape(bs, tq)))
        outs["s_win"].append(nwin.reshape(kv_shape(bs, wb)))
        outs["s_pool"].append(npool)

    return (xp.reshape(bp, t, D_MODEL), xs.reshape(bs, tq, D_MODEL),
            jnp.stack(outs["p_cmp"]), jnp.stack(outs["p_sel"]), jnp.stack(outs["p_win"]), jnp.stack(outs["p_pool"]),
            jnp.stack(outs["s_cmp"]), jnp.stack(outs["s_sel"]), jnp.stack(outs["s_win"]), jnp.stack(outs["s_pool"]))
```

```python
import functools
import math

import numpy as np
import jax
import jax.numpy as jnp
from jax import lax
from jax.experimental import pallas as pl
from jax.experimental.pallas import tpu as pltpu

D_MODEL = 1024
HEAD_DIM = 64
N_HEADS = 8
N_KV = 2
GROUP = N_HEADS // N_KV
ATT_WIDTH = N_HEADS * HEAD_DIM
POOL_WIDTH = D_MODEL - ATT_WIDTH
KV_COLS = 2 * N_KV * HEAD_DIM
KV_LANES = N_KV * HEAD_DIM
CMP_LEN = 32
CMP_STRIDE = 16
CMP_RATIO = CMP_LEN // CMP_STRIDE
SEL_BLOCK = 64
N_SELECT = 16
WINDOW = 512
N_BRANCH = 3
GATE_COLS = N_HEADS * N_BRANCH
POOL_WINDOWS = (2, 4, 8, 16)
POOL_GROUP_DIM = POOL_WIDTH // len(POOL_WINDOWS)
POOL_STATE = max(POOL_WINDOWS) - 1
POOL_HALO = 16
N_BUCKETS = 32
MAX_DISTANCE = 128
PAGE_SIZE = 128
EPS = 1e-6
NEG = -1e30
VALID_MIN = -1e29

LANE = 128
TQ = 128
FAR_TILE = 512
ROW_TILE = 512
VMEM_LIMIT = 52 * 1024 * 1024

F32 = jnp.float32
BF16 = jnp.bfloat16
NT_DIMS = (((1,), (1,)), ((), ()))


def _bucket_np(dist):
    n = np.maximum(dist, 0)
    max_exact = N_BUCKETS // 2
    nf = np.maximum(n, 1).astype(np.float32)
    large = max_exact + (np.log(nf / max_exact) / math.log(MAX_DISTANCE / max_exact)
                         * (N_BUCKETS - max_exact)).astype(np.int32)
    return np.where(n < max_exact, n, np.minimum(large, N_BUCKETS - 1)).astype(np.int32)


def _bucket_or_masked(dist, valid):
    return np.where(valid, _bucket_np(dist), -1).astype(np.int32)


FAR_DIST = TQ + 1


def _check_far_bucket(max_dist):
    assert (_bucket_np(np.arange(FAR_DIST, max_dist + 1)) == N_BUCKETS - 1).all()


def _bias_tile_kernel(bucket_ref, rb_ref, out_ref):
    bucket = bucket_ref[...]
    acc = jnp.zeros(bucket.shape, F32)
    for b in range(N_BUCKETS):
        acc = acc + jnp.where(bucket == b, rb_ref[b:b + 1, :], 0.0)
    acc = acc - rb_ref[N_BUCKETS - 1:N_BUCKETS, :]
    out_ref[...] = jnp.where(bucket < 0, NEG, acc)


def _bias_tiles(bucket_np, rb_cols):
    rows, cols = bucket_np.shape
    return pl.pallas_call(
        _bias_tile_kernel,
        out_shape=jax.ShapeDtypeStruct((rows, cols), F32),
    )(jnp.asarray(bucket_np), rb_cols)


DD0 = 0
DC0 = 2 * TQ
DC_ROWS = 24
DW0 = DC0 + 32
PROMPT_DELTA_ROWS = DW0 + TQ


def _prompt_bucket_table():
    i = np.arange(TQ)[None, :]
    kk = np.arange(2 * TQ)[:, None]
    d = i + TQ - kk
    dd = _bucket_or_masked(d, d >= 0)
    cc = np.arange(32)[:, None] - 16
    d = i - CMP_STRIDE * cc - (CMP_LEN - 1)
    dc = _bucket_or_masked(d, (d >= 0) & (cc < 8))
    j = np.arange(TQ)[:, None]
    d = i + WINDOW - j
    dw = _bucket_or_masked(d, (d >= 0) & (d < WINDOW))
    tab = np.concatenate([dd, dc, dw], axis=0)
    return np.tile(tab, (1, N_HEADS))


def _sample_bucket_table(past, tq, wb):
    col = np.arange(LANE)
    t = (col % tq)[None, :]
    colok = (col < N_HEADS * tq)[None, :]
    kk = np.arange(LANE)[:, None]
    d = LANE + t - kk
    dlast = _bucket_or_masked(d, colok & (d >= 0))
    d = t - kk
    dnew = _bucket_or_masked(d, colok & (d >= 0) & (kk < tq))
    d = wb + t - kk
    dedge = _bucket_or_masked(d, colok & (d >= 0) & (d < WINDOW))
    nsub = past // CMP_STRIDE
    c = np.arange(nsub)[:, None]
    d = past + t - CMP_STRIDE * c - (CMP_LEN - 1)
    dcmp = _bucket_or_masked(d, colok & (d >= 0) & (c < nsub - CMP_RATIO + 1))
    return np.concatenate([dlast, dnew, dedge, dcmp], axis=0)


IN_Q = 0
IN_KV = ATT_WIDTH
IN_U = IN_KV + N_BRANCH * KV_COLS
IN_G = IN_U + POOL_WIDTH
IN_COLS_PAD = IN_G + LANE


def _rms(x, g):
    return x * lax.rsqrt(jnp.mean(x * x, axis=-1, keepdims=True) + EPS) * g


def _sigmoid(x):
    return 1.0 / (1.0 + jnp.exp(-x))


def _inproj_common(x_ref, g_ref, w_ref, q_ref, u_ref, gate_ref):
    h = _rms(x_ref[...], g_ref[...]).astype(BF16)
    z = jnp.dot(h, w_ref[...], preferred_element_type=F32)
    q_ref[...] = z[:, IN_Q:IN_KV].astype(BF16)
    u_ref[...] = z[:, IN_U:IN_G]
    gate_ref[...] = _sigmoid(z[:, IN_G:IN_COLS_PAD])
    return [z[:, IN_KV + i * KV_COLS:IN_KV + (i + 1) * KV_COLS] for i in range(N_BRANCH)]


def _inproj_rows_kernel(x_ref, g_ref, w_ref, q_ref, u_ref, gate_ref, kvc_ref, kvs_ref, kvw_ref):
    kvc, kvs, kvw = _inproj_common(x_ref, g_ref, w_ref, q_ref, u_ref, gate_ref)
    kvc_ref[...] = kvc
    kvs_ref[...] = kvs
    kvw_ref[...] = kvw


def _inproj_seq_kernel(x_ref, g_ref, w_ref, cbuf_ref, sbuf_ref, q_ref, u_ref, gate_ref, kvc_ref, kb_ref,
                       vst_ref, vwt_ref, kvct_ref, kvst_ref, kvwt_ref):
    del cbuf_ref, sbuf_ref
    kvc, kvs, kvw = _inproj_common(x_ref, g_ref, w_ref, q_ref, u_ref, gate_ref)
    kvc_ref[...] = kvc
    kb_ref[...] = jnp.concatenate([kvs[:, 0:KV_LANES], kvw[:, 0:KV_LANES]], axis=1).astype(BF16)
    kvst = kvs.T
    kvwt = kvw.T
    kvct_ref[0, 0] = kvc.T
    kvst_ref[0, 0] = kvst
    kvwt_ref[0] = kvwt
    vst_ref[0] = kvst[KV_LANES:KV_COLS].astype(BF16)
    vwt_ref[0] = kvwt[KV_LANES:KV_COLS].astype(BF16)


def _inproj_rows(x, g, w):
    n = x.shape[0]
    tm = min(ROW_TILE, n)
    row = lambda c: pl.BlockSpec((tm, c), lambda i: (i, 0))
    full = lambda a: pl.BlockSpec(a.shape, lambda i: (0,) * a.ndim)
    outs = [(ATT_WIDTH, BF16), (POOL_WIDTH, F32), (LANE, F32), (KV_COLS, F32), (KV_COLS, F32), (KV_COLS, F32)]
    return pl.pallas_call(
        _inproj_rows_kernel,
        grid=(n // tm,),
        in_specs=[row(D_MODEL), full(g), full(w)],
        out_specs=[row(c) for c, _ in outs],
        out_shape=[jax.ShapeDtypeStruct((n, c), dt) for c, dt in outs],
        compiler_params=pltpu.CompilerParams(dimension_semantics=("parallel",), vmem_limit_bytes=VMEM_LIMIT),
    )(x, g, w)


def _inproj_seq(layer, x, g, w, kvct_all, kvst_all):
    n = x.shape[0]
    _, b, _, t = kvct_all.shape
    tm = min(ROW_TILE, t)
    tiles = t // tm
    row = lambda c: pl.BlockSpec((tm, c), lambda i: (i, 0))
    full = lambda a: pl.BlockSpec(a.shape, lambda i: (0,) * a.ndim)
    hbm = pl.BlockSpec(memory_space=pl.ANY)
    seq_t = lambda r: pl.BlockSpec((1, r, tm), lambda i: (i // tiles, 0, i % tiles))
    all_t = pl.BlockSpec((1, 1, KV_COLS, tm), lambda i: (layer, i // tiles, 0, i % tiles))
    rows = [(ATT_WIDTH, BF16), (POOL_WIDTH, F32), (LANE, F32), (KV_COLS, F32), (KV_COLS, BF16)]
    return pl.pallas_call(
        _inproj_seq_kernel,
        grid=(n // tm,),
        in_specs=[row(D_MODEL), full(g), full(w), hbm, hbm],
        out_specs=[row(c) for c, _ in rows] + [seq_t(KV_LANES), seq_t(KV_LANES), all_t, all_t, seq_t(KV_COLS)],
        out_shape=[jax.ShapeDtypeStruct((n, c), dt) for c, dt in rows]
        + [jax.ShapeDtypeStruct((b, KV_LANES, t), BF16), jax.ShapeDtypeStruct((b, KV_LANES, t), BF16),
           jax.ShapeDtypeStruct(kvct_all.shape, F32), jax.ShapeDtypeStruct(kvst_all.shape, F32),
           jax.ShapeDtypeStruct((b, KV_COLS, t), F32)],
        input_output_aliases={3: 7, 4: 8},
        compiler_params=pltpu.CompilerParams(dimension_semantics=("parallel",), vmem_limit_bytes=VMEM_LIMIT),
    )(x, g, w, kvct_all, kvst_all)


def _compress_tokens(x_of, w_ref, pe_ref, n_sub):
    outs = []
    for j in range(2):
        parts = [jnp.zeros((n_sub, KV_LANES), F32) for _ in range(CMP_RATIO)]
        for l in range(CMP_STRIDE):
            x = x_of(l, j)
            for r in range(CMP_RATIO):
                row = (j * CMP_RATIO + r) * CMP_STRIDE + l
                a = (x + pe_ref[row:row + 1, :]).astype(BF16)
                parts[r] = parts[r] + jnp.dot(a, w_ref[j, r, l], preferred_element_type=F32)
        comp = parts[0]
        for r in range(1, CMP_RATIO):
            comp = comp + pltpu.roll(parts[r], n_sub - r, axis=0)
        outs.append(comp)
    return outs


def _compress_kernel(x_ref, w_ref, pe_ref, ck_ref, cvt_ref):
    n_sub = x_ref.shape[1]
    x_of = lambda l, j: x_ref[0, :, l * KV_COLS + j * KV_LANES:l * KV_COLS + (j + 1) * KV_LANES]
    ck, cv = _compress_tokens(x_of, w_ref, pe_ref, n_sub)
    ck_ref[0] = ck.astype(BF16)
    cvt_ref[0] = cv.T.astype(BF16)


def _compress_prompt(kvc, w_bd, pe_rows):
    b, t, _ = kvc.shape
    n_sub = t // CMP_STRIDE
    x = kvc.reshape(b, n_sub, CMP_STRIDE * KV_COLS)
    full = lambda a: pl.BlockSpec(a.shape, lambda i: (0,) * a.ndim)
    return pl.pallas_call(
        _compress_kernel,
        grid=(b,),
        in_specs=[pl.BlockSpec((1, n_sub, CMP_STRIDE * KV_COLS), lambda i: (i, 0, 0)), full(w_bd), full(pe_rows)],
        out_specs=[pl.BlockSpec((1, n_sub, KV_LANES), lambda i: (i, 0, 0)),
                   pl.BlockSpec((1, KV_LANES, n_sub), lambda i: (i, 0, 0))],
        out_shape=[jax.ShapeDtypeStruct((b, n_sub, KV_LANES), BF16), jax.ShapeDtypeStruct((b, KV_LANES, n_sub), BF16)],
        compiler_params=pltpu.CompilerParams(dimension_semantics=("parallel",), vmem_limit_bytes=VMEM_LIMIT),
    )(x, w_bd, pe_rows)


def _split_dot(m, p):
    hi = p.astype(BF16)
    lo = (p - hi.astype(F32)).astype(BF16)
    return jnp.dot(m, hi, preferred_element_type=F32) + jnp.dot(m, lo, preferred_element_type=F32)


def _pool_mix(ext_ref, u, pos, wpool_ref, pscale_ref, rows):
    outs = []
    for k, w in enumerate(POOL_WINDOWS):
        ln = slice(k * POOL_GROUP_DIM, (k + 1) * POOL_GROUP_DIM)
        ws = ext_ref[POOL_HALO:POOL_HALO + rows, ln]
        for s in range(1, w):
            ws = ws + ext_ref[POOL_HALO - s:POOL_HALO - s + rows, ln]
        cnt = jnp.minimum(pos + 1, w).astype(F32)
        pooled = (ws / cnt - u[:, ln]).astype(BF16)
        o = jnp.dot(pooled, wpool_ref[k], preferred_element_type=F32) * pscale_ref[:, ln]
        outs.append(o.astype(BF16))
    return outs


def _prompt_attn_kernel(q_ref, gate_ref, u_ref, uh_ref, pst_ref, ksel_ref, kwin_ref, vst_ref, vwt_ref,
                        ck_ref, cvt_ref, delta_ref, mimp_ref, wpool_ref, pscale_ref,
                        mix_ref,
                        sc_s, sel_s, self_s, acc_s, m_s, l_s, ext_s):
    qb = pl.program_id(1)
    t_len = ksel_ref.shape[1]
    n_cmp = ck_ref.shape[1]
    n_sel = t_len // SEL_BLOCK
    cols = N_HEADS * TQ
    blk_per_q = TQ // SEL_BLOCK
    blk_per_far = FAR_TILE // SEL_BLOCK

    q = q_ref[0].astype(F32)
    zeros = jnp.zeros((TQ, HEAD_DIM), F32)
    blocks = []
    for h in range(N_HEADS):
        qh = q[:, h * HEAD_DIM:(h + 1) * HEAD_DIM]
        halves = [zeros] * N_KV
        halves[h // GROUP] = qh
        blocks.append(jnp.concatenate(halves, axis=1))
    qbd = jnp.concatenate(blocks, axis=0).astype(BF16)

    def scores(k_rows):
        return lax.dot_general(k_rows, qbd, NT_DIMS, preferred_element_type=F32)

    sc = scores(ck_ref[0])
    crow = lax.broadcasted_iota(jnp.int32, (n_cmp, cols), 0)
    sc = jnp.where(crow >= 8 * qb + 8, NEG, sc)
    sc_s[0:16, :] = jnp.zeros((16, cols), F32)
    sc_s[16 + n_cmp:16 + n_cmp + 8, :] = jnp.zeros((8, cols), F32)
    sc_s[16:16 + n_cmp, :] = sc
    near = pl.ds(pl.multiple_of(8 * qb, 8), DC_ROWS)
    sc_s[near, :] = sc_s[near, :] + delta_ref[DC0:DC0 + DC_ROWS, :]
    sc = sc_s[16:16 + n_cmp, :]
    m_c = jnp.max(sc, axis=0, keepdims=True)
    p = jnp.exp(sc - m_c)
    l_c = jnp.sum(p, axis=0, keepdims=True)
    pn = p * jnp.where(m_c > VALID_MIN, 1.0 / l_c, 0.0)
    o_cmp = jnp.dot(cvt_ref[0], pn.astype(BF16), preferred_element_type=F32)

    jidx = lax.broadcasted_iota(jnp.int32, (n_sel, TQ), 0)
    qi = lax.broadcasted_iota(jnp.int32, (n_sel, TQ), 1)
    cur = blk_per_q * qb + qi // SEL_BLOCK
    forced = (jidx == 0) | (jidx == cur) | (jidx == cur - 1)
    sel_g = []
    for g in range(N_KV):
        ps = pn[:, (g * GROUP) * TQ:(g * GROUP + 1) * TQ]
        for r in range(1, GROUP):
            ps = ps + pn[:, (g * GROUP + r) * TQ:(g * GROUP + r + 1) * TQ]
        imp = _split_dot(mimp_ref[...], ps)
        score = jnp.where(forced, jnp.inf, imp)
        score = jnp.where(jidx > cur, -jnp.inf, score)
        cnt = jnp.zeros((n_sel, TQ), jnp.int32)
        for jp in range(n_sel):
            row = score[jp:jp + 1, :]
            cnt = cnt + jnp.where(jidx > jp, jnp.where(row >= score, 1, 0), jnp.where(row > score, 1, 0))
        sel = (cnt < min(N_SELECT, n_sel)) & (jidx <= cur)
        sel_g.append(jnp.where(sel, 1.0, 0.0))
    sel_full = jnp.concatenate([sel_g[h // GROUP] for h in range(N_HEADS)], axis=1)
    prev_blk = blk_per_q * jnp.maximum(qb - 1, 0)
    for j in range(n_sel):
        row = sel_full[j:j + 1, :]
        sel_s[j] = row
        self_s[j] = jnp.where(j < prev_blk, row, 0.0)

    m_s[...] = jnp.full((1, cols), NEG, F32)
    l_s[...] = jnp.zeros((1, cols), F32)
    acc_s[...] = jnp.zeros((KV_LANES, cols), F32)

    def online_update(s, vt):
        m_old = m_s[...]
        m_new = jnp.maximum(m_old, jnp.max(s, axis=0, keepdims=True))
        alpha = jnp.exp(m_old - m_new)
        p = jnp.exp(s - m_new)
        l_s[...] = alpha * l_s[...] + jnp.sum(p, axis=0, keepdims=True)
        acc_s[...] = alpha * acc_s[...] + jnp.dot(vt, p.astype(BF16), preferred_element_type=F32)
        m_s[...] = m_new

    def masked(s, mask3):
        nb = mask3.shape[0]
        s3 = s.reshape(nb, SEL_BLOCK, cols)
        return jnp.where(mask3 > 0.5, s3, NEG).reshape(nb * SEL_BLOCK, cols)

    prev_key = TQ * jnp.maximum(qb - 1, 0)
    n_far = (prev_key + FAR_TILE - 1) // FAR_TILE

    def far_body(t, carry):
        k0 = pl.multiple_of(t * FAR_TILE, FAR_TILE)
        s = scores(ksel_ref[0, pl.ds(k0, FAR_TILE), :])
        s = masked(s, self_s[pl.ds(t * blk_per_far, blk_per_far)])
        online_update(s, vst_ref[0, :, pl.ds(k0, FAR_TILE)])
        return carry

    lax.fori_loop(0, n_far, far_body, 0)

    k_prev = pl.multiple_of(prev_key, TQ)
    k_diag = pl.multiple_of(qb * TQ, TQ)
    no_prev = jnp.where(qb == 0, NEG, 0.0)
    d_prev = delta_ref[DD0:DD0 + TQ, :]
    d_diag = delta_ref[DD0 + TQ:DD0 + 2 * TQ, :]
    s_prev = masked(scores(ksel_ref[0, pl.ds(k_prev, TQ), :]) + d_prev + no_prev, sel_s[pl.ds(prev_blk, blk_per_q)])
    s_diag = masked(scores(ksel_ref[0, pl.ds(k_diag, TQ), :]) + d_diag, sel_s[pl.ds(blk_per_q * qb, blk_per_q)])
    vt_near = jnp.concatenate([vst_ref[0, :, pl.ds(k_prev, TQ)], vst_ref[0, :, pl.ds(k_diag, TQ)]], axis=1)
    online_update(jnp.concatenate([s_prev, s_diag], axis=0), vt_near)
    l_sel = l_s[...]
    o_sel = acc_s[...]

    n_wc = WINDOW // TQ + 1
    s_w, k_w = [], []
    for c in range(n_wc):
        pos0 = (qb - (n_wc - 1) + c) * TQ
        k0 = pl.multiple_of(jnp.maximum(pos0, 0), TQ)
        s = scores(kwin_ref[0, pl.ds(k0, TQ), :])
        if c == 0:
            s = s + delta_ref[DW0:DW0 + TQ, :]
        elif c == n_wc - 2:
            s = s + d_prev
        elif c == n_wc - 1:
            s = s + d_diag
        if c < n_wc - 1:
            s = s + jnp.where(pos0 < 0, NEG, 0.0)
        s_w.append(s)
        k_w.append(k0)
    m_w = s_w[0].max(axis=0, keepdims=True)
    for s in s_w[1:]:
        m_w = jnp.maximum(m_w, s.max(axis=0, keepdims=True))
    l_win = jnp.zeros((1, cols), F32)
    o_win = jnp.zeros((KV_LANES, cols), F32)
    for s, k0 in zip(s_w, k_w):
        p = jnp.exp(s - m_w)
        l_win = l_win + jnp.sum(p, axis=0, keepdims=True)
        o_win = o_win + jnp.dot(vwt_ref[0, :, pl.ds(k0, TQ)], p.astype(BF16), preferred_element_type=F32)

    gt = gate_ref[0].T
    gate_row = lambda x: jnp.concatenate([gt[N_BRANCH * h + x:N_BRANCH * h + x + 1, :] for h in range(N_HEADS)], axis=1)
    o_t = o_cmp * gate_row(0) + o_sel * (gate_row(1) / l_sel) + o_win * (gate_row(2) / l_win)
    pieces = []
    for h in range(N_HEADS):
        g = h // GROUP
        pieces.append(o_t[g * HEAD_DIM:(g + 1) * HEAD_DIM, h * TQ:(h + 1) * TQ])
    mix_ref[0, :, 0:ATT_WIDTH] = jnp.concatenate(pieces, axis=0).T.astype(BF16)

    u = u_ref[0]
    ext_s[0:POOL_HALO, :] = jnp.where(qb == 0, pst_ref[0], uh_ref[0])
    ext_s[POOL_HALO:POOL_HALO + TQ, :] = u
    pos = qb * TQ + lax.broadcasted_iota(jnp.int32, (TQ, 1), 0)
    for k, o in enumerate(_pool_mix(ext_s, u, pos, wpool_ref, pscale_ref, TQ)):
        mix_ref[0, :, ATT_WIDTH + k * POOL_GROUP_DIM:ATT_WIDTH + (k + 1) * POOL_GROUP_DIM] = o


def _prompt_attention(q, gates, u, pool_prev, kb, vst, vwt, ck, cvt, delta, mimp, wpool, pscale):
    b, t, _ = q.shape
    assert t % FAR_TILE == 0 and t >= WINDOW + TQ
    n_sel = t // SEL_BLOCK
    n_cmp = ck.shape[1]
    cols = N_HEADS * TQ
    halo_per_q = TQ // POOL_HALO
    full = lambda a: pl.BlockSpec(a.shape, lambda i, j: (0,) * a.ndim)
    qblk = lambda c: pl.BlockSpec((1, TQ, c), lambda i, j: (i, j, 0))
    seq = lambda c: pl.BlockSpec((1, t, KV_LANES), lambda i, j: (i, 0, c))
    per_b = lambda a: pl.BlockSpec((1,) + a.shape[1:], lambda i, j: (i,) + (0,) * (a.ndim - 1))
    return pl.pallas_call(
        _prompt_attn_kernel,
        grid=(b, t // TQ),
        in_specs=[qblk(ATT_WIDTH), qblk(LANE), qblk(POOL_WIDTH),
                  pl.BlockSpec((1, POOL_HALO, POOL_WIDTH), lambda i, j: (i, jnp.maximum(j * halo_per_q - 1, 0), 0)),
                  per_b(pool_prev), seq(0), seq(1), per_b(vst), per_b(vwt), per_b(ck), per_b(cvt),
                  full(delta), full(mimp), full(wpool), full(pscale)],
        out_specs=pl.BlockSpec((1, TQ, D_MODEL), lambda i, j: (i, j, 0)),
        out_shape=jax.ShapeDtypeStruct((b, t, D_MODEL), BF16),
        scratch_shapes=[pltpu.VMEM((16 + n_cmp + 8, cols), F32),
                        pltpu.VMEM((n_sel, 1, cols), F32), pltpu.VMEM((n_sel, 1, cols), F32),
                        pltpu.VMEM((KV_LANES, cols), F32), pltpu.VMEM((1, cols), F32), pltpu.VMEM((1, cols), F32),
                        pltpu.VMEM((POOL_HALO + TQ, POOL_WIDTH), F32)],
        compiler_params=pltpu.CompilerParams(dimension_semantics=("parallel", "arbitrary"),
                                             vmem_limit_bytes=VMEM_LIMIT),
    )(q, gates, u, u, pool_prev, kb, kb, vst, vwt, ck, cvt, delta, mimp, wpool, pscale)


SD_LAST = 0
SD_NEW = LANE
SD_EDGE = 2 * LANE
SD_CMP = 3 * LANE


def _sample_attn_kernel(pt_ref, q_ref, gate_ref, u_ref, kvs_ref, kvw_ref, swin_ref, spool_ref,
                        ccache_ref, scache_ref, nwin_in_ref, wcmp_ref, pe_ref, delta_ref, mimp_ref, expand_ref,
                        wpool_ref, pscale_ref,
                        mix_ref, nwin_ref, npool_ref,
                        cbuf, sbuf, xk_s, xv_s, sem, ext_s, *, layer):
    del nwin_in_ref
    b = pl.program_id(0)
    nb = pl.num_programs(0)
    n_pages = pt_ref.shape[1]
    past = n_pages * PAGE_SIZE
    n_sub = past // CMP_STRIDE
    tq = q_ref.shape[1]
    rows = N_HEADS * tq
    wb = swin_ref.shape[3]
    n_sel = past // SEL_BLOCK + 1
    slot = lax.rem(b, 2)

    def page_copies(seq, slt):
        cps = []
        for p in range(n_pages):
            pg = pt_ref[seq, p]
            dst = (slice(None), pl.ds(p * PAGE_SIZE, PAGE_SIZE))
            cps.append(pltpu.make_async_copy(ccache_ref.at[layer, pg], cbuf.at[slt].at[dst], sem.at[0, slt]))
            cps.append(pltpu.make_async_copy(scache_ref.at[layer, pg], sbuf.at[slt].at[dst], sem.at[1, slt]))
        return cps

    @pl.when(b == 0)
    def _():
        for cp in page_copies(0, 0):
            cp.start()

    @pl.when(b + 1 < nb)
    def _():
        for cp in page_copies(b + 1, 1 - slot):
            cp.start()

    for cp in page_copies(b, slot):
        cp.wait()

    q = q_ref[0].astype(F32)
    zeros = jnp.zeros((tq, HEAD_DIM), F32)
    blocks = []
    for h in range(N_HEADS):
        halves = [zeros] * N_KV
        halves[h // GROUP] = q[:, h * HEAD_DIM:(h + 1) * HEAD_DIM]
        blocks.append(jnp.concatenate(halves, axis=1))
    qbd = jnp.concatenate(blocks, axis=0).astype(BF16)

    def scores(k_rows):
        return lax.dot_general(qbd, k_rows, NT_DIMS, preferred_element_type=F32)

    def scores_t(k_t):
        return jnp.dot(qbd, k_t.astype(BF16), preferred_element_type=F32)

    def pad_rows(x):
        return jnp.concatenate([x, jnp.zeros((LANE - tq, KV_LANES), F32)], axis=0)

    pad_new = lambda x: pad_rows(x).astype(BF16)
    d_last = delta_ref[:, SD_LAST:SD_LAST + LANE]
    d_new = delta_ref[:, SD_NEW:SD_NEW + LANE]

    for p in range(n_pages):
        tok = slice(p * PAGE_SIZE, (p + 1) * PAGE_SIZE)
        xk_s[tok, :] = cbuf[slot, 0:KV_LANES, tok].T
        xv_s[tok, :] = cbuf[slot, KV_LANES:KV_COLS, tok].T
    x_of = lambda l, j: (xk_s, xv_s)[j][pl.ds(l, n_sub, stride=CMP_STRIDE), :]
    ck, cv = _compress_tokens(x_of, wcmp_ref, pe_ref, n_sub)
    sc = scores(ck.astype(BF16)) + delta_ref[:, SD_CMP:SD_CMP + n_sub]
    m_c = jnp.max(sc, axis=1, keepdims=True)
    p = jnp.exp(sc - m_c)
    l_c = jnp.sum(p, axis=1, keepdims=True)
    pn = p * jnp.where(m_c > VALID_MIN, 1.0 / l_c, 0.0)
    o_cmp = jnp.dot(pn.astype(BF16), cv.astype(BF16), preferred_element_type=F32)

    jidx = lax.broadcasted_iota(jnp.int32, (tq, LANE), 1)
    cur = (past + lax.broadcasted_iota(jnp.int32, (tq, LANE), 0)) // SEL_BLOCK
    forced = (jidx == 0) | (jidx == cur) | (jidx == cur - 1)
    sel_g = []
    for g in range(N_KV):
        ps = pn[g * GROUP * tq:(g * GROUP + 1) * tq, :]
        for r in range(1, GROUP):
            ps = ps + pn[(g * GROUP + r) * tq:(g * GROUP + r + 1) * tq, :]
        hi = ps.astype(BF16)
        lo = (ps - hi.astype(F32)).astype(BF16)
        imp = (jnp.dot(hi, mimp_ref[...], preferred_element_type=F32)
               + jnp.dot(lo, mimp_ref[...], preferred_element_type=F32))
        score = jnp.where(forced, jnp.inf, imp)
        score = jnp.where((jidx > cur) | (jidx >= n_sel), -jnp.inf, score)
        cnt = jnp.zeros((tq, LANE), jnp.int32)
        for jp in range(n_sel):
            col = score[:, jp:jp + 1]
            cnt = cnt + jnp.where(jidx > jp, jnp.where(col >= score, 1, 0), jnp.where(col > score, 1, 0))
        sel = (cnt < min(N_SELECT, n_sel)) & (jidx <= cur) & (jidx < n_sel)
        sel_g.append(jnp.where(sel, 1.0, 0.0))
    sel_rows = jnp.concatenate([sel_g[h // GROUP] for h in range(N_HEADS)], axis=0).astype(BF16)
    mask = jnp.dot(sel_rows, expand_ref[...], preferred_element_type=F32)

    def softmax_pv(s_old, vt_old, s_new, v_new):
        m = jnp.maximum(s_old.max(axis=1, keepdims=True), s_new.max(axis=1, keepdims=True))
        p_old = jnp.exp(s_old - m)
        p_new = jnp.exp(s_new - m)
        l = jnp.sum(p_old, axis=1, keepdims=True) + jnp.sum(p_new, axis=1, keepdims=True)
        o = (lax.dot_general(p_old.astype(BF16), vt_old.astype(BF16), NT_DIMS, preferred_element_type=F32)
             + jnp.dot(p_new.astype(BF16), v_new, preferred_element_type=F32))
        return o, l

    s_far = scores_t(sbuf[slot, 0:KV_LANES, 0:past - LANE])
    s_last = scores_t(sbuf[slot, 0:KV_LANES, past - LANE:past]) + d_last
    s_past = jnp.where(mask > 0.5, jnp.concatenate([s_far, s_last], axis=1), NEG)
    kvs_new = kvs_ref[0]
    s_new = scores(pad_new(kvs_new[:, 0:KV_LANES])) + d_new
    o_sel, l_sel = softmax_pv(s_past, sbuf[slot, KV_LANES:KV_COLS, :], s_new, pad_new(kvs_new[:, KV_LANES:KV_COLS]))

    kvw_new = kvw_ref[0]
    s_first = scores_t(swin_ref[0, 0, 0:KV_LANES, 0:LANE]) + delta_ref[:, SD_EDGE:SD_EDGE + LANE]
    s_mid = scores_t(swin_ref[0, 0, 0:KV_LANES, LANE:wb - LANE])
    s_lastw = scores_t(swin_ref[0, 0, 0:KV_LANES, wb - LANE:wb]) + d_last
    s_wnew = scores(pad_new(kvw_new[:, 0:KV_LANES])) + d_new
    o_win, l_win = softmax_pv(jnp.concatenate([s_first, s_mid, s_lastw], axis=1), swin_ref[0, 0, KV_LANES:KV_COLS, :],
                              s_wnew, pad_new(kvw_new[:, KV_LANES:KV_COLS]))

    shifted = pltpu.roll(swin_ref[0, 0], wb - tq, axis=1)
    new_t = jnp.concatenate([pad_rows(kvw_new[:, 0:KV_LANES]).T, pad_rows(kvw_new[:, KV_LANES:KV_COLS]).T], axis=0)
    new_t = pltpu.roll(new_t, LANE - tq, axis=1)
    lane = lax.broadcasted_iota(jnp.int32, (KV_COLS, LANE), 1)
    nwin_ref[0, 0, :, 0:wb - LANE] = shifted[:, 0:wb - LANE]
    nwin_ref[0, 0, :, wb - LANE:wb] = jnp.where(lane >= LANE - tq, new_t, shifted[:, wb - LANE:wb])

    gates = gate_ref[0]
    gate_col = lambda x: jnp.concatenate([gates[:, N_BRANCH * h + x:N_BRANCH * h + x + 1] for h in range(N_HEADS)], axis=0)
    o = o_cmp * gate_col(0) + o_sel * (gate_col(1) / l_sel) + o_win * (gate_col(2) / l_win)
    pieces = []
    for h in range(N_HEADS):
        g = h // GROUP
        pieces.append(o[h * tq:(h + 1) * tq, g * HEAD_DIM:(g + 1) * HEAD_DIM])
    mix_ref[0, :, 0:ATT_WIDTH] = jnp.concatenate(pieces, axis=1).astype(BF16)

    u = u_ref[0]
    ext_s[0:1, :] = jnp.zeros((1, POOL_WIDTH), F32)
    ext_s[1:POOL_HALO, :] = spool_ref[0, 0]
    ext_s[POOL_HALO:POOL_HALO + tq, :] = u
    pos = past + lax.broadcasted_iota(jnp.int32, (tq, 1), 0)
    for k, o_k in enumerate(_pool_mix(ext_s, u, pos, wpool_ref, pscale_ref, tq)):
        mix_ref[0, :, ATT_WIDTH + k * POOL_GROUP_DIM:ATT_WIDTH + (k + 1) * POOL_GROUP_DIM] = o_k
    npool_ref[0] = ext_s[POOL_HALO + tq - POOL_STATE:POOL_HALO + tq, :]


def _sample_attention(layer, page_table, q, gates, u, kvs, kvw, swin_t, state_pool, ccache_t, scache_t, nwin_all,
                      wcmp, pe_rows, delta, mimp, expand, wpool, pscale):
    nb, tq, _ = q.shape
    n_pages = page_table.shape[1]
    past = n_pages * PAGE_SIZE
    n_sub = past // CMP_STRIDE
    wb = swin_t.shape[3]
    assert wb == WINDOW and wb >= 3 * LANE and tq <= CMP_STRIDE and POOL_HALO + tq - POOL_STATE >= 0
    assert past // SEL_BLOCK + 1 <= LANE and past % LANE == 0 and n_sub <= LANE
    full = lambda a: pl.BlockSpec(a.shape, lambda i, pt: (0,) * a.ndim)
    per_b = lambda c: pl.BlockSpec((1, tq, c), lambda i, pt: (i, 0, 0))
    layer_b = lambda a: pl.BlockSpec((1, 1) + a.shape[2:], lambda i, pt: (layer, i) + (0,) * (a.ndim - 2))
    hbm = pl.BlockSpec(memory_space=pl.ANY)
    grid_spec = pltpu.PrefetchScalarGridSpec(
        num_scalar_prefetch=1,
        grid=(nb,),
        in_specs=[per_b(ATT_WIDTH), per_b(LANE), per_b(POOL_WIDTH), per_b(KV_COLS), per_b(KV_COLS),
                  layer_b(swin_t), layer_b(state_pool), hbm, hbm, hbm,
                  full(wcmp), full(pe_rows), full(delta), full(mimp), full(expand), full(wpool), full(pscale)],
        out_specs=[pl.BlockSpec((1, tq, D_MODEL), lambda i, pt: (i, 0, 0)),
                   layer_b(nwin_all),
                   pl.BlockSpec((1, POOL_STATE, POOL_WIDTH), lambda i, pt: (i, 0, 0))],
        scratch_shapes=[pltpu.VMEM((2, KV_COLS, past), F32), pltpu.VMEM((2, KV_COLS, past), F32),
                        pltpu.VMEM((past, KV_LANES), F32), pltpu.VMEM((past, KV_LANES), F32),
                        pltpu.SemaphoreType.DMA((2, 2)), pltpu.VMEM((POOL_HALO + tq, POOL_WIDTH), F32)],
    )
    return pl.pallas_call(
        functools.partial(_sample_attn_kernel, layer=layer),
        grid_spec=grid_spec,
        out_shape=[jax.ShapeDtypeStruct((nb, tq, D_MODEL), BF16),
                   jax.ShapeDtypeStruct(nwin_all.shape, F32),
                   jax.ShapeDtypeStruct((nb, POOL_STATE, POOL_WIDTH), F32)],
        input_output_aliases={10: 1},
        compiler_params=pltpu.CompilerParams(dimension_semantics=("arbitrary",), vmem_limit_bytes=VMEM_LIMIT),
    )(page_table, q, gates, u, kvs, kvw, swin_t, state_pool, ccache_t, scache_t, nwin_all,
      wcmp, pe_rows, delta, mimp, expand, wpool, pscale)


def _ffn_kernel(x_ref, mix_ref, wout_ref, gn_ref, wg_ref, wu_ref, wd_ref, gf_ref, o_ref, xmid_s, h_s, acc_s, *,
                final_norm):
    c = pl.program_id(1)

    @pl.when(c == 0)
    def _():
        xm = x_ref[...] + jnp.dot(mix_ref[...], wout_ref[...], preferred_element_type=F32)
        xmid_s[...] = xm
        h_s[...] = _rms(xm, gn_ref[...]).astype(BF16)
        acc_s[...] = jnp.zeros(acc_s.shape, F32)

    h = h_s[...]
    gate = jnp.dot(h, wg_ref[...], preferred_element_type=F32)
    up = jnp.dot(h, wu_ref[...], preferred_element_type=F32)
    act = (gate * _sigmoid(gate) * up).astype(BF16)
    acc_s[...] += jnp.dot(act, wd_ref[...], preferred_element_type=F32)

    @pl.when(c == pl.num_programs(1) - 1)
    def _():
        y = xmid_s[...] + acc_s[...]
        o_ref[...] = _rms(y, gf_ref[...]) if final_norm else y


def _ffn(x, mix, wout, gn, wffn_in, wffn_out, gf, final_norm):
    n = x.shape[0]
    tm = min(ROW_TILE, n)
    d_ff = wffn_out.shape[0]
    n_chunk = 2
    tf = d_ff // n_chunk
    assert tf % LANE == 0 and n % tm == 0
    full = lambda a: pl.BlockSpec(a.shape, lambda i, c: (0,) * a.ndim)
    row = lambda cdim: pl.BlockSpec((tm, cdim), lambda i, c: (i, 0))
    return pl.pallas_call(
        functools.partial(_ffn_kernel, final_norm=final_norm),
        grid=(n // tm, n_chunk),
        in_specs=[row(D_MODEL), row(D_MODEL), full(wout), full(gn),
                  pl.BlockSpec((D_MODEL, tf), lambda i, c: (0, c)),
                  pl.BlockSpec((D_MODEL, tf), lambda i, c: (0, n_chunk + c)),
                  pl.BlockSpec((tf, D_MODEL), lambda i, c: (c, 0)), full(gf)],
        out_specs=row(D_MODEL),
        out_shape=jax.ShapeDtypeStruct((n, D_MODEL), F32),
        scratch_shapes=[pltpu.VMEM((tm, D_MODEL), F32), pltpu.VMEM((tm, D_MODEL), BF16), pltpu.VMEM((tm, D_MODEL), F32)],
        compiler_params=pltpu.CompilerParams(dimension_semantics=("parallel", "arbitrary"),
                                             vmem_limit_bytes=VMEM_LIMIT),
    )(x, mix, wout, gn, wffn_in, wffn_in, wffn_out, gf)


def _importance_matrix(n_sel, n_cmp_rows, n_cmp):
    spb = SEL_BLOCK // CMP_STRIDE
    j = np.arange(n_sel)[:, None]
    c = np.arange(n_cmp_rows)[None, :]
    return ((c >= spb * j - (CMP_RATIO - 1)) & (c <= spb * j + spb - 1) & (c < n_cmp)).astype(np.float32)


def kernel(x_prompt, x_sample, cache_cmp, cache_sel, state_win, state_pool, page_table, rel_bias, norm_mix, norm_ffn,
           norm_final, w_in, w_out, cmp_pos, w_cmp, w_pool, pool_scale, w_ffn_in, w_ffn_out):
    bp, t, _ = x_prompt.shape
    bs, tq, _ = x_sample.shape
    depth = w_in.shape[0]
    n_pages = page_table.shape[1]
    past = n_pages * PAGE_SIZE
    n_phys = cache_cmp.shape[1]
    wb = state_win.shape[2]
    _check_far_bucket(max(t, past + tq) + WINDOW)

    rb_prompt = jnp.repeat(rel_bias.astype(F32), TQ, axis=1)
    delta_p = _bias_tiles(_prompt_bucket_table(), rb_prompt)
    rb_sample = jnp.pad(jnp.repeat(rel_bias.astype(F32), tq, axis=1), ((0, 0), (0, LANE - N_HEADS * tq)))
    delta_s = _bias_tiles(_sample_bucket_table(past, tq, wb), rb_sample).T[:N_HEADS * tq]

    n_sub_p = t // CMP_STRIDE
    mimp_p = jnp.asarray(_importance_matrix(t // SEL_BLOCK, n_sub_p, n_sub_p - CMP_RATIO + 1), BF16)
    n_sub_s = past // CMP_STRIDE
    mimp_s = jnp.asarray(_importance_matrix(LANE, n_sub_s, n_sub_s - CMP_RATIO + 1).T, BF16)
    expand = jnp.asarray((np.arange(LANE)[:, None] == np.arange(past)[None, :] // SEL_BLOCK).astype(np.float32), BF16)

    pool_zero = jnp.zeros((bp, POOL_HALO, POOL_WIDTH), F32)
    feature_major = lambda a: jnp.moveaxis(a, 2, -1).reshape(a.shape[0], a.shape[1], KV_COLS, a.shape[2])
    token_major = lambda a: jnp.moveaxis(a.reshape(a.shape[:2] + (2, N_KV, HEAD_DIM, a.shape[3])), -1, 2)
    ccache_t = feature_major(cache_cmp)
    scache_t = feature_major(cache_sel)
    swin_t = feature_major(state_win)
    kvct_all = jnp.zeros((depth, bp, KV_COLS, t), F32)
    kvst_all = jnp.zeros((depth, bp, KV_COLS, t), F32)
    nwin_all = jnp.zeros((depth, bs, KV_COLS, wb), F32)

    xp = x_prompt.reshape(bp * t, D_MODEL)
    xs = x_sample.reshape(bs * tq, D_MODEL)
    outs = {k: [] for k in ("p_win", "p_pool", "s_cmp", "s_sel", "s_pool")}
    kv_shape = lambda b, n: (b, n, 2, N_KV, HEAD_DIM)
    off_g = ATT_WIDTH + N_BRANCH * KV_COLS
    for l in range(depth):
        wl = w_in[l]
        w_all = jnp.concatenate([wl[:, :ATT_WIDTH] * HEAD_DIM ** -0.5, wl[:, ATT_WIDTH:off_g],
                                 wl[:, off_g + GATE_COLS:], wl[:, off_g:off_g + GATE_COLS],
                                 jnp.zeros((D_MODEL, LANE - GATE_COLS), F32)], axis=1).astype(BF16)
        g_mix = norm_mix[l].reshape(1, D_MODEL)
        g_ffn = norm_ffn[l].reshape(1, D_MODEL)
        g_fin = norm_final.reshape(1, D_MODEL)
        wc = w_cmp[l].reshape(2, CMP_RATIO, CMP_STRIDE, HEAD_DIM, HEAD_DIM)
        zero = jnp.zeros_like(wc)
        w_bd = jnp.concatenate([jnp.concatenate([wc, zero], axis=-1), jnp.concatenate([zero, wc], axis=-1)],
                               axis=-2).astype(BF16)
        pe = cmp_pos[l].reshape(CMP_RATIO, CMP_STRIDE, 2, HEAD_DIM).transpose(2, 0, 1, 3)
        pe_rows = jnp.tile(pe.reshape(2 * CMP_RATIO * CMP_STRIDE, HEAD_DIM), (1, N_KV)).astype(F32)
        wpool = w_pool[l].astype(BF16)
        pscale = pool_scale[l].reshape(1, POOL_WIDTH)
        wout = w_out[l].astype(BF16)
        wffn_in = w_ffn_in[l].astype(BF16)
        wffn_out = w_ffn_out[l].astype(BF16)
        last = l == depth - 1

        q, u, gates, kvc, kb, vst, vwt, kvct_all, kvst_all, kvwt = _inproj_seq(l, xp, g_mix, w_all, kvct_all, kvst_all)
        ck, cvt = _compress_prompt(kvc.reshape(bp, t, KV_COLS), w_bd, pe_rows)
        mix = _prompt_attention(q.reshape(bp, t, ATT_WIDTH), gates.reshape(bp, t, LANE), u.reshape(bp, t, POOL_WIDTH),
                                pool_zero, kb.reshape(bp, t, KV_COLS), vst, vwt, ck, cvt, delta_p, mimp_p, wpool, pscale)
        xp = _ffn(xp, mix.reshape(bp * t, D_MODEL), wout, g_ffn, wffn_in, wffn_out, g_fin, last)
        outs["p_win"].append(kvwt[:, :, t - wb:])
        outs["p_pool"].append(u.reshape(bp, t, POOL_WIDTH)[:, t - POOL_STATE:])

        q, u, gates, kvc, kvs, kvw = _inproj_rows(xs, g_mix, w_all)
        mix, nwin_all, npool = _sample_attention(
            l, page_table, q.reshape(bs, tq, ATT_WIDTH), gates.reshape(bs, tq, LANE), u.reshape(bs, tq, POOL_WIDTH),
            kvs.reshape(bs, tq, KV_COLS), kvw.reshape(bs, tq, KV_COLS), swin_t, state_pool, ccache_t, scache_t,
            nwin_all, w_bd, pe_rows, delta_s, mimp_s, expand, wpool, pscale)
        xs = _ffn(xs, mix.reshape(bs * tq, D_MODEL), wout, g_ffn, wffn_in, wffn_out, g_fin, last)
        outs["s_cmp"].append(kvc.reshape(kv_shape(bs, tq)))
        outs["s_sel"].append(kvs.reshape(kv_shape(bs, tq)))
        outs["s_pool"].append(npool)

    return (xp.reshape(bp, t, D_MODEL), xs.reshape(bs, tq, D_MODEL),
            token_major(kvct_all), token_major(kvst_all), token_major(jnp.stack(outs["p_win"])),
            jnp.stack(outs["p_pool"]), jnp.stack(outs["s_cmp"]), jnp.stack(outs["s_sel"]), token_major(nwin_all),
            jnp.stack(outs["s_pool"]))
```

```python
import functools
import math

import numpy as np
import jax
import jax.numpy as jnp
from jax import lax
from jax.experimental import pallas as pl
from jax.experimental.pallas import tpu as pltpu

D_MODEL = 1024
HEAD_DIM = 64
N_HEADS = 8
N_KV = 2
GROUP = N_HEADS // N_KV
ATT_WIDTH = N_HEADS * HEAD_DIM
POOL_WIDTH = D_MODEL - ATT_WIDTH
KV_COLS = 2 * N_KV * HEAD_DIM
KV_LANES = N_KV * HEAD_DIM
CMP_LEN = 32
CMP_STRIDE = 16
CMP_RATIO = CMP_LEN // CMP_STRIDE
SEL_BLOCK = 64
N_SELECT = 16
WINDOW = 512
N_BRANCH = 3
GATE_COLS = N_HEADS * N_BRANCH
POOL_WINDOWS = (2, 4, 8, 16)
POOL_GROUP_DIM = POOL_WIDTH // len(POOL_WINDOWS)
POOL_STATE = max(POOL_WINDOWS) - 1
POOL_HALO = 16
N_BUCKETS = 32
MAX_DISTANCE = 128
PAGE_SIZE = 128
EPS = 1e-6
NEG = -1e30
VALID_MIN = -1e29

LANE = 128
TQ = 128
FAR_TILE = 512
CHAIN_COLS = 256
LOG2E = 1.4426950408889634
ROW_TILE = 512
VMEM_LIMIT = 52 * 1024 * 1024

F32 = jnp.float32
BF16 = jnp.bfloat16
NT_DIMS = (((1,), (1,)), ((), ()))


def _bucket_np(dist):
    n = np.maximum(dist, 0)
    max_exact = N_BUCKETS // 2
    nf = np.maximum(n, 1).astype(np.float32)
    large = max_exact + (np.log(nf / max_exact) / math.log(MAX_DISTANCE / max_exact)
                         * (N_BUCKETS - max_exact)).astype(np.int32)
    return np.where(n < max_exact, n, np.minimum(large, N_BUCKETS - 1)).astype(np.int32)


def _bucket_or_masked(dist, valid):
    return np.where(valid, _bucket_np(dist), -1).astype(np.int32)


FAR_DIST = TQ + 1


def _check_far_bucket(max_dist):
    assert (_bucket_np(np.arange(FAR_DIST, max_dist + 1)) == N_BUCKETS - 1).all()


def _bias_tile_kernel(bucket_ref, rb_ref, out_ref):
    bucket = bucket_ref[...]
    acc = jnp.zeros(bucket.shape, F32)
    for b in range(N_BUCKETS):
        acc = acc + jnp.where(bucket == b, rb_ref[b:b + 1, :], 0.0)
    acc = (acc - rb_ref[N_BUCKETS - 1:N_BUCKETS, :]) * LOG2E
    out_ref[...] = jnp.where(bucket < 0, NEG, acc)


def _bias_tiles(bucket_np, rb_cols):
    rows, cols = bucket_np.shape
    return pl.pallas_call(
        _bias_tile_kernel,
        out_shape=jax.ShapeDtypeStruct((rows, cols), F32),
    )(jnp.asarray(bucket_np), rb_cols)


DD0 = 0
DC0 = 2 * TQ
DC_ROWS = 24
DW0 = DC0 + 32
PROMPT_DELTA_ROWS = DW0 + TQ


def _prompt_bucket_table():
    i = np.arange(TQ)[None, :]
    kk = np.arange(2 * TQ)[:, None]
    d = i + TQ - kk
    dd = _bucket_or_masked(d, d >= 0)
    cc = np.arange(32)[:, None] - 16
    d = i - CMP_STRIDE * cc - (CMP_LEN - 1)
    dc = _bucket_or_masked(d, (d >= 0) & (cc < 8))
    j = np.arange(TQ)[:, None]
    d = i + WINDOW - j
    dw = _bucket_or_masked(d, (d >= 0) & (d < WINDOW))
    tab = np.concatenate([dd, dc, dw], axis=0)
    return np.tile(tab, (1, N_HEADS))


def _sample_bucket_table(past, tq, wb):
    col = np.arange(LANE)
    t = (col % tq)[None, :]
    colok = (col < N_HEADS * tq)[None, :]
    kk = np.arange(LANE)[:, None]
    d = LANE + t - kk
    dlast = _bucket_or_masked(d, colok & (d >= 0))
    d = t - kk
    dnew = _bucket_or_masked(d, colok & (d >= 0) & (kk < tq))
    d = wb + t - kk
    dedge = _bucket_or_masked(d, colok & (d >= 0) & (d < WINDOW))
    nsub = past // CMP_STRIDE
    c = np.arange(nsub)[:, None]
    d = past + t - CMP_STRIDE * c - (CMP_LEN - 1)
    dcmp = _bucket_or_masked(d, colok & (d >= 0) & (c < nsub - CMP_RATIO + 1))
    return np.concatenate([dlast, dnew, dedge, dcmp], axis=0)


IN_Q = 0
IN_KV = ATT_WIDTH
IN_U = IN_KV + N_BRANCH * KV_COLS
IN_G = IN_U + POOL_WIDTH
IN_COLS_PAD = IN_G + LANE


def _rms(x, g):
    return x * lax.rsqrt(jnp.mean(x * x, axis=-1, keepdims=True) + EPS) * g


def _sigmoid(x):
    return 1.0 / (1.0 + jnp.exp(-x))


def _inproj_common(x_ref, g_ref, w_ref, q_ref, u_ref, gate_ref):
    h = _rms(x_ref[...], g_ref[...]).astype(BF16)
    z = jnp.dot(h, w_ref[...], preferred_element_type=F32)
    q_ref[...] = z[:, IN_Q:IN_KV].astype(BF16)
    u_ref[...] = z[:, IN_U:IN_G]
    gate_ref[...] = _sigmoid(z[:, IN_G:IN_COLS_PAD])
    return [z[:, IN_KV + i * KV_COLS:IN_KV + (i + 1) * KV_COLS] for i in range(N_BRANCH)]


def _inproj_rows_kernel(x_ref, g_ref, w_ref, q_ref, u_ref, gate_ref, kvc_ref, kvs_ref, kvw_ref):
    kvc, kvs, kvw = _inproj_common(x_ref, g_ref, w_ref, q_ref, u_ref, gate_ref)
    kvc_ref[...] = kvc
    kvs_ref[...] = kvs
    kvw_ref[...] = kvw


def _inproj_seq_kernel(x_ref, g_ref, w_ref, cbuf_ref, sbuf_ref, q_ref, u_ref, gate_ref, kvc_ref, kb_ref,
                       vst_ref, vwt_ref, kvct_ref, kvst_ref, kvwt_ref):
    del cbuf_ref, sbuf_ref
    kvc, kvs, kvw = _inproj_common(x_ref, g_ref, w_ref, q_ref, u_ref, gate_ref)
    kvc_ref[...] = kvc
    kb_ref[...] = jnp.concatenate([kvs[:, 0:KV_LANES], kvw[:, 0:KV_LANES]], axis=1).astype(BF16)
    kvst = kvs.T
    kvwt = kvw.T
    kvct_ref[0, 0] = kvc.T
    kvst_ref[0, 0] = kvst
    kvwt_ref[0] = kvwt
    vst_ref[0] = kvst[KV_LANES:KV_COLS].astype(BF16)
    vwt_ref[0] = kvwt[KV_LANES:KV_COLS].astype(BF16)


def _inproj_rows(x, g, w):
    n = x.shape[0]
    tm = min(ROW_TILE, n)
    row = lambda c: pl.BlockSpec((tm, c), lambda i: (i, 0))
    full = lambda a: pl.BlockSpec(a.shape, lambda i: (0,) * a.ndim)
    outs = [(ATT_WIDTH, BF16), (POOL_WIDTH, F32), (LANE, F32), (KV_COLS, F32), (KV_COLS, F32), (KV_COLS, F32)]
    return pl.pallas_call(
        _inproj_rows_kernel,
        grid=(n // tm,),
        in_specs=[row(D_MODEL), full(g), full(w)],
        out_specs=[row(c) for c, _ in outs],
        out_shape=[jax.ShapeDtypeStruct((n, c), dt) for c, dt in outs],
        compiler_params=pltpu.CompilerParams(dimension_semantics=("parallel",), vmem_limit_bytes=VMEM_LIMIT),
    )(x, g, w)


def _inproj_seq(layer, x, g, w, kvct_all, kvst_all):
    n = x.shape[0]
    _, b, _, t = kvct_all.shape
    tm = min(ROW_TILE, t)
    tiles = t // tm
    row = lambda c: pl.BlockSpec((tm, c), lambda i: (i, 0))
    full = lambda a: pl.BlockSpec(a.shape, lambda i: (0,) * a.ndim)
    hbm = pl.BlockSpec(memory_space=pl.ANY)
    seq_t = lambda r: pl.BlockSpec((1, r, tm), lambda i: (i // tiles, 0, i % tiles))
    all_t = pl.BlockSpec((1, 1, KV_COLS, tm), lambda i: (layer, i // tiles, 0, i % tiles))
    rows = [(ATT_WIDTH, BF16), (POOL_WIDTH, F32), (LANE, F32), (KV_COLS, F32), (KV_COLS, BF16)]
    return pl.pallas_call(
        _inproj_seq_kernel,
        grid=(n // tm,),
        in_specs=[row(D_MODEL), full(g), full(w), hbm, hbm],
        out_specs=[row(c) for c, _ in rows] + [seq_t(KV_LANES), seq_t(KV_LANES), all_t, all_t, seq_t(KV_COLS)],
        out_shape=[jax.ShapeDtypeStruct((n, c), dt) for c, dt in rows]
        + [jax.ShapeDtypeStruct((b, KV_LANES, t), BF16), jax.ShapeDtypeStruct((b, KV_LANES, t), BF16),
           jax.ShapeDtypeStruct(kvct_all.shape, F32), jax.ShapeDtypeStruct(kvst_all.shape, F32),
           jax.ShapeDtypeStruct((b, KV_COLS, t), F32)],
        input_output_aliases={3: 7, 4: 8},
        compiler_params=pltpu.CompilerParams(dimension_semantics=("parallel",), vmem_limit_bytes=VMEM_LIMIT),
    )(x, g, w, kvct_all, kvst_all)


def _compress_tokens(x_of, w_ref, pe_ref, n_sub):
    outs = []
    for j in range(2):
        parts = [jnp.zeros((n_sub, KV_LANES), F32) for _ in range(CMP_RATIO)]
        for l in range(CMP_STRIDE):
            x = x_of(l, j)
            for r in range(CMP_RATIO):
                row = (j * CMP_RATIO + r) * CMP_STRIDE + l
                a = (x + pe_ref[row:row + 1, :]).astype(BF16)
                parts[r] = parts[r] + jnp.dot(a, w_ref[j, r, l], preferred_element_type=F32)
        comp = parts[0]
        for r in range(1, CMP_RATIO):
            comp = comp + pltpu.roll(parts[r], n_sub - r, axis=0)
        outs.append(comp)
    return outs


def _compress_kernel(x_ref, w_ref, pe_ref, ck_ref, cvt_ref):
    n_sub = x_ref.shape[1]
    x_of = lambda l, j: x_ref[0, :, l * KV_COLS + j * KV_LANES:l * KV_COLS + (j + 1) * KV_LANES]
    ck, cv = _compress_tokens(x_of, w_ref, pe_ref, n_sub)
    ck_ref[0] = ck.astype(BF16)
    cvt_ref[0] = cv.T.astype(BF16)


def _compress_prompt(kvc, w_bd, pe_rows):
    b, t, _ = kvc.shape
    n_sub = t // CMP_STRIDE
    x = kvc.reshape(b, n_sub, CMP_STRIDE * KV_COLS)
    full = lambda a: pl.BlockSpec(a.shape, lambda i: (0,) * a.ndim)
    return pl.pallas_call(
        _compress_kernel,
        grid=(b,),
        in_specs=[pl.BlockSpec((1, n_sub, CMP_STRIDE * KV_COLS), lambda i: (i, 0, 0)), full(w_bd), full(pe_rows)],
        out_specs=[pl.BlockSpec((1, n_sub, KV_LANES), lambda i: (i, 0, 0)),
                   pl.BlockSpec((1, KV_LANES, n_sub), lambda i: (i, 0, 0))],
        out_shape=[jax.ShapeDtypeStruct((b, n_sub, KV_LANES), BF16), jax.ShapeDtypeStruct((b, KV_LANES, n_sub), BF16)],
        compiler_params=pltpu.CompilerParams(dimension_semantics=("parallel",), vmem_limit_bytes=VMEM_LIMIT),
    )(x, w_bd, pe_rows)


def _split_dot(m, p):
    hi = p.astype(BF16)
    lo = (p - hi.astype(F32)).astype(BF16)
    return jnp.dot(m, hi, preferred_element_type=F32) + jnp.dot(m, lo, preferred_element_type=F32)


def _pool_mix(ext_ref, u, pos, wpool_ref, pscale_ref, rows):
    outs = []
    for k, w in enumerate(POOL_WINDOWS):
        ln = slice(k * POOL_GROUP_DIM, (k + 1) * POOL_GROUP_DIM)
        ws = ext_ref[POOL_HALO:POOL_HALO + rows, ln]
        for s in range(1, w):
            ws = ws + ext_ref[POOL_HALO - s:POOL_HALO - s + rows, ln]
        cnt = jnp.minimum(pos + 1, w).astype(F32)
        pooled = (ws / cnt - u[:, ln]).astype(BF16)
        o = jnp.dot(pooled, wpool_ref[k], preferred_element_type=F32) * pscale_ref[:, ln]
        outs.append(o.astype(BF16))
    return outs


def _prompt_attn_kernel(q_ref, gate_ref, u_ref, uh_ref, pst_ref, ksel_ref, kwin_ref, vst_ref, vwt_ref,
                        ck_ref, cvt_ref, delta_ref, mimp_ref, eblk_ref, wpool_ref, pscale_ref,
                        mix_ref,
                        sc_s, qa_s, qf_s, acc_s, m_s, l_s, ext_s):
    qb = pl.program_id(1)
    t_len = ksel_ref.shape[1]
    n_cmp = ck_ref.shape[1]
    n_sel = t_len // SEL_BLOCK
    cols = N_HEADS * TQ
    blk_per_q = TQ // SEL_BLOCK
    n_chain = cols // CHAIN_COLS
    chains = [slice(c * CHAIN_COLS, (c + 1) * CHAIN_COLS) for c in range(n_chain)]

    q = q_ref[0].astype(F32)
    zeros = jnp.zeros((TQ, HEAD_DIM), F32)
    blocks = []
    for h in range(N_HEADS):
        qh = q[:, h * HEAD_DIM:(h + 1) * HEAD_DIM]
        halves = [zeros] * N_KV
        halves[h // GROUP] = qh
        blocks.append(jnp.concatenate(halves, axis=1))
    qbd = jnp.concatenate(blocks, axis=0).astype(BF16)

    def scores(k_rows):
        return lax.dot_general(k_rows, qbd, NT_DIMS, preferred_element_type=F32)

    sc = scores(ck_ref[0])
    crow = lax.broadcasted_iota(jnp.int32, (n_cmp, cols), 0)
    sc = jnp.where(crow >= 8 * qb + 8, NEG, sc)
    sc_s[0:16, :] = jnp.zeros((16, cols), F32)
    sc_s[16 + n_cmp:16 + n_cmp + 8, :] = jnp.zeros((8, cols), F32)
    sc_s[16:16 + n_cmp, :] = sc
    near = pl.ds(pl.multiple_of(8 * qb, 8), DC_ROWS)
    sc_s[near, :] = sc_s[near, :] + delta_ref[DC0:DC0 + DC_ROWS, :]
    sc = sc_s[16:16 + n_cmp, :]
    m_c = jnp.max(sc, axis=0, keepdims=True)
    p = jnp.exp2(sc - m_c)
    l_c = jnp.sum(p, axis=0, keepdims=True)
    pn = p * jnp.where(m_c > VALID_MIN, 1.0 / l_c, 0.0)
    o_cmp = jnp.dot(cvt_ref[0], pn.astype(BF16), preferred_element_type=F32)

    jidx = lax.broadcasted_iota(jnp.int32, (n_sel, TQ), 0)
    qi = lax.broadcasted_iota(jnp.int32, (n_sel, TQ), 1)
    cur = blk_per_q * qb + qi // SEL_BLOCK
    forced = (jidx == 0) | (jidx == cur) | (jidx == cur - 1)
    prev_blk = blk_per_q * jnp.maximum(qb - 1, 0)
    pad = jnp.zeros((LANE - n_sel, TQ), F32)
    imp_g = []
    for g in range(N_KV):
        ps = pn[:, (g * GROUP) * TQ:(g * GROUP + 1) * TQ]
        for r in range(1, GROUP):
            ps = ps + pn[:, (g * GROUP + r) * TQ:(g * GROUP + r + 1) * TQ]
        imp_g.append(_split_dot(mimp_ref[...], ps))

    n_wc = WINDOW // TQ + 1
    w_pos0 = [(qb - (n_wc - 1) + c) * TQ for c in range(n_wc)]
    w_k0 = [pl.multiple_of(jnp.maximum(p0, 0), TQ) for p0 in w_pos0]
    w_keys = [kwin_ref[0, pl.ds(k0, TQ), :] for k0 in w_k0]
    w_vt = [vwt_ref[0, :, pl.ds(k0, TQ)] for k0 in w_k0]
    w_delta = {0: DW0, n_wc - 2: DD0, n_wc - 1: DD0 + TQ}
    s_win = []
    for cs in chains:
        s_w = []
        for c in range(n_wc):
            s = lax.dot_general(w_keys[c], qbd[cs], NT_DIMS, preferred_element_type=F32)
            if c in w_delta:
                s = s + delta_ref[w_delta[c]:w_delta[c] + TQ, cs]
            if c < n_wc - 1:
                s = s + jnp.where(w_pos0[c] < 0, NEG, 0.0)
            s_w.append(s)
        s_win.append(s_w)

    ns_t = []
    for imp in imp_g:
        score = jnp.where(forced, jnp.inf, imp)
        score = jnp.where(jidx > cur, -jnp.inf, score)
        cnt = jnp.zeros((n_sel, TQ), jnp.int32)
        for jp in range(n_sel):
            row = score[jp:jp + 1, :]
            cnt = cnt + jnp.where(jidx > jp, jnp.where(row >= score, 1, 0), jnp.where(row > score, 1, 0))
        sel = (cnt < min(N_SELECT, n_sel)) & (jidx <= cur)
        ns_all = jnp.where(sel, 0.0, NEG)
        ns_far = jnp.where(jidx < prev_blk, ns_all, NEG)
        ns_t.append([jnp.concatenate([x, pad], axis=0).T.astype(BF16) for x in (ns_all, ns_far)])
    for h in range(N_HEADS):
        rows_h = slice(h * TQ, (h + 1) * TQ)
        qa_s[rows_h, 0:LANE] = qbd[rows_h]
        qf_s[rows_h, 0:LANE] = qbd[rows_h]
        qa_s[rows_h, LANE:2 * LANE] = ns_t[h // GROUP][0]
        qf_s[rows_h, LANE:2 * LANE] = ns_t[h // GROUP][1]

    m_s[...] = jnp.full((1, cols), NEG, F32)
    l_s[...] = jnp.zeros((1, cols), F32)
    acc_s[...] = jnp.zeros((KV_LANES, cols), F32)

    def online_update(cs, s, vt):
        m_old = m_s[:, cs]
        m_new = jnp.maximum(m_old, jnp.max(s, axis=0, keepdims=True))
        alpha = jnp.exp2(m_old - m_new)
        p = jnp.exp2(s - m_new)
        l_s[:, cs] = alpha * l_s[:, cs] + jnp.sum(p, axis=0, keepdims=True)
        acc_s[:, cs] = alpha * acc_s[:, cs] + jnp.dot(vt, p.astype(BF16), preferred_element_type=F32)
        m_s[:, cs] = m_new

    def keys_with_block(k0, n):
        return jnp.concatenate([ksel_ref[0, pl.ds(k0, n), :], eblk_ref[pl.ds(k0, n), :]], axis=1)

    def masked_scores(kcat, q_s, cs):
        return lax.dot_general(kcat, q_s[cs, :], NT_DIMS, preferred_element_type=F32)

    prev_key = TQ * jnp.maximum(qb - 1, 0)
    n_far = (prev_key + FAR_TILE - 1) // FAR_TILE

    def far_scores(t):
        kcat = keys_with_block(pl.multiple_of(t * FAR_TILE, FAR_TILE), FAR_TILE)
        return tuple(masked_scores(kcat, qf_s, cs) for cs in chains)

    def far_update(t, s_all):
        vt = vst_ref[0, :, pl.ds(pl.multiple_of(t * FAR_TILE, FAR_TILE), FAR_TILE)]
        for cs, s in zip(chains, s_all):
            online_update(cs, s, vt)

    def far_body(t, carry):
        far_update(t, far_scores(t))
        return carry

    lax.fori_loop(0, n_far, far_body, 0)

    k_prev = pl.multiple_of(prev_key, TQ)
    k_diag = pl.multiple_of(qb * TQ, TQ)
    no_prev = jnp.where(qb == 0, NEG, 0.0)
    kcat_prev = keys_with_block(k_prev, TQ)
    kcat_diag = keys_with_block(k_diag, TQ)
    vt_near = jnp.concatenate([vst_ref[0, :, pl.ds(k_prev, TQ)], vst_ref[0, :, pl.ds(k_diag, TQ)]], axis=1)
    s_near = []
    for cs in chains:
        s_prev = masked_scores(kcat_prev, qa_s, cs) + delta_ref[DD0:DD0 + TQ, cs] + no_prev
        s_diag = masked_scores(kcat_diag, qa_s, cs) + delta_ref[DD0 + TQ:DD0 + 2 * TQ, cs]
        s_near.append(jnp.concatenate([s_prev, s_diag], axis=0))

    for cs, s in zip(chains, s_near):
        online_update(cs, s, vt_near)
    l_sel = l_s[...]
    o_sel = acc_s[...]

    l_w, o_w = [], []
    for s_w in s_win:
        m_w = s_w[0].max(axis=0, keepdims=True)
        for s in s_w[1:]:
            m_w = jnp.maximum(m_w, s.max(axis=0, keepdims=True))
        l_c = jnp.zeros((1, CHAIN_COLS), F32)
        o_c = jnp.zeros((KV_LANES, CHAIN_COLS), F32)
        for s, vt in zip(s_w, w_vt):
            p = jnp.exp2(s - m_w)
            l_c = l_c + jnp.sum(p, axis=0, keepdims=True)
            o_c = o_c + jnp.dot(vt, p.astype(BF16), preferred_element_type=F32)
        l_w.append(l_c)
        o_w.append(o_c)
    l_win = jnp.concatenate(l_w, axis=1)
    o_win = jnp.concatenate(o_w, axis=1)

    gt = gate_ref[0].T
    gate_row = lambda x: jnp.concatenate([gt[N_BRANCH * h + x:N_BRANCH * h + x + 1, :] for h in range(N_HEADS)], axis=1)
    o_t = o_cmp * gate_row(0) + o_sel * (gate_row(1) / l_sel) + o_win * (gate_row(2) / l_win)
    pieces = []
    for h in range(N_HEADS):
        g = h // GROUP
        pieces.append(o_t[g * HEAD_DIM:(g + 1) * HEAD_DIM, h * TQ:(h + 1) * TQ])
    mix_ref[0, :, 0:ATT_WIDTH] = jnp.concatenate(pieces, axis=0).T.astype(BF16)

    u = u_ref[0]
    ext_s[0:POOL_HALO, :] = jnp.where(qb == 0, pst_ref[0], uh_ref[0])
    ext_s[POOL_HALO:POOL_HALO + TQ, :] = u
    pos = qb * TQ + lax.broadcasted_iota(jnp.int32, (TQ, 1), 0)
    for k, o in enumerate(_pool_mix(ext_s, u, pos, wpool_ref, pscale_ref, TQ)):
        mix_ref[0, :, ATT_WIDTH + k * POOL_GROUP_DIM:ATT_WIDTH + (k + 1) * POOL_GROUP_DIM] = o


def _prompt_attention(q, gates, u, pool_prev, kb, vst, vwt, ck, cvt, delta, mimp, eblk, wpool, pscale):
    b, t, _ = q.shape
    assert t % FAR_TILE == 0 and t >= WINDOW + TQ and t // SEL_BLOCK <= LANE
    n_cmp = ck.shape[1]
    cols = N_HEADS * TQ
    halo_per_q = TQ // POOL_HALO
    full = lambda a: pl.BlockSpec(a.shape, lambda i, j: (0,) * a.ndim)
    qblk = lambda c: pl.BlockSpec((1, TQ, c), lambda i, j: (i, j, 0))
    seq = lambda c: pl.BlockSpec((1, t, KV_LANES), lambda i, j: (i, 0, c))
    per_b = lambda a: pl.BlockSpec((1,) + a.shape[1:], lambda i, j: (i,) + (0,) * (a.ndim - 1))
    return pl.pallas_call(
        _prompt_attn_kernel,
        grid=(b, t // TQ),
        in_specs=[qblk(ATT_WIDTH), qblk(LANE), qblk(POOL_WIDTH),
                  pl.BlockSpec((1, POOL_HALO, POOL_WIDTH), lambda i, j: (i, jnp.maximum(j * halo_per_q - 1, 0), 0)),
                  per_b(pool_prev), seq(0), seq(1), per_b(vst), per_b(vwt), per_b(ck), per_b(cvt),
                  full(delta), full(mimp), full(eblk), full(wpool), full(pscale)],
        out_specs=pl.BlockSpec((1, TQ, D_MODEL), lambda i, j: (i, j, 0)),
        out_shape=jax.ShapeDtypeStruct((b, t, D_MODEL), BF16),
        scratch_shapes=[pltpu.VMEM((16 + n_cmp + 8, cols), F32),
                        pltpu.VMEM((cols, 2 * LANE), BF16), pltpu.VMEM((cols, 2 * LANE), BF16),
                        pltpu.VMEM((KV_LANES, cols), F32), pltpu.VMEM((1, cols), F32), pltpu.VMEM((1, cols), F32),
                        pltpu.VMEM((POOL_HALO + TQ, POOL_WIDTH), F32)],
        compiler_params=pltpu.CompilerParams(dimension_semantics=("parallel", "arbitrary"),
                                             vmem_limit_bytes=VMEM_LIMIT),
    )(q, gates, u, u, pool_prev, kb, kb, vst, vwt, ck, cvt, delta, mimp, eblk, wpool, pscale)


SD_LAST = 0
SD_NEW = LANE
SD_EDGE = 2 * LANE
SD_CMP = 3 * LANE


def _sample_attn_kernel(pt_ref, q_ref, gate_ref, u_ref, kvs_ref, kvw_ref, swin_ref, spool_ref,
                        ccache_ref, scache_ref, nwin_in_ref, wcmp_ref, pe_ref, delta_ref, mimp_ref, expand_ref,
                        wpool_ref, pscale_ref,
                        mix_ref, nwin_ref, npool_ref,
                        cbuf, sbuf, xk_s, xv_s, sem, ext_s, *, layer):
    del nwin_in_ref
    b = pl.program_id(0)
    nb = pl.num_programs(0)
    n_pages = pt_ref.shape[1]
    past = n_pages * PAGE_SIZE
    n_sub = past // CMP_STRIDE
    tq = q_ref.shape[1]
    rows = N_HEADS * tq
    wb = swin_ref.shape[3]
    n_sel = past // SEL_BLOCK + 1
    slot = lax.rem(b, 2)

    def page_copies(seq, slt):
        cps = []
        for p in range(n_pages):
            pg = pt_ref[seq, p]
            dst = (slice(None), pl.ds(p * PAGE_SIZE, PAGE_SIZE))
            cps.append(pltpu.make_async_copy(ccache_ref.at[layer, pg], cbuf.at[slt].at[dst], sem.at[0, slt]))
            cps.append(pltpu.make_async_copy(scache_ref.at[layer, pg], sbuf.at[slt].at[dst], sem.at[1, slt]))
        return cps

    @pl.when(b == 0)
    def _():
        for cp in page_copies(0, 0):
            cp.start()

    @pl.when(b + 1 < nb)
    def _():
        for cp in page_copies(b + 1, 1 - slot):
            cp.start()

    for cp in page_copies(b, slot):
        cp.wait()

    q = q_ref[0].astype(F32)
    zeros = jnp.zeros((tq, HEAD_DIM), F32)
    blocks = []
    for h in range(N_HEADS):
        halves = [zeros] * N_KV
        halves[h // GROUP] = q[:, h * HEAD_DIM:(h + 1) * HEAD_DIM]
        blocks.append(jnp.concatenate(halves, axis=1))
    qbd = jnp.concatenate(blocks, axis=0).astype(BF16)

    def scores(k_rows):
        return lax.dot_general(qbd, k_rows, NT_DIMS, preferred_element_type=F32)

    def scores_t(k_t):
        return jnp.dot(qbd, k_t.astype(BF16), preferred_element_type=F32)

    def pad_rows(x):
        return jnp.concatenate([x, jnp.zeros((LANE - tq, KV_LANES), F32)], axis=0)

    pad_new = lambda x: pad_rows(x).astype(BF16)
    d_last = delta_ref[:, SD_LAST:SD_LAST + LANE]
    d_new = delta_ref[:, SD_NEW:SD_NEW + LANE]

    def softmax_pv(s_old, vt_old, s_new, v_new):
        m = jnp.maximum(s_old.max(axis=1, keepdims=True), s_new.max(axis=1, keepdims=True))
        p_old = jnp.exp2(s_old - m)
        p_new = jnp.exp2(s_new - m)
        l = jnp.sum(p_old, axis=1, keepdims=True) + jnp.sum(p_new, axis=1, keepdims=True)
        o = (lax.dot_general(p_old.astype(BF16), vt_old.astype(BF16), NT_DIMS, preferred_element_type=F32)
             + jnp.dot(p_new.astype(BF16), v_new, preferred_element_type=F32))
        return o, l

    kvw_new = kvw_ref[0]
    s_first = scores_t(swin_ref[0, 0, 0:KV_LANES, 0:LANE]) + delta_ref[:, SD_EDGE:SD_EDGE + LANE]
    s_mid = scores_t(swin_ref[0, 0, 0:KV_LANES, LANE:wb - LANE])
    s_lastw = scores_t(swin_ref[0, 0, 0:KV_LANES, wb - LANE:wb]) + d_last
    s_wnew = scores(pad_new(kvw_new[:, 0:KV_LANES])) + d_new
    o_win, l_win = softmax_pv(jnp.concatenate([s_first, s_mid, s_lastw], axis=1), swin_ref[0, 0, KV_LANES:KV_COLS, :],
                              s_wnew, pad_new(kvw_new[:, KV_LANES:KV_COLS]))

    shifted = pltpu.roll(swin_ref[0, 0], wb - tq, axis=1)
    new_t = jnp.concatenate([pad_rows(kvw_new[:, 0:KV_LANES]).T, pad_rows(kvw_new[:, KV_LANES:KV_COLS]).T], axis=0)
    new_t = pltpu.roll(new_t, LANE - tq, axis=1)
    lane = lax.broadcasted_iota(jnp.int32, (KV_COLS, LANE), 1)
    nwin_ref[0, 0, :, 0:wb - LANE] = shifted[:, 0:wb - LANE]
    nwin_ref[0, 0, :, wb - LANE:wb] = jnp.where(lane >= LANE - tq, new_t, shifted[:, wb - LANE:wb])

    kvs_new = kvs_ref[0]
    s_far = scores_t(sbuf[slot, 0:KV_LANES, 0:past - LANE])
    s_last = scores_t(sbuf[slot, 0:KV_LANES, past - LANE:past]) + d_last
    s_new = scores(pad_new(kvs_new[:, 0:KV_LANES])) + d_new

    pitch = xk_s.shape[0] // CMP_STRIDE
    sub_per_page = PAGE_SIZE // CMP_STRIDE
    for p in range(n_pages):
        tok = slice(p * PAGE_SIZE, (p + 1) * PAGE_SIZE)
        for x_s, feat in ((xk_s, slice(0, KV_LANES)), (xv_s, slice(KV_LANES, KV_COLS))):
            xt = cbuf[slot, feat, tok].T
            for v in range(PAGE_SIZE // 8):
                l0 = (8 * v) % CMP_STRIDE
                sub = p * sub_per_page + (8 * v) // CMP_STRIDE
                x_s[pl.ds(l0 * pitch + sub, 8, stride=pitch), :] = xt[8 * v:8 * v + 8, :]
    x_of = lambda l, j: (xk_s, xv_s)[j][l * pitch:l * pitch + n_sub, :]
    ck, cv = _compress_tokens(x_of, wcmp_ref, pe_ref, n_sub)
    sc = scores(ck.astype(BF16)) + delta_ref[:, SD_CMP:SD_CMP + n_sub]
    m_c = jnp.max(sc, axis=1, keepdims=True)
    p = jnp.exp2(sc - m_c)
    l_c = jnp.sum(p, axis=1, keepdims=True)
    pn = p * jnp.where(m_c > VALID_MIN, 1.0 / l_c, 0.0)
    o_cmp = jnp.dot(pn.astype(BF16), cv.astype(BF16), preferred_element_type=F32)

    jidx = lax.broadcasted_iota(jnp.int32, (tq, LANE), 1)
    cur = (past + lax.broadcasted_iota(jnp.int32, (tq, LANE), 0)) // SEL_BLOCK
    forced = (jidx == 0) | (jidx == cur) | (jidx == cur - 1)
    sel_g = []
    for g in range(N_KV):
        ps = pn[g * GROUP * tq:(g * GROUP + 1) * tq, :]
        for r in range(1, GROUP):
            ps = ps + pn[(g * GROUP + r) * tq:(g * GROUP + r + 1) * tq, :]
        hi = ps.astype(BF16)
        lo = (ps - hi.astype(F32)).astype(BF16)
        imp = (jnp.dot(hi, mimp_ref[...], preferred_element_type=F32)
               + jnp.dot(lo, mimp_ref[...], preferred_element_type=F32))
        score = jnp.where(forced, jnp.inf, imp)
        score = jnp.where((jidx > cur) | (jidx >= n_sel), -jnp.inf, score)
        cnt = jnp.zeros((tq, LANE), jnp.int32)
        for jp in range(n_sel):
            col = score[:, jp:jp + 1]
            cnt = cnt + jnp.where(jidx > jp, jnp.where(col >= score, 1, 0), jnp.where(col > score, 1, 0))
        sel = (cnt < min(N_SELECT, n_sel)) & (jidx <= cur) & (jidx < n_sel)
        sel_g.append(jnp.where(sel, 1.0, 0.0))
    sel_rows = jnp.concatenate([sel_g[h // GROUP] for h in range(N_HEADS)], axis=0).astype(BF16)
    mask = jnp.dot(sel_rows, expand_ref[...], preferred_element_type=F32)

    s_past = jnp.where(mask > 0.5, jnp.concatenate([s_far, s_last], axis=1), NEG)
    o_sel, l_sel = softmax_pv(s_past, sbuf[slot, KV_LANES:KV_COLS, :], s_new, pad_new(kvs_new[:, KV_LANES:KV_COLS]))

    gates = gate_ref[0]
    gate_col = lambda x: jnp.concatenate([gates[:, N_BRANCH * h + x:N_BRANCH * h + x + 1] for h in range(N_HEADS)], axis=0)
    o = o_cmp * gate_col(0) + o_sel * (gate_col(1) / l_sel) + o_win * (gate_col(2) / l_win)
    pieces = []
    for h in range(N_HEADS):
        g = h // GROUP
        pieces.append(o[h * tq:(h + 1) * tq, g * HEAD_DIM:(g + 1) * HEAD_DIM])
    mix_ref[0, :, 0:ATT_WIDTH] = jnp.concatenate(pieces, axis=1).astype(BF16)

    u = u_ref[0]
    ext_s[0:1, :] = jnp.zeros((1, POOL_WIDTH), F32)
    ext_s[1:POOL_HALO, :] = spool_ref[0, 0]
    ext_s[POOL_HALO:POOL_HALO + tq, :] = u
    pos = past + lax.broadcasted_iota(jnp.int32, (tq, 1), 0)
    for k, o_k in enumerate(_pool_mix(ext_s, u, pos, wpool_ref, pscale_ref, tq)):
        mix_ref[0, :, ATT_WIDTH + k * POOL_GROUP_DIM:ATT_WIDTH + (k + 1) * POOL_GROUP_DIM] = o_k
    npool_ref[0] = ext_s[POOL_HALO + tq - POOL_STATE:POOL_HALO + tq, :]


def _sample_attention(layer, page_table, q, gates, u, kvs, kvw, swin_t, state_pool, ccache_t, scache_t, nwin_all,
                      wcmp, pe_rows, delta, mimp, expand, wpool, pscale):
    nb, tq, _ = q.shape
    n_pages = page_table.shape[1]
    past = n_pages * PAGE_SIZE
    n_sub = past // CMP_STRIDE
    wb = swin_t.shape[3]
    assert wb == WINDOW and wb >= 3 * LANE and tq <= CMP_STRIDE and POOL_HALO + tq - POOL_STATE >= 0
    assert past // SEL_BLOCK + 1 <= LANE and past % LANE == 0 and n_sub <= LANE
    full = lambda a: pl.BlockSpec(a.shape, lambda i, pt: (0,) * a.ndim)
    per_b = lambda c: pl.BlockSpec((1, tq, c), lambda i, pt: (i, 0, 0))
    layer_b = lambda a: pl.BlockSpec((1, 1) + a.shape[2:], lambda i, pt: (layer, i) + (0,) * (a.ndim - 2))
    hbm = pl.BlockSpec(memory_space=pl.ANY)
    grid_spec = pltpu.PrefetchScalarGridSpec(
        num_scalar_prefetch=1,
        grid=(nb,),
        in_specs=[per_b(ATT_WIDTH), per_b(LANE), per_b(POOL_WIDTH), per_b(KV_COLS), per_b(KV_COLS),
                  layer_b(swin_t), layer_b(state_pool), hbm, hbm, hbm,
                  full(wcmp), full(pe_rows), full(delta), full(mimp), full(expand), full(wpool), full(pscale)],
        out_specs=[pl.BlockSpec((1, tq, D_MODEL), lambda i, pt: (i, 0, 0)),
                   layer_b(nwin_all),
                   pl.BlockSpec((1, POOL_STATE, POOL_WIDTH), lambda i, pt: (i, 0, 0))],
        scratch_shapes=[pltpu.VMEM((2, KV_COLS, past), F32), pltpu.VMEM((2, KV_COLS, past), F32),
                        pltpu.VMEM((CMP_STRIDE * (n_sub + 8), KV_LANES), F32),
                        pltpu.VMEM((CMP_STRIDE * (n_sub + 8), KV_LANES), F32),
                        pltpu.SemaphoreType.DMA((2, 2)), pltpu.VMEM((POOL_HALO + tq, POOL_WIDTH), F32)],
    )
    return pl.pallas_call(
        functools.partial(_sample_attn_kernel, layer=layer),
        grid_spec=grid_spec,
        out_shape=[jax.ShapeDtypeStruct((nb, tq, D_MODEL), BF16),
                   jax.ShapeDtypeStruct(nwin_all.shape, F32),
                   jax.ShapeDtypeStruct((nb, POOL_STATE, POOL_WIDTH), F32)],
        input_output_aliases={10: 1},
        compiler_params=pltpu.CompilerParams(dimension_semantics=("arbitrary",), vmem_limit_bytes=VMEM_LIMIT),
    )(page_table, q, gates, u, kvs, kvw, swin_t, state_pool, ccache_t, scache_t, nwin_all,
      wcmp, pe_rows, delta, mimp, expand, wpool, pscale)


def _ffn_kernel(x_ref, mix_ref, wout_ref, gn_ref, wg_ref, wu_ref, wd_ref, gf_ref, o_ref, xmid_s, h_s, acc_s, *,
                final_norm):
    c = pl.program_id(1)

    @pl.when(c == 0)
    def _():
        xm = x_ref[...] + jnp.dot(mix_ref[...], wout_ref[...], preferred_element_type=F32)
        xmid_s[...] = xm
        h_s[...] = _rms(xm, gn_ref[...]).astype(BF16)
        acc_s[...] = jnp.zeros(acc_s.shape, F32)

    h = h_s[...]
    gate = jnp.dot(h, wg_ref[...], preferred_element_type=F32)
    up = jnp.dot(h, wu_ref[...], preferred_element_type=F32)
    act = (gate * _sigmoid(gate) * up).astype(BF16)
    acc_s[...] += jnp.dot(act, wd_ref[...], preferred_element_type=F32)

    @pl.when(c == pl.num_programs(1) - 1)
    def _():
        y = xmid_s[...] + acc_s[...]
        o_ref[...] = _rms(y, gf_ref[...]) if final_norm else y


def _ffn(x, mix, wout, gn, wffn_in, wffn_out, gf, final_norm):
    n = x.shape[0]
    tm = min(ROW_TILE, n)
    d_ff = wffn_out.shape[0]
    n_chunk = 2
    tf = d_ff // n_chunk
    assert tf % LANE == 0 and n % tm == 0
    full = lambda a: pl.BlockSpec(a.shape, lambda i, c: (0,) * a.ndim)
    row = lambda cdim: pl.BlockSpec((tm, cdim), lambda i, c: (i, 0))
    return pl.pallas_call(
        functools.partial(_ffn_kernel, final_norm=final_norm),
        grid=(n // tm, n_chunk),
        in_specs=[row(D_MODEL), row(D_MODEL), full(wout), full(gn),
                  pl.BlockSpec((D_MODEL, tf), lambda i, c: (0, c)),
                  pl.BlockSpec((D_MODEL, tf), lambda i, c: (0, n_chunk + c)),
                  pl.BlockSpec((tf, D_MODEL), lambda i, c: (c, 0)), full(gf)],
        out_specs=row(D_MODEL),
        out_shape=jax.ShapeDtypeStruct((n, D_MODEL), F32),
        scratch_shapes=[pltpu.VMEM((tm, D_MODEL), F32), pltpu.VMEM((tm, D_MODEL), BF16), pltpu.VMEM((tm, D_MODEL), F32)],
        compiler_params=pltpu.CompilerParams(dimension_semantics=("parallel", "arbitrary"),
                                             vmem_limit_bytes=VMEM_LIMIT),
    )(x, mix, wout, gn, wffn_in, wffn_in, wffn_out, gf)


def _importance_matrix(n_sel, n_cmp_rows, n_cmp):
    spb = SEL_BLOCK // CMP_STRIDE
    j = np.arange(n_sel)[:, None]
    c = np.arange(n_cmp_rows)[None, :]
    return ((c >= spb * j - (CMP_RATIO - 1)) & (c <= spb * j + spb - 1) & (c < n_cmp)).astype(np.float32)


def kernel(x_prompt, x_sample, cache_cmp, cache_sel, state_win, state_pool, page_table, rel_bias, norm_mix, norm_ffn,
           norm_final, w_in, w_out, cmp_pos, w_cmp, w_pool, pool_scale, w_ffn_in, w_ffn_out):
    bp, t, _ = x_prompt.shape
    bs, tq, _ = x_sample.shape
    depth = w_in.shape[0]
    n_pages = page_table.shape[1]
    past = n_pages * PAGE_SIZE
    n_phys = cache_cmp.shape[1]
    wb = state_win.shape[2]
    _check_far_bucket(max(t, past + tq) + WINDOW)

    rb_prompt = jnp.repeat(rel_bias.astype(F32), TQ, axis=1)
    delta_p = _bias_tiles(_prompt_bucket_table(), rb_prompt)
    rb_sample = jnp.pad(jnp.repeat(rel_bias.astype(F32), tq, axis=1), ((0, 0), (0, LANE - N_HEADS * tq)))
    delta_s = _bias_tiles(_sample_bucket_table(past, tq, wb), rb_sample).T[:N_HEADS * tq]

    n_sub_p = t // CMP_STRIDE
    mimp_p = jnp.asarray(_importance_matrix(t // SEL_BLOCK, n_sub_p, n_sub_p - CMP_RATIO + 1), BF16)
    eblk_p = jnp.asarray((np.arange(t)[:, None] // SEL_BLOCK == np.arange(LANE)[None, :]).astype(np.float32), BF16)
    n_sub_s = past // CMP_STRIDE
    mimp_s =jnp.asarray(_importance_matrix(LANE, n_sub_s, n_sub_s - CMP_RATIO + 1).T, BF16)
    expand = jnp.asarray((np.arange(LANE)[:, None] == np.arange(past)[None, :] // SEL_BLOCK).astype(np.float32), BF16)

    pool_zero = jnp.zeros((bp, POOL_HALO, POOL_WIDTH), F32)
    feature_major = lambda a: jnp.moveaxis(a, 2, -1).reshape(a.shape[0], a.shape[1], KV_COLS, a.shape[2])
    token_major = lambda a: jnp.moveaxis(a.reshape(a.shape[:2] + (2, N_KV, HEAD_DIM, a.shape[3])), -1, 2)
    ccache_t = feature_major(cache_cmp)
    scache_t = feature_major(cache_sel)
    swin_t = feature_major(state_win)
    kvct_all = jnp.zeros((depth, bp, KV_COLS, t), F32)
    kvst_all = jnp.zeros((depth, bp, KV_COLS, t), F32)
    nwin_all = jnp.zeros((depth, bs, KV_COLS, wb), F32)

    xp = x_prompt.reshape(bp * t, D_MODEL)
    xs = x_sample.reshape(bs * tq, D_MODEL)
    outs = {k: [] for k in ("p_win", "p_pool", "s_cmp", "s_sel", "s_pool")}
    kv_shape = lambda b, n: (b, n, 2, N_KV, HEAD_DIM)
    off_g = ATT_WIDTH + N_BRANCH * KV_COLS
    for l in range(depth):
        wl = w_in[l]
        w_all = jnp.concatenate([wl[:, :ATT_WIDTH] * (HEAD_DIM ** -0.5 * LOG2E), wl[:, ATT_WIDTH:off_g],
                                 wl[:, off_g + GATE_COLS:], wl[:, off_g:off_g + GATE_COLS],
                                 jnp.zeros((D_MODEL, LANE - GATE_COLS), F32)], axis=1).astype(BF16)
        g_mix = norm_mix[l].reshape(1, D_MODEL)
        g_ffn = norm_ffn[l].reshape(1, D_MODEL)
        g_fin = norm_final.reshape(1, D_MODEL)
        wc = w_cmp[l].reshape(2, CMP_RATIO, CMP_STRIDE, HEAD_DIM, HEAD_DIM)
        zero = jnp.zeros_like(wc)
        w_bd = jnp.concatenate([jnp.concatenate([wc, zero], axis=-1), jnp.concatenate([zero, wc], axis=-1)],
                               axis=-2).astype(BF16)
        pe = cmp_pos[l].reshape(CMP_RATIO, CMP_STRIDE, 2, HEAD_DIM).transpose(2, 0, 1, 3)
        pe_rows = jnp.tile(pe.reshape(2 * CMP_RATIO * CMP_STRIDE, HEAD_DIM), (1, N_KV)).astype(F32)
        wpool = w_pool[l].astype(BF16)
        pscale = pool_scale[l].reshape(1, POOL_WIDTH)
        wout = w_out[l].astype(BF16)
        wffn_in = w_ffn_in[l].astype(BF16)
        wffn_out = w_ffn_out[l].astype(BF16)
        last = l == depth - 1

        q, u, gates, kvc, kb, vst, vwt, kvct_all, kvst_all, kvwt = _inproj_seq(l, xp, g_mix, w_all, kvct_all, kvst_all)
        ck, cvt = _compress_prompt(kvc.reshape(bp, t, KV_COLS), w_bd, pe_rows)
        mix = _prompt_attention(q.reshape(bp, t, ATT_WIDTH), gates.reshape(bp, t, LANE), u.reshape(bp, t, POOL_WIDTH),
                                pool_zero, kb.reshape(bp, t, KV_COLS), vst, vwt, ck, cvt, delta_p, mimp_p, eblk_p,
                                wpool, pscale)
        xp = _ffn(xp, mix.reshape(bp * t, D_MODEL), wout, g_ffn, wffn_in, wffn_out, g_fin, last)
        outs["p_win"].append(kvwt[:, :, t - wb:])
        outs["p_pool"].append(u.reshape(bp, t, POOL_WIDTH)[:, t - POOL_STATE:])

        q, u, gates, kvc, kvs, kvw = _inproj_rows(xs, g_mix, w_all)
        mix, nwin_all, npool = _sample_attention(
            l, page_table, q.reshape(bs, tq, ATT_WIDTH), gates.reshape(bs, tq, LANE), u.reshape(bs, tq, POOL_WIDTH),
            kvs.reshape(bs, tq, KV_COLS), kvw.reshape(bs, tq, KV_COLS), swin_t, state_pool, ccache_t, scache_t,
            nwin_all, w_bd, pe_rows, delta_s, mimp_s, expand, wpool, pscale)
        xs = _ffn(xs, mix.reshape(bs * tq, D_MODEL), wout, g_ffn, wffn_in, wffn_out, g_fin, last)
        outs["s_cmp"].append(kvc.reshape(kv_shape(bs, tq)))
        outs["s_sel"].append(kvs.reshape(kv_shape(bs, tq)))
        outs["s_pool"].append(npool)

    return (xp.reshape(bp, t, D_MODEL), xs.reshape(bs, tq, D_MODEL),
            token_major(kvct_all), token_major(kvst_all), token_major(jnp.stack(outs["p_win"])),
            jnp.stack(outs["p_pool"]), jnp.stack(outs["s_cmp"]), jnp.stack(outs["s_sel"]), token_major(nwin_all),
            jnp.stack(outs["s_pool"]))
```

```python
import functools
import math

import numpy as np
import jax
import jax.numpy as jnp
from jax import lax
from jax.experimental import pallas as pl
from jax.experimental.pallas import tpu as pltpu

D_MODEL = 1024
HEAD_DIM = 64
N_HEADS = 8
N_KV = 2
GROUP = N_HEADS // N_KV
ATT_WIDTH = N_HEADS * HEAD_DIM
POOL_WIDTH = D_MODEL - ATT_WIDTH
KV_COLS = 2 * N_KV * HEAD_DIM
KV_LANES = N_KV * HEAD_DIM
VT_ROWS = KV_LANES + 16
CMP_LEN = 32
CMP_STRIDE = 16
CMP_RATIO = CMP_LEN // CMP_STRIDE
SEL_BLOCK = 64
N_SELECT = 16
WINDOW = 512
N_BRANCH = 3
GATE_COLS = N_HEADS * N_BRANCH
POOL_WINDOWS = (2, 4, 8, 16)
POOL_GROUP_DIM = POOL_WIDTH // len(POOL_WINDOWS)
POOL_STATE = max(POOL_WINDOWS) - 1
POOL_HALO = 16
N_BUCKETS = 32
MAX_DISTANCE = 128
PAGE_SIZE = 128
EPS = 1e-6
NEG = -1e30
VALID_MIN = -1e29

LANE = 128
TQ = 128
FAR_TILE = 512
CHAIN_COLS = 256
LOG2E = 1.4426950408889634
ROW_TILE = 512
FFN_SLICE = 512
VMEM_LIMIT = 52 * 1024 * 1024

F32 = jnp.float32
BF16 = jnp.bfloat16
NT_DIMS = (((1,), (1,)), ((), ()))


def _bucket_np(dist):
    n = np.maximum(dist, 0)
    max_exact = N_BUCKETS // 2
    nf = np.maximum(n, 1).astype(np.float32)
    large = max_exact + (np.log(nf / max_exact) / math.log(MAX_DISTANCE / max_exact)
                         * (N_BUCKETS - max_exact)).astype(np.int32)
    return np.where(n < max_exact, n, np.minimum(large, N_BUCKETS - 1)).astype(np.int32)


def _bucket_or_masked(dist, valid):
    return np.where(valid, _bucket_np(dist), -1).astype(np.int32)


FAR_DIST = TQ + 1


def _check_far_bucket(max_dist):
    assert (_bucket_np(np.arange(FAR_DIST, max_dist + 1)) == N_BUCKETS - 1).all()


def _bias_tile_kernel(bucket_ref, rb_ref, out_ref):
    bucket = bucket_ref[...]
    acc = jnp.zeros(bucket.shape, F32)
    for b in range(N_BUCKETS):
        acc = acc + jnp.where(bucket == b, rb_ref[b:b + 1, :], 0.0)
    acc = (acc - rb_ref[N_BUCKETS - 1:N_BUCKETS, :]) * LOG2E
    out_ref[...] = jnp.where(bucket < 0, NEG, acc)


def _bias_tiles(bucket_np, rb_cols):
    rows, cols = bucket_np.shape
    return pl.pallas_call(
        _bias_tile_kernel,
        out_shape=jax.ShapeDtypeStruct((rows, cols), F32),
    )(jnp.asarray(bucket_np), rb_cols)


DD0 = 0
DC0 = 2 * TQ
DC_ROWS = 24
DW0 = DC0 + 32
PROMPT_DELTA_ROWS = DW0 + TQ


def _prompt_bucket_table():
    i = np.arange(TQ)[None, :]
    kk = np.arange(2 * TQ)[:, None]
    d = i + TQ - kk
    dd = _bucket_or_masked(d, d >= 0)
    cc = np.arange(32)[:, None] - 16
    d = i - CMP_STRIDE * cc - (CMP_LEN - 1)
    dc = _bucket_or_masked(d, (d >= 0) & (cc < 8))
    j = np.arange(TQ)[:, None]
    d = i + WINDOW - j
    dw = _bucket_or_masked(d, (d >= 0) & (d < WINDOW))
    tab = np.concatenate([dd, dc, dw], axis=0)
    return np.tile(tab, (1, N_HEADS))


def _sample_bucket_table(past, tq, wb):
    col = np.arange(LANE)
    t = (col % tq)[None, :]
    colok = (col < N_HEADS * tq)[None, :]
    kk = np.arange(LANE)[:, None]
    d = LANE + t - kk
    dlast = _bucket_or_masked(d, colok & (d >= 0))
    d = t - kk
    dnew = _bucket_or_masked(d, colok & (d >= 0) & (kk < tq))
    d = wb + t - kk
    dedge = _bucket_or_masked(d, colok & (d >= 0) & (d < WINDOW))
    nsub = past // CMP_STRIDE
    c = np.arange(nsub)[:, None]
    d = past + t - CMP_STRIDE * c - (CMP_LEN - 1)
    dcmp = _bucket_or_masked(d, colok & (d >= 0) & (c < nsub - CMP_RATIO + 1))
    return np.concatenate([dlast, dnew, dedge, dcmp], axis=0)


IN_Q = 0
IN_KV = ATT_WIDTH
IN_U = IN_KV + N_BRANCH * KV_COLS
IN_G = IN_U + POOL_WIDTH
IN_COLS_PAD = IN_G + LANE


def _rms(x, g):
    return x * lax.rsqrt(jnp.mean(x * x, axis=-1, keepdims=True) + EPS) * g


def _sigmoid(x):
    return 1.0 / (1.0 + jnp.exp(-x))


def _inproj_common(x_ref, g_ref, w_ref, q_ref, u_ref, gate_ref):
    h = _rms(x_ref[...], g_ref[...]).astype(BF16)
    z = jnp.dot(h, w_ref[...], preferred_element_type=F32)
    q_ref[...] = z[:, IN_Q:IN_KV].astype(BF16)
    u_ref[...] = z[:, IN_U:IN_G]
    gate_ref[...] = _sigmoid(z[:, IN_G:IN_COLS_PAD])
    return [z[:, IN_KV + i * KV_COLS:IN_KV + (i + 1) * KV_COLS] for i in range(N_BRANCH)]


def _inproj_rows_kernel(x_ref, g_ref, w_ref, q_ref, u_ref, gate_ref, kvc_ref, kvs_ref, kvw_ref):
    kvc, kvs, kvw = _inproj_common(x_ref, g_ref, w_ref, q_ref, u_ref, gate_ref)
    kvc_ref[...] = kvc
    kvs_ref[...] = kvs
    kvw_ref[...] = kvw


def _inproj_seq_kernel(x_ref, g_ref, w_ref, cbuf_ref, sbuf_ref, q_ref, u_ref, gate_ref, kvc_ref, kb_ref,
                       vst_ref, vwt_ref, kvct_ref, kvst_ref, kvwt_ref):
    del cbuf_ref, sbuf_ref
    kvc, kvs, kvw = _inproj_common(x_ref, g_ref, w_ref, q_ref, u_ref, gate_ref)
    kvc_ref[...] = kvc
    kb_ref[...] = jnp.concatenate([kvs[:, 0:KV_LANES], kvw[:, 0:KV_LANES]], axis=1).astype(BF16)
    kvst = kvs.T
    kvwt = kvw.T
    kvct_ref[0, 0] = kvc.T
    kvst_ref[0, 0] = kvst
    kvwt_ref[0] = kvwt
    ones = jnp.ones((VT_ROWS - KV_LANES, kvst.shape[1]), F32)
    vst_ref[0] = jnp.concatenate([kvst[KV_LANES:KV_COLS], ones], axis=0).astype(BF16)
    vwt_ref[0] = jnp.concatenate([kvwt[KV_LANES:KV_COLS], ones], axis=0).astype(BF16)


def _inproj_rows(x, g, w):
    n = x.shape[0]
    tm = min(ROW_TILE, n)
    row = lambda c: pl.BlockSpec((tm, c), lambda i: (i, 0))
    full = lambda a: pl.BlockSpec(a.shape, lambda i: (0,) * a.ndim)
    outs = [(ATT_WIDTH, BF16), (POOL_WIDTH, F32), (LANE, F32), (KV_COLS, F32), (KV_COLS, F32), (KV_COLS, F32)]
    return pl.pallas_call(
        _inproj_rows_kernel,
        grid=(n // tm,),
        in_specs=[row(D_MODEL), full(g), full(w)],
        out_specs=[row(c) for c, _ in outs],
        out_shape=[jax.ShapeDtypeStruct((n, c), dt) for c, dt in outs],
        compiler_params=pltpu.CompilerParams(dimension_semantics=("parallel",), vmem_limit_bytes=VMEM_LIMIT),
    )(x, g, w)


def _inproj_seq(layer, x, g, w, kvct_all, kvst_all):
    n = x.shape[0]
    _, b, _, t = kvct_all.shape
    tm = min(ROW_TILE, t)
    tiles = t // tm
    row = lambda c: pl.BlockSpec((tm, c), lambda i: (i, 0))
    full = lambda a: pl.BlockSpec(a.shape, lambda i: (0,) * a.ndim)
    hbm = pl.BlockSpec(memory_space=pl.ANY)
    seq_t = lambda r: pl.BlockSpec((1, r, tm), lambda i: (i // tiles, 0, i % tiles))
    all_t = pl.BlockSpec((1, 1, KV_COLS, tm), lambda i: (layer, i // tiles, 0, i % tiles))
    rows = [(ATT_WIDTH, BF16), (POOL_WIDTH, F32), (LANE, F32), (KV_COLS, F32), (KV_COLS, BF16)]
    return pl.pallas_call(
        _inproj_seq_kernel,
        grid=(n // tm,),
        in_specs=[row(D_MODEL), full(g), full(w), hbm, hbm],
        out_specs=[row(c) for c, _ in rows] + [seq_t(VT_ROWS), seq_t(VT_ROWS), all_t, all_t, seq_t(KV_COLS)],
        out_shape=[jax.ShapeDtypeStruct((n, c), dt) for c, dt in rows]
        + [jax.ShapeDtypeStruct((b, VT_ROWS, t), BF16), jax.ShapeDtypeStruct((b, VT_ROWS, t), BF16),
           jax.ShapeDtypeStruct(kvct_all.shape, F32), jax.ShapeDtypeStruct(kvst_all.shape, F32),
           jax.ShapeDtypeStruct((b, KV_COLS, t), F32)],
        input_output_aliases={3: 7, 4: 8},
        compiler_params=pltpu.CompilerParams(dimension_semantics=("parallel",), vmem_limit_bytes=VMEM_LIMIT),
    )(x, g, w, kvct_all, kvst_all)


def _compress_tokens(x_of, w_ref, pe_ref, n_sub):
    outs = []
    for j in range(2):
        parts = [jnp.zeros((n_sub, KV_LANES), F32) for _ in range(CMP_RATIO)]
        for l in range(CMP_STRIDE):
            x = x_of(l, j)
            for r in range(CMP_RATIO):
                row = (j * CMP_RATIO + r) * CMP_STRIDE + l
                a = (x + pe_ref[row:row + 1, :]).astype(BF16)
                parts[r] = parts[r] + jnp.dot(a, w_ref[j, r, l], preferred_element_type=F32)
        comp = parts[0]
        for r in range(1, CMP_RATIO):
            comp = comp + pltpu.roll(parts[r], n_sub - r, axis=0)
        outs.append(comp)
    return outs


def _compress_kernel(x_ref, w_ref, pe_ref, ck_ref, cvt_ref):
    n_sub = x_ref.shape[1]
    x_of = lambda l, j: x_ref[0, :, l * KV_COLS + j * KV_LANES:l * KV_COLS + (j + 1) * KV_LANES]
    ck, cv = _compress_tokens(x_of, w_ref, pe_ref, n_sub)
    ck_ref[0] = ck.astype(BF16)
    cvt_ref[0] = cv.T.astype(BF16)


def _compress_prompt(kvc, w_bd, pe_rows):
    b, t, _ = kvc.shape
    n_sub = t // CMP_STRIDE
    x = kvc.reshape(b, n_sub, CMP_STRIDE * KV_COLS)
    full = lambda a: pl.BlockSpec(a.shape, lambda i: (0,) * a.ndim)
    return pl.pallas_call(
        _compress_kernel,
        grid=(b,),
        in_specs=[pl.BlockSpec((1, n_sub, CMP_STRIDE * KV_COLS), lambda i: (i, 0, 0)), full(w_bd), full(pe_rows)],
        out_specs=[pl.BlockSpec((1, n_sub, KV_LANES), lambda i: (i, 0, 0)),
                   pl.BlockSpec((1, KV_LANES, n_sub), lambda i: (i, 0, 0))],
        out_shape=[jax.ShapeDtypeStruct((b, n_sub, KV_LANES), BF16), jax.ShapeDtypeStruct((b, KV_LANES, n_sub), BF16)],
        compiler_params=pltpu.CompilerParams(dimension_semantics=("parallel",), vmem_limit_bytes=VMEM_LIMIT),
    )(x, w_bd, pe_rows)


def _split_dot(m, p):
    hi = p.astype(BF16)
    lo = (p - hi.astype(F32)).astype(BF16)
    return jnp.dot(m, hi, preferred_element_type=F32) + jnp.dot(m, lo, preferred_element_type=F32)


def _col_max(s):
    rows = s.shape[0]
    parts = [s[r:r + 64] for r in range(0, rows, 64)] if rows % 64 == 0 and rows > 64 else [s]
    while len(parts) > 1:
        parts = [jnp.maximum(a, b) for a, b in zip(parts[0::2], parts[1::2])] + ([parts[-1]] if len(parts) % 2 else [])
    return jnp.max(parts[0], axis=0, keepdims=True)


def _pool_mix(ext_ref, u, pos, wpool_ref, pscale_ref, rows):
    outs = []
    for k, w in enumerate(POOL_WINDOWS):
        ln = slice(k * POOL_GROUP_DIM, (k + 1) * POOL_GROUP_DIM)
        ws = ext_ref[POOL_HALO:POOL_HALO + rows, ln]
        for s in range(1, w):
            ws = ws + ext_ref[POOL_HALO - s:POOL_HALO - s + rows, ln]
        cnt = jnp.minimum(pos + 1, w).astype(F32)
        pooled = (ws / cnt - u[:, ln]).astype(BF16)
        o = jnp.dot(pooled, wpool_ref[k], preferred_element_type=F32) * pscale_ref[:, ln]
        outs.append(o.astype(BF16))
    return outs


def _prompt_attn_kernel(q_ref, gate_ref, u_ref, uh_ref, pst_ref, ksel_ref, kwin_ref, vst_ref, vwt_ref,
                        ck_ref, cvt_ref, delta_ref, mimp_ref, eblk_ref, wpool_ref, pscale_ref,
                        mix_ref,
                        sc_s, qa_s, qf_s, acc_s, m_s, ext_s):
    qb = pl.program_id(1)
    t_len = ksel_ref.shape[1]
    n_cmp = ck_ref.shape[1]
    n_sel = t_len // SEL_BLOCK
    cols = N_HEADS * TQ
    blk_per_q = TQ // SEL_BLOCK
    n_chain = cols // CHAIN_COLS
    chains = [slice(c * CHAIN_COLS, (c + 1) * CHAIN_COLS) for c in range(n_chain)]

    q = q_ref[0].astype(F32)
    zeros = jnp.zeros((TQ, HEAD_DIM), F32)
    blocks = []
    for h in range(N_HEADS):
        qh = q[:, h * HEAD_DIM:(h + 1) * HEAD_DIM]
        halves = [zeros] * N_KV
        halves[h // GROUP] = qh
        blocks.append(jnp.concatenate(halves, axis=1))
    qbd = jnp.concatenate(blocks, axis=0).astype(BF16)

    def scores(k_rows):
        return lax.dot_general(k_rows, qbd, NT_DIMS, preferred_element_type=F32)

    sc = scores(ck_ref[0])
    crow = lax.broadcasted_iota(jnp.int32, (n_cmp, cols), 0)
    sc = jnp.where(crow >= 8 * qb + 8, NEG, sc)
    sc_s[0:16, :] = jnp.zeros((16, cols), F32)
    sc_s[16 + n_cmp:16 + n_cmp + 8, :] = jnp.zeros((8, cols), F32)
    sc_s[16:16 + n_cmp, :] = sc
    near = pl.ds(pl.multiple_of(8 * qb, 8), DC_ROWS)
    sc_s[near, :] = sc_s[near, :] + delta_ref[DC0:DC0 + DC_ROWS, :]
    sc = sc_s[16:16 + n_cmp, :]
    m_c = jnp.max(sc, axis=0, keepdims=True)
    p = jnp.exp2(sc - m_c)
    l_c = jnp.sum(p, axis=0, keepdims=True)
    pn = p * jnp.where(m_c > VALID_MIN, 1.0 / l_c, 0.0)
    o_cmp = jnp.dot(cvt_ref[0], pn.astype(BF16), preferred_element_type=F32)

    jidx = lax.broadcasted_iota(jnp.int32, (n_sel, TQ), 0)
    qi = lax.broadcasted_iota(jnp.int32, (n_sel, TQ), 1)
    cur = blk_per_q * qb + qi // SEL_BLOCK
    forced = (jidx == 0) | (jidx == cur) | (jidx == cur - 1)
    prev_blk = blk_per_q * jnp.maximum(qb - 1, 0)
    pad = jnp.zeros((LANE - n_sel, TQ), F32)
    imp_g = []
    for g in range(N_KV):
        ps = pn[:, (g * GROUP) * TQ:(g * GROUP + 1) * TQ]
        for r in range(1, GROUP):
            ps = ps + pn[:, (g * GROUP + r) * TQ:(g * GROUP + r + 1) * TQ]
        imp_g.append(_split_dot(mimp_ref[...], ps))

    n_wc = WINDOW // TQ + 1
    w_pos0 = [(qb - (n_wc - 1) + c) * TQ for c in range(n_wc)]
    w_k0 = [pl.multiple_of(jnp.maximum(p0, 0), TQ) for p0 in w_pos0]
    w_keys = [kwin_ref[0, pl.ds(k0, TQ), :] for k0 in w_k0]
    w_vt = [vwt_ref[0, :, pl.ds(k0, TQ)] for k0 in w_k0]
    w_delta = {0: DW0, n_wc - 2: DD0, n_wc - 1: DD0 + TQ}
    s_win = []
    for cs in chains:
        s_w = []
        for c in range(n_wc):
            s = lax.dot_general(w_keys[c], qbd[cs], NT_DIMS, preferred_element_type=F32)
            if c in w_delta:
                s = s + delta_ref[w_delta[c]:w_delta[c] + TQ, cs]
            if c < n_wc - 1:
                s = s + jnp.where(w_pos0[c] < 0, NEG, 0.0)
            s_w.append(s)
        s_win.append(s_w)

    ns_t = []
    for imp in imp_g:
        score = jnp.where(forced, jnp.inf, imp)
        score = jnp.where(jidx > cur, -jnp.inf, score)
        cnt = jnp.zeros((n_sel, TQ), jnp.int32)
        for jp in range(n_sel):
            row = score[jp:jp + 1, :]
            cnt = cnt + jnp.where(jidx > jp, jnp.where(row >= score, 1, 0), jnp.where(row > score, 1, 0))
        sel = (cnt < min(N_SELECT, n_sel)) & (jidx <= cur)
        ns_all = jnp.where(sel, 0.0, NEG)
        ns_far = jnp.where(jidx < prev_blk, ns_all, NEG)
        ns_t.append([jnp.concatenate([x, pad], axis=0).T.astype(BF16) for x in (ns_all, ns_far)])
    for h in range(N_HEADS):
        rows_h = slice(h * TQ, (h + 1) * TQ)
        qa_s[rows_h, 0:LANE] = qbd[rows_h]
        qf_s[rows_h, 0:LANE] = qbd[rows_h]
        qa_s[rows_h, LANE:2 * LANE] = ns_t[h // GROUP][0]
        qf_s[rows_h, LANE:2 * LANE] = ns_t[h // GROUP][1]

    m_s[...] = jnp.full((1, cols), NEG, F32)
    acc_s[...] = jnp.zeros((VT_ROWS, cols), F32)

    def online_update(cs, s, vt):
        m_old = m_s[:, cs]
        m_new = jnp.maximum(m_old, _col_max(s))
        alpha = jnp.exp2(m_old - m_new)
        p = jnp.exp2(s - m_new)
        acc_s[:, cs] = alpha * acc_s[:, cs] + jnp.dot(vt, p.astype(BF16), preferred_element_type=F32)
        m_s[:, cs] = m_new

    def keys_with_block(k0, n):
        return jnp.concatenate([ksel_ref[0, pl.ds(k0, n), :], eblk_ref[pl.ds(k0, n), :]], axis=1)

    def masked_scores(kcat, q_s, cs):
        return lax.dot_general(kcat, q_s[cs, :], NT_DIMS, preferred_element_type=F32)

    prev_key = TQ * jnp.maximum(qb - 1, 0)
    n_far = (prev_key + FAR_TILE - 1) // FAR_TILE

    def far_scores(t):
        kcat = keys_with_block(pl.multiple_of(t * FAR_TILE, FAR_TILE), FAR_TILE)
        return tuple(masked_scores(kcat, qf_s, cs) for cs in chains)

    def far_update(t, s_all):
        vt = vst_ref[0, :, pl.ds(pl.multiple_of(t * FAR_TILE, FAR_TILE), FAR_TILE)]
        for cs, s in zip(chains, s_all):
            online_update(cs, s, vt)

    def far_tiles(t0, n):
        s_tiles = [far_scores(t0 + i) for i in range(n)]
        for i, s_all in enumerate(s_tiles):
            far_update(t0 + i, s_all)

    odd = lax.rem(n_far, 2)

    @pl.when(odd == 1)
    def _():
        far_tiles(0, 1)

    def pair_body(i, carry):
        far_tiles(odd + 2 * i, 2)
        return carry

    lax.fori_loop(0, lax.div(n_far, 2), pair_body, 0)

    k_prev = pl.multiple_of(prev_key, TQ)
    k_diag = pl.multiple_of(qb * TQ, TQ)
    no_prev = jnp.where(qb == 0, NEG, 0.0)
    kcat_prev = keys_with_block(k_prev, TQ)
    kcat_diag = keys_with_block(k_diag, TQ)
    vt_near = jnp.concatenate([vst_ref[0, :, pl.ds(k_prev, TQ)], vst_ref[0, :, pl.ds(k_diag, TQ)]], axis=1)
    s_near = []
    for cs in chains:
        s_prev = masked_scores(kcat_prev, qa_s, cs) + delta_ref[DD0:DD0 + TQ, cs] + no_prev
        s_diag = masked_scores(kcat_diag, qa_s, cs) + delta_ref[DD0 + TQ:DD0 + 2 * TQ, cs]
        s_near.append(jnp.concatenate([s_prev, s_diag], axis=0))

    for cs, s in zip(chains, s_near):
        online_update(cs, s, vt_near)
    l_sel = acc_s[KV_LANES:KV_LANES + 1, :]
    o_sel = acc_s[0:KV_LANES, :]

    o_w = []
    for s_w in s_win:
        m_w = s_w[0].max(axis=0, keepdims=True)
        for s in s_w[1:]:
            m_w = jnp.maximum(m_w, s.max(axis=0, keepdims=True))
        o_c = jnp.zeros((VT_ROWS, CHAIN_COLS), F32)
        for s, vt in zip(s_w, w_vt):
            o_c = o_c + jnp.dot(vt, jnp.exp2(s - m_w).astype(BF16), preferred_element_type=F32)
        o_w.append(o_c)
    o_win = jnp.concatenate(o_w, axis=1)
    l_win = o_win[KV_LANES:KV_LANES + 1, :]
    o_win = o_win[0:KV_LANES, :]

    gt = gate_ref[0].T
    gate_row = lambda x: jnp.concatenate([gt[N_BRANCH * h + x:N_BRANCH * h + x + 1, :] for h in range(N_HEADS)], axis=1)
    o_t = o_cmp * gate_row(0) + o_sel * (gate_row(1) / l_sel) + o_win * (gate_row(2) / l_win)
    pieces = []
    for h in range(N_HEADS):
        g = h // GROUP
        pieces.append(o_t[g * HEAD_DIM:(g + 1) * HEAD_DIM, h * TQ:(h + 1) * TQ])
    mix_ref[0, :, 0:ATT_WIDTH] = jnp.concatenate(pieces, axis=0).T.astype(BF16)

    u = u_ref[0]
    ext_s[0:POOL_HALO, :] = jnp.where(qb == 0, pst_ref[0], uh_ref[0])
    ext_s[POOL_HALO:POOL_HALO + TQ, :] = u
    pos = qb * TQ + lax.broadcasted_iota(jnp.int32, (TQ, 1), 0)
    for k, o in enumerate(_pool_mix(ext_s, u, pos, wpool_ref, pscale_ref, TQ)):
        mix_ref[0, :, ATT_WIDTH + k * POOL_GROUP_DIM:ATT_WIDTH + (k + 1) * POOL_GROUP_DIM] = o


def _prompt_attention(q, gates, u, pool_prev, kb, vst, vwt, ck, cvt, delta, mimp, eblk, wpool, pscale):
    b, t, _ = q.shape
    assert t % FAR_TILE == 0 and t >= WINDOW + TQ and t // SEL_BLOCK <= LANE
    n_cmp = ck.shape[1]
    cols = N_HEADS * TQ
    halo_per_q = TQ // POOL_HALO
    full = lambda a: pl.BlockSpec(a.shape, lambda i, j: (0,) * a.ndim)
    qblk = lambda c: pl.BlockSpec((1, TQ, c), lambda i, j: (i, j, 0))
    seq = lambda c: pl.BlockSpec((1, t, KV_LANES), lambda i, j: (i, 0, c))
    per_b = lambda a: pl.BlockSpec((1,) + a.shape[1:], lambda i, j: (i,) + (0,) * (a.ndim - 1))
    return pl.pallas_call(
        _prompt_attn_kernel,
        grid=(b, t // TQ),
        in_specs=[qblk(ATT_WIDTH), qblk(LANE), qblk(POOL_WIDTH),
                  pl.BlockSpec((1, POOL_HALO, POOL_WIDTH), lambda i, j: (i, jnp.maximum(j * halo_per_q - 1, 0), 0)),
                  per_b(pool_prev), seq(0), seq(1), per_b(vst), per_b(vwt), per_b(ck), per_b(cvt),
                  full(delta), full(mimp), full(eblk), full(wpool), full(pscale)],
        out_specs=pl.BlockSpec((1, TQ, D_MODEL), lambda i, j: (i, j, 0)),
        out_shape=jax.ShapeDtypeStruct((b, t, D_MODEL), BF16),
        scratch_shapes=[pltpu.VMEM((16 + n_cmp + 8, cols), F32),
                        pltpu.VMEM((cols, 2 * LANE), BF16), pltpu.VMEM((cols, 2 * LANE), BF16),
                        pltpu.VMEM((VT_ROWS, cols), F32), pltpu.VMEM((1, cols), F32),
                        pltpu.VMEM((POOL_HALO + TQ, POOL_WIDTH), F32)],
        compiler_params=pltpu.CompilerParams(dimension_semantics=("parallel", "arbitrary"),
                                             vmem_limit_bytes=VMEM_LIMIT),
    )(q, gates, u, u, pool_prev, kb, kb, vst, vwt, ck, cvt, delta, mimp, eblk, wpool, pscale)


SD_LAST = 0
SD_NEW = LANE
SD_EDGE = 2 * LANE
SD_CMP = 3 * LANE


def _sample_attn_kernel(pt_ref, q_ref, gate_ref, u_ref, kvs_ref, kvw_ref, swin_ref, spool_ref,
                        ccache_ref, scache_ref, nwin_in_ref, wcmp_ref, pe_ref, delta_ref, mimp_ref, expand_ref,
                        wpool_ref, pscale_ref,
                        mix_ref, nwin_ref, npool_ref,
                        cbuf, sbuf, xk_s, xv_s, sem, ext_s, *, layer):
    del nwin_in_ref
    b = pl.program_id(0)
    nb = pl.num_programs(0)
    n_pages = pt_ref.shape[1]
    past = n_pages * PAGE_SIZE
    n_sub = past // CMP_STRIDE
    tq = q_ref.shape[1]
    rows = N_HEADS * tq
    wb = swin_ref.shape[3]
    n_sel = past // SEL_BLOCK + 1
    slot = lax.rem(b, 2)

    def page_copies(seq, slt):
        cps = []
        for p in range(n_pages):
            pg = pt_ref[seq, p]
            dst = (slice(None), pl.ds(p * PAGE_SIZE, PAGE_SIZE))
            cps.append(pltpu.make_async_copy(ccache_ref.at[layer, pg], cbuf.at[slt].at[dst], sem.at[0, slt]))
            cps.append(pltpu.make_async_copy(scache_ref.at[layer, pg], sbuf.at[slt].at[dst], sem.at[1, slt]))
        return cps

    @pl.when(b == 0)
    def _():
        for cp in page_copies(0, 0):
            cp.start()

    @pl.when(b + 1 < nb)
    def _():
        for cp in page_copies(b + 1, 1 - slot):
            cp.start()

    for cp in page_copies(b, slot):
        cp.wait()

    q = q_ref[0].astype(F32)
    zeros = jnp.zeros((tq, HEAD_DIM), F32)
    blocks = []
    for h in range(N_HEADS):
        halves = [zeros] * N_KV
        halves[h // GROUP] = q[:, h * HEAD_DIM:(h + 1) * HEAD_DIM]
        blocks.append(jnp.concatenate(halves, axis=1))
    qbd = jnp.concatenate(blocks, axis=0).astype(BF16)

    def scores(k_rows):
        return lax.dot_general(qbd, k_rows, NT_DIMS, preferred_element_type=F32)

    def scores_t(k_t):
        return jnp.dot(qbd, k_t.astype(BF16), preferred_element_type=F32)

    def pad_rows(x):
        return jnp.concatenate([x, jnp.zeros((LANE - tq, KV_LANES), F32)], axis=0)

    pad_new = lambda x: pad_rows(x).astype(BF16)
    d_last = delta_ref[:, SD_LAST:SD_LAST + LANE]
    d_new = delta_ref[:, SD_NEW:SD_NEW + LANE]

    def softmax_pv(s_old, vt_old, s_new, v_new):
        m = jnp.maximum(s_old.max(axis=1, keepdims=True), s_new.max(axis=1, keepdims=True))
        p_old = jnp.exp2(s_old - m)
        p_new = jnp.exp2(s_new - m)
        l = jnp.sum(p_old, axis=1, keepdims=True) + jnp.sum(p_new, axis=1, keepdims=True)
        o = (lax.dot_general(p_old.astype(BF16), vt_old.astype(BF16), NT_DIMS, preferred_element_type=F32)
             + jnp.dot(p_new.astype(BF16), v_new, preferred_element_type=F32))
        return o, l

    kvw_new = kvw_ref[0]
    s_first = scores_t(swin_ref[0, 0, 0:KV_LANES, 0:LANE]) + delta_ref[:, SD_EDGE:SD_EDGE + LANE]
    s_mid = scores_t(swin_ref[0, 0, 0:KV_LANES, LANE:wb - LANE])
    s_lastw = scores_t(swin_ref[0, 0, 0:KV_LANES, wb - LANE:wb]) + d_last
    s_wnew = scores(pad_new(kvw_new[:, 0:KV_LANES])) + d_new
    o_win, l_win = softmax_pv(jnp.concatenate([s_first, s_mid, s_lastw], axis=1), swin_ref[0, 0, KV_LANES:KV_COLS, :],
                              s_wnew, pad_new(kvw_new[:, KV_LANES:KV_COLS]))

    shifted = pltpu.roll(swin_ref[0, 0], wb - tq, axis=1)
    new_t = jnp.concatenate([pad_rows(kvw_new[:, 0:KV_LANES]).T, pad_rows(kvw_new[:, KV_LANES:KV_COLS]).T], axis=0)
    new_t = pltpu.roll(new_t, LANE - tq, axis=1)
    lane = lax.broadcasted_iota(jnp.int32, (KV_COLS, LANE), 1)
    nwin_ref[0, 0, :, 0:wb - LANE] = shifted[:, 0:wb - LANE]
    nwin_ref[0, 0, :, wb - LANE:wb] = jnp.where(lane >= LANE - tq, new_t, shifted[:, wb - LANE:wb])

    kvs_new = kvs_ref[0]
    s_far = scores_t(sbuf[slot, 0:KV_LANES, 0:past - LANE])
    s_last = scores_t(sbuf[slot, 0:KV_LANES, past - LANE:past]) + d_last
    s_new = scores(pad_new(kvs_new[:, 0:KV_LANES])) + d_new

    pitch = xk_s.shape[0] // CMP_STRIDE
    sub_per_page = PAGE_SIZE // CMP_STRIDE
    for p in range(n_pages):
        tok = slice(p * PAGE_SIZE, (p + 1) * PAGE_SIZE)
        for x_s, feat in ((xk_s, slice(0, KV_LANES)), (xv_s, slice(KV_LANES, KV_COLS))):
            xt = cbuf[slot, feat, tok].T
            for v in range(PAGE_SIZE // 8):
                l0 = (8 * v) % CMP_STRIDE
                sub = p * sub_per_page + (8 * v) // CMP_STRIDE
                x_s[pl.ds(l0 * pitch + sub, 8, stride=pitch), :] = xt[8 * v:8 * v + 8, :]
    x_of = lambda l, j: (xk_s, xv_s)[j][l * pitch:l * pitch + n_sub, :]
    ck, cv = _compress_tokens(x_of, wcmp_ref, pe_ref, n_sub)
    sc = scores(ck.astype(BF16)) + delta_ref[:, SD_CMP:SD_CMP + n_sub]
    m_c = jnp.max(sc, axis=1, keepdims=True)
    p = jnp.exp2(sc - m_c)
    l_c = jnp.sum(p, axis=1, keepdims=True)
    pn = p * jnp.where(m_c > VALID_MIN, 1.0 / l_c, 0.0)
    o_cmp = jnp.dot(pn.astype(BF16), cv.astype(BF16), preferred_element_type=F32)

    jidx = lax.broadcasted_iota(jnp.int32, (tq, LANE), 1)
    cur = (past + lax.broadcasted_iota(jnp.int32, (tq, LANE), 0)) // SEL_BLOCK
    forced = (jidx == 0) | (jidx == cur) | (jidx == cur - 1)
    sel_g = []
    for g in range(N_KV):
        ps = pn[g * GROUP * tq:(g * GROUP + 1) * tq, :]
        for r in range(1, GROUP):
            ps = ps + pn[(g * GROUP + r) * tq:(g * GROUP + r + 1) * tq, :]
        hi = ps.astype(BF16)
        lo = (ps - hi.astype(F32)).astype(BF16)
        imp = (jnp.dot(hi, mimp_ref[...], preferred_element_type=F32)
               + jnp.dot(lo, mimp_ref[...], preferred_element_type=F32))
        score = jnp.where(forced, jnp.inf, imp)
        score = jnp.where((jidx > cur) | (jidx >= n_sel), -jnp.inf, score)
        cnt = jnp.zeros((tq, LANE), jnp.int32)
        for jp in range(n_sel):
            col = score[:, jp:jp + 1]
            cnt = cnt + jnp.where(jidx > jp, jnp.where(col >= score, 1, 0), jnp.where(col > score, 1, 0))
        sel = (cnt < min(N_SELECT, n_sel)) & (jidx <= cur) & (jidx < n_sel)
        sel_g.append(jnp.where(sel, 1.0, 0.0))
    sel_rows = jnp.concatenate([sel_g[h // GROUP] for h in range(N_HEADS)], axis=0).astype(BF16)
    mask = jnp.dot(sel_rows, expand_ref[...], preferred_element_type=F32)

    s_past = jnp.where(mask > 0.5, jnp.concatenate([s_far, s_last], axis=1), NEG)
    o_sel, l_sel = softmax_pv(s_past, sbuf[slot, KV_LANES:KV_COLS, :], s_new, pad_new(kvs_new[:, KV_LANES:KV_COLS]))

    gates = gate_ref[0]
    gate_col = lambda x: jnp.concatenate([gates[:, N_BRANCH * h + x:N_BRANCH * h + x + 1] for h in range(N_HEADS)], axis=0)
    o = o_cmp * gate_col(0) + o_sel * (gate_col(1) / l_sel) + o_win * (gate_col(2) / l_win)
    pieces = []
    for h in range(N_HEADS):
        g = h // GROUP
        pieces.append(o[h * tq:(h + 1) * tq, g * HEAD_DIM:(g + 1) * HEAD_DIM])
    mix_ref[0, :, 0:ATT_WIDTH] = jnp.concatenate(pieces, axis=1).astype(BF16)

    u = u_ref[0]
    ext_s[0:1, :] = jnp.zeros((1, POOL_WIDTH), F32)
    ext_s[1:POOL_HALO, :] = spool_ref[0, 0]
    ext_s[POOL_HALO:POOL_HALO + tq, :] = u
    pos = past + lax.broadcasted_iota(jnp.int32, (tq, 1), 0)
    for k, o_k in enumerate(_pool_mix(ext_s, u, pos, wpool_ref, pscale_ref, tq)):
        mix_ref[0, :, ATT_WIDTH + k * POOL_GROUP_DIM:ATT_WIDTH + (k + 1) * POOL_GROUP_DIM] = o_k
    npool_ref[0] = ext_s[POOL_HALO + tq - POOL_STATE:POOL_HALO + tq, :]


def _sample_attention(layer, page_table, q, gates, u, kvs, kvw, swin_t, state_pool, ccache_t, scache_t, nwin_all,
                      wcmp, pe_rows, delta, mimp, expand, wpool, pscale):
    nb, tq, _ = q.shape
    n_pages = page_table.shape[1]
    past = n_pages * PAGE_SIZE
    n_sub = past // CMP_STRIDE
    wb = swin_t.shape[3]
    assert wb == WINDOW and wb >= 3 * LANE and tq <= CMP_STRIDE and POOL_HALO + tq - POOL_STATE >= 0
    assert past // SEL_BLOCK + 1 <= LANE and past % LANE == 0 and n_sub <= LANE
    full = lambda a: pl.BlockSpec(a.shape, lambda i, pt: (0,) * a.ndim)
    per_b = lambda c: pl.BlockSpec((1, tq, c), lambda i, pt: (i, 0, 0))
    layer_b = lambda a: pl.BlockSpec((1, 1) + a.shape[2:], lambda i, pt: (layer, i) + (0,) * (a.ndim - 2))
    hbm = pl.BlockSpec(memory_space=pl.ANY)
    grid_spec = pltpu.PrefetchScalarGridSpec(
        num_scalar_prefetch=1,
        grid=(nb,),
        in_specs=[per_b(ATT_WIDTH), per_b(LANE), per_b(POOL_WIDTH), per_b(KV_COLS), per_b(KV_COLS),
                  layer_b(swin_t), layer_b(state_pool), hbm, hbm, hbm,
                  full(wcmp), full(pe_rows), full(delta), full(mimp), full(expand), full(wpool), full(pscale)],
        out_specs=[pl.BlockSpec((1, tq, D_MODEL), lambda i, pt: (i, 0, 0)),
                   layer_b(nwin_all),
                   pl.BlockSpec((1, POOL_STATE, POOL_WIDTH), lambda i, pt: (i, 0, 0))],
        scratch_shapes=[pltpu.VMEM((2, KV_COLS, past), F32), pltpu.VMEM((2, KV_COLS, past), F32),
                        pltpu.VMEM((CMP_STRIDE * (n_sub + 8), KV_LANES), F32),
                        pltpu.VMEM((CMP_STRIDE * (n_sub + 8), KV_LANES), F32),
                        pltpu.SemaphoreType.DMA((2, 2)), pltpu.VMEM((POOL_HALO + tq, POOL_WIDTH), F32)],
    )
    return pl.pallas_call(
        functools.partial(_sample_attn_kernel, layer=layer),
        grid_spec=grid_spec,
        out_shape=[jax.ShapeDtypeStruct((nb, tq, D_MODEL), BF16),
                   jax.ShapeDtypeStruct(nwin_all.shape, F32),
                   jax.ShapeDtypeStruct((nb, POOL_STATE, POOL_WIDTH), F32)],
        input_output_aliases={10: 1},
        compiler_params=pltpu.CompilerParams(dimension_semantics=("arbitrary",), vmem_limit_bytes=VMEM_LIMIT),
    )(page_table, q, gates, u, kvs, kvw, swin_t, state_pool, ccache_t, scache_t, nwin_all,
      wcmp, pe_rows, delta, mimp, expand, wpool, pscale)


def _ffn_kernel(x_ref, mix_ref, wout_ref, gn_ref, win_ref, wd_ref, gf_ref, o_ref, *, final_norm):
    xm = x_ref[...] + jnp.dot(mix_ref[...], wout_ref[...], preferred_element_type=F32)
    h = _rms(xm, gn_ref[...]).astype(BF16)
    d_ff = wd_ref.shape[0]
    cuts = list(range(0, d_ff, FFN_SLICE)) + [d_ff]
    slices = list(zip(cuts[:-1], cuts[1:]))

    def gate_up(a, b):
        return (jnp.dot(h, win_ref[:, a:b], preferred_element_type=F32),
                jnp.dot(h, win_ref[:, d_ff + a:d_ff + b], preferred_element_type=F32))

    y = xm
    pending = gate_up(*slices[0])
    for j, (a, b) in enumerate(slices):
        gate, up = pending
        if j + 1 < len(slices):
            pending = gate_up(*slices[j + 1])
        act = (gate * _sigmoid(gate) * up).astype(BF16)
        y = y + jnp.dot(act, wd_ref[a:b, :], preferred_element_type=F32)
    o_ref[...] = _rms(y, gf_ref[...]) if final_norm else y


def _ffn(x, mix, wout, gn, wffn_in, wffn_out, gf, final_norm):
    n = x.shape[0]
    tm = min(ROW_TILE, n)
    assert n % tm == 0 and wffn_out.shape[0] % LANE == 0
    once = lambda a: pl.BlockSpec(a.shape, lambda i: (0,) * a.ndim, pipeline_mode=pl.Buffered(1))
    row = lambda cdim: pl.BlockSpec((tm, cdim), lambda i: (i, 0))
    return pl.pallas_call(
        functools.partial(_ffn_kernel, final_norm=final_norm),
        grid=(n // tm,),
        in_specs=[row(D_MODEL), row(D_MODEL), once(wout), once(gn), once(wffn_in), once(wffn_out), once(gf)],
        out_specs=row(D_MODEL),
        out_shape=jax.ShapeDtypeStruct((n, D_MODEL), F32),
        compiler_params=pltpu.CompilerParams(dimension_semantics=("parallel",), vmem_limit_bytes=VMEM_LIMIT),
    )(x, mix, wout, gn, wffn_in, wffn_out, gf)


def _importance_matrix(n_sel, n_cmp_rows, n_cmp):
    spb = SEL_BLOCK // CMP_STRIDE
    j = np.arange(n_sel)[:, None]
    c = np.arange(n_cmp_rows)[None, :]
    return ((c >= spb * j - (CMP_RATIO - 1)) & (c <= spb * j + spb - 1) & (c < n_cmp)).astype(np.float32)


def kernel(x_prompt, x_sample, cache_cmp, cache_sel, state_win, state_pool, page_table, rel_bias, norm_mix, norm_ffn,
           norm_final, w_in, w_out, cmp_pos, w_cmp, w_pool, pool_scale, w_ffn_in, w_ffn_out):
    bp, t, _ = x_prompt.shape
    bs, tq, _ = x_sample.shape
    depth = w_in.shape[0]
    n_pages = page_table.shape[1]
    past = n_pages * PAGE_SIZE
    n_phys = cache_cmp.shape[1]
    wb = state_win.shape[2]
    _check_far_bucket(max(t, past + tq) + WINDOW)

    rb_prompt = jnp.repeat(rel_bias.astype(F32), TQ, axis=1)
    delta_p = _bias_tiles(_prompt_bucket_table(), rb_prompt)
    rb_sample = jnp.pad(jnp.repeat(rel_bias.astype(F32), tq, axis=1), ((0, 0), (0, LANE - N_HEADS * tq)))
    delta_s = _bias_tiles(_sample_bucket_table(past, tq, wb), rb_sample).T[:N_HEADS * tq]

    n_sub_p = t // CMP_STRIDE
    mimp_p = jnp.asarray(_importance_matrix(t // SEL_BLOCK, n_sub_p, n_sub_p - CMP_RATIO + 1), BF16)
    eblk_p = jnp.asarray((np.arange(t)[:, None] // SEL_BLOCK == np.arange(LANE)[None, :]).astype(np.float32), BF16)
    n_sub_s = past // CMP_STRIDE
    mimp_s =jnp.asarray(_importance_matrix(LANE, n_sub_s, n_sub_s - CMP_RATIO + 1).T, BF16)
    expand = jnp.asarray((np.arange(LANE)[:, None] == np.arange(past)[None, :] // SEL_BLOCK).astype(np.float32), BF16)

    pool_zero = jnp.zeros((bp, POOL_HALO, POOL_WIDTH), F32)
    feature_major = lambda a: jnp.moveaxis(a, 2, -1).reshape(a.shape[0], a.shape[1], KV_COLS, a.shape[2])
    token_major = lambda a: jnp.moveaxis(a.reshape(a.shape[:2] + (2, N_KV, HEAD_DIM, a.shape[3])), -1, 2)
    ccache_t = feature_major(cache_cmp)
    scache_t = feature_major(cache_sel)
    swin_t = feature_major(state_win)
    kvct_all = jnp.zeros((depth, bp, KV_COLS, t), F32)
    kvst_all = jnp.zeros((depth, bp, KV_COLS, t), F32)
    nwin_all = jnp.zeros((depth, bs, KV_COLS, wb), F32)

    xp = x_prompt.reshape(bp * t, D_MODEL)
    xs = x_sample.reshape(bs * tq, D_MODEL)
    outs = {k: [] for k in ("p_win", "p_pool", "s_cmp", "s_sel", "s_pool")}
    kv_shape = lambda b, n: (b, n, 2, N_KV, HEAD_DIM)
    off_g = ATT_WIDTH + N_BRANCH * KV_COLS
    for l in range(depth):
        wl = w_in[l]
        w_all = jnp.concatenate([wl[:, :ATT_WIDTH] * (HEAD_DIM ** -0.5 * LOG2E), wl[:, ATT_WIDTH:off_g],
                                 wl[:, off_g + GATE_COLS:], wl[:, off_g:off_g + GATE_COLS],
                                 jnp.zeros((D_MODEL, LANE - GATE_COLS), F32)], axis=1).astype(BF16)
        g_mix = norm_mix[l].reshape(1, D_MODEL)
        g_ffn = norm_ffn[l].reshape(1, D_MODEL)
        g_fin = norm_final.reshape(1, D_MODEL)
        wc = w_cmp[l].reshape(2, CMP_RATIO, CMP_STRIDE, HEAD_DIM, HEAD_DIM)
        zero = jnp.zeros_like(wc)
        w_bd = jnp.concatenate([jnp.concatenate([wc, zero], axis=-1), jnp.concatenate([zero, wc], axis=-1)],
                               axis=-2).astype(BF16)
        pe = cmp_pos[l].reshape(CMP_RATIO, CMP_STRIDE, 2, HEAD_DIM).transpose(2, 0, 1, 3)
        pe_rows = jnp.tile(pe.reshape(2 * CMP_RATIO * CMP_STRIDE, HEAD_DIM), (1, N_KV)).astype(F32)
        wpool = w_pool[l].astype(BF16)
        pscale = pool_scale[l].reshape(1, POOL_WIDTH)
        wout = w_out[l].astype(BF16)
        wffn_in = w_ffn_in[l].astype(BF16)
        wffn_out = w_ffn_out[l].astype(BF16)
        last = l == depth - 1

        q, u, gates, kvc, kb, vst, vwt, kvct_all, kvst_all, kvwt = _inproj_seq(l, xp, g_mix, w_all, kvct_all, kvst_all)
        ck, cvt = _compress_prompt(kvc.reshape(bp, t, KV_COLS), w_bd, pe_rows)
        mix = _prompt_attention(q.reshape(bp, t, ATT_WIDTH), gates.reshape(bp, t, LANE), u.reshape(bp, t, POOL_WIDTH),
                                pool_zero, kb.reshape(bp, t, KV_COLS), vst, vwt, ck, cvt, delta_p, mimp_p, eblk_p,
                                wpool, pscale)
        xp = _ffn(xp, mix.reshape(bp * t, D_MODEL), wout, g_ffn, wffn_in, wffn_out, g_fin, last)
        outs["p_win"].append(kvwt[:, :, t - wb:])
        outs["p_pool"].append(u.reshape(bp, t, POOL_WIDTH)[:, t - POOL_STATE:])

        q, u, gates, kvc, kvs, kvw = _inproj_rows(xs, g_mix, w_all)
        mix, nwin_all, npool = _sample_attention(
            l, page_table, q.reshape(bs, tq, ATT_WIDTH), gates.reshape(bs, tq, LANE), u.reshape(bs, tq, POOL_WIDTH),
            kvs.reshape(bs, tq, KV_COLS), kvw.reshape(bs, tq, KV_COLS), swin_t, state_pool, ccache_t, scache_t,
            nwin_all, w_bd, pe_rows, delta_s, mimp_s, expand, wpool, pscale)
        xs = _ffn(xs, mix.reshape(bs * tq, D_MODEL), wout, g_ffn, wffn_in, wffn_out, g_fin, last)
        outs["s_cmp"].append(kvc.reshape(kv_shape(bs, tq)))
        outs["s_sel"].append(kvs.reshape(kv_shape(bs, tq)))
        outs["s_pool"].append(npool)

    return (xp.reshape(bp, t, D_MODEL), xs.reshape(bs, tq, D_MODEL),
            token_major(kvct_all), token_major(kvst_all), token_major(jnp.stack(outs["p_win"])),
            jnp.stack(outs["p_pool"]), jnp.stack(outs["s_cmp"]), jnp.stack(outs["s_sel"]), token_major(nwin_all),
            jnp.stack(outs["s_pool"]))
```

```python
import functools
import math

import numpy as np
import jax
import jax.numpy as jnp
from jax import lax
from jax.experimental import pallas as pl
from jax.experimental.pallas import tpu as pltpu

D_MODEL = 1024
HEAD_DIM = 64
N_HEADS = 8
N_KV = 2
GROUP = N_HEADS // N_KV
ATT_WIDTH = N_HEADS * HEAD_DIM
POOL_WIDTH = D_MODEL - ATT_WIDTH
KV_COLS = 2 * N_KV * HEAD_DIM
KV_LANES = N_KV * HEAD_DIM
VT_ROWS = KV_LANES + 16
CMP_LEN = 32
CMP_STRIDE = 16
CMP_RATIO = CMP_LEN // CMP_STRIDE
SEL_BLOCK = 64
N_SELECT = 16
WINDOW = 512
N_BRANCH = 3
GATE_COLS = N_HEADS * N_BRANCH
POOL_WINDOWS = (2, 4, 8, 16)
POOL_GROUP_DIM = POOL_WIDTH // len(POOL_WINDOWS)
POOL_STATE = max(POOL_WINDOWS) - 1
POOL_HALO = 16
N_BUCKETS = 32
MAX_DISTANCE = 128
PAGE_SIZE = 128
EPS = 1e-6
NEG = -1e30
VALID_MIN = -1e29

LANE = 128
TQ = 128
FAR_TILE = 512
CHAIN_COLS = 256
LOG2E = 1.4426950408889634
ROW_TILE = 512
FFN_SLICE = 512
VMEM_LIMIT = 52 * 1024 * 1024

F32 = jnp.float32
BF16 = jnp.bfloat16
NT_DIMS = (((1,), (1,)), ((), ()))


def _bucket_np(dist):
    n = np.maximum(dist, 0)
    max_exact = N_BUCKETS // 2
    nf = np.maximum(n, 1).astype(np.float32)
    large = max_exact + (np.log(nf / max_exact) / math.log(MAX_DISTANCE / max_exact)
                         * (N_BUCKETS - max_exact)).astype(np.int32)
    return np.where(n < max_exact, n, np.minimum(large, N_BUCKETS - 1)).astype(np.int32)


def _bucket_or_masked(dist, valid):
    return np.where(valid, _bucket_np(dist), -1).astype(np.int32)


FAR_DIST = TQ + 1


def _check_far_bucket(max_dist):
    assert (_bucket_np(np.arange(FAR_DIST, max_dist + 1)) == N_BUCKETS - 1).all()


def _bias_tile_kernel(bucket_ref, rb_ref, out_ref):
    bucket = bucket_ref[...]
    acc = jnp.zeros(bucket.shape, F32)
    for b in range(N_BUCKETS):
        acc = acc + jnp.where(bucket == b, rb_ref[b:b + 1, :], 0.0)
    acc = (acc - rb_ref[N_BUCKETS - 1:N_BUCKETS, :]) * LOG2E
    out_ref[...] = jnp.where(bucket < 0, NEG, acc)


def _bias_tiles(bucket_np, rb_cols):
    rows, cols = bucket_np.shape
    return pl.pallas_call(
        _bias_tile_kernel,
        out_shape=jax.ShapeDtypeStruct((rows, cols), F32),
    )(jnp.asarray(bucket_np), rb_cols)


DD0 = 0
DC0 = 2 * TQ
DC_ROWS = 24
DW0 = DC0 + 32
PROMPT_DELTA_ROWS = DW0 + TQ


def _prompt_bucket_table():
    i = np.arange(TQ)[None, :]
    kk = np.arange(2 * TQ)[:, None]
    d = i + TQ - kk
    dd = _bucket_or_masked(d, d >= 0)
    cc = np.arange(32)[:, None] - 16
    d = i - CMP_STRIDE * cc - (CMP_LEN - 1)
    dc = _bucket_or_masked(d, (d >= 0) & (cc < 8))
    j = np.arange(TQ)[:, None]
    d = i + WINDOW - j
    dw = _bucket_or_masked(d, (d >= 0) & (d < WINDOW))
    tab = np.concatenate([dd, dc, dw], axis=0)
    return np.tile(tab, (1, N_HEADS))


def _sample_bucket_table(past, tq, wb):
    col = np.arange(LANE)
    t = (col % tq)[None, :]
    colok = (col < N_HEADS * tq)[None, :]
    kk = np.arange(LANE)[:, None]
    d = LANE + t - kk
    dlast = _bucket_or_masked(d, colok & (d >= 0))
    d = t - kk
    dnew = _bucket_or_masked(d, colok & (d >= 0) & (kk < tq))
    d = wb + t - kk
    dedge = _bucket_or_masked(d, colok & (d >= 0) & (d < WINDOW))
    nsub = past // CMP_STRIDE
    c = np.arange(nsub)[:, None]
    d = past + t - CMP_STRIDE * c - (CMP_LEN - 1)
    dcmp = _bucket_or_masked(d, colok & (d >= 0) & (c < nsub - CMP_RATIO + 1))
    return np.concatenate([dlast, dnew, dedge, dcmp], axis=0)


IN_Q = 0
IN_KV = ATT_WIDTH
IN_U = IN_KV + N_BRANCH * KV_COLS
IN_G = IN_U + POOL_WIDTH
IN_COLS_PAD = IN_G + LANE


def _rms(x, g):
    return x * lax.rsqrt(jnp.mean(x * x, axis=-1, keepdims=True) + EPS) * g


def _sigmoid(x):
    return 1.0 / (1.0 + jnp.exp(-x))


def _inproj_common(x_ref, g_ref, w_ref, q_ref, u_ref, gate_ref):
    h = _rms(x_ref[...], g_ref[...]).astype(BF16)
    z = jnp.dot(h, w_ref[...], preferred_element_type=F32)
    q_ref[...] = z[:, IN_Q:IN_KV].astype(BF16)
    u_ref[...] = z[:, IN_U:IN_G]
    gate_ref[...] = _sigmoid(z[:, IN_G:IN_COLS_PAD])
    return [z[:, IN_KV + i * KV_COLS:IN_KV + (i + 1) * KV_COLS] for i in range(N_BRANCH)]


def _inproj_rows_kernel(x_ref, g_ref, w_ref, q_ref, u_ref, gate_ref, kvc_ref, kvs_ref, kvw_ref):
    kvc, kvs, kvw = _inproj_common(x_ref, g_ref, w_ref, q_ref, u_ref, gate_ref)
    kvc_ref[...] = kvc
    kvs_ref[...] = kvs
    kvw_ref[...] = kvw


def _inproj_seq_kernel(x_ref, g_ref, w_ref, cbuf_ref, sbuf_ref, q_ref, u_ref, gate_ref, kvc_ref, kb_ref,
                       vst_ref, vwt_ref, kvct_ref, kvst_ref, kvwt_ref):
    del cbuf_ref, sbuf_ref
    kvc, kvs, kvw = _inproj_common(x_ref, g_ref, w_ref, q_ref, u_ref, gate_ref)
    kvc_ref[...] = kvc
    kb_ref[...] = jnp.concatenate([kvs[:, 0:KV_LANES], kvw[:, 0:KV_LANES]], axis=1).astype(BF16)
    kvst = kvs.T
    kvwt = kvw.T
    kvct_ref[0, 0] = kvc.T
    kvst_ref[0, 0] = kvst
    kvwt_ref[0] = kvwt
    ones = jnp.ones((VT_ROWS - KV_LANES, kvst.shape[1]), F32)
    vst_ref[0] = jnp.concatenate([kvst[KV_LANES:KV_COLS], ones], axis=0).astype(BF16)
    vwt_ref[0] = jnp.concatenate([kvwt[KV_LANES:KV_COLS], ones], axis=0).astype(BF16)


def _inproj_rows(x, g, w):
    n = x.shape[0]
    tm = min(ROW_TILE, n)
    row = lambda c: pl.BlockSpec((tm, c), lambda i: (i, 0))
    full = lambda a: pl.BlockSpec(a.shape, lambda i: (0,) * a.ndim)
    outs = [(ATT_WIDTH, BF16), (POOL_WIDTH, F32), (LANE, F32), (KV_COLS, F32), (KV_COLS, F32), (KV_COLS, F32)]
    return pl.pallas_call(
        _inproj_rows_kernel,
        grid=(n // tm,),
        in_specs=[row(D_MODEL), full(g), full(w)],
        out_specs=[row(c) for c, _ in outs],
        out_shape=[jax.ShapeDtypeStruct((n, c), dt) for c, dt in outs],
        compiler_params=pltpu.CompilerParams(dimension_semantics=("parallel",), vmem_limit_bytes=VMEM_LIMIT),
    )(x, g, w)


def _inproj_seq(layer, x, g, w, kvct_all, kvst_all):
    n = x.shape[0]
    _, b, _, t = kvct_all.shape
    tm = min(ROW_TILE, t)
    tiles = t // tm
    row = lambda c: pl.BlockSpec((tm, c), lambda i: (i, 0))
    full = lambda a: pl.BlockSpec(a.shape, lambda i: (0,) * a.ndim)
    hbm = pl.BlockSpec(memory_space=pl.ANY)
    seq_t = lambda r: pl.BlockSpec((1, r, tm), lambda i: (i // tiles, 0, i % tiles))
    all_t = pl.BlockSpec((1, 1, KV_COLS, tm), lambda i: (layer, i // tiles, 0, i % tiles))
    rows = [(ATT_WIDTH, BF16), (POOL_WIDTH, F32), (LANE, F32), (KV_COLS, F32), (KV_COLS, BF16)]
    return pl.pallas_call(
        _inproj_seq_kernel,
        grid=(n // tm,),
        in_specs=[row(D_MODEL), full(g), full(w), hbm, hbm],
        out_specs=[row(c) for c, _ in rows] + [seq_t(VT_ROWS), seq_t(VT_ROWS), all_t, all_t, seq_t(KV_COLS)],
        out_shape=[jax.ShapeDtypeStruct((n, c), dt) for c, dt in rows]
        + [jax.ShapeDtypeStruct((b, VT_ROWS, t), BF16), jax.ShapeDtypeStruct((b, VT_ROWS, t), BF16),
           jax.ShapeDtypeStruct(kvct_all.shape, F32), jax.ShapeDtypeStruct(kvst_all.shape, F32),
           jax.ShapeDtypeStruct((b, KV_COLS, t), F32)],
        input_output_aliases={3: 7, 4: 8},
        compiler_params=pltpu.CompilerParams(dimension_semantics=("parallel",), vmem_limit_bytes=VMEM_LIMIT),
    )(x, g, w, kvct_all, kvst_all)


def _compress_tokens(x_of, w_ref, pe_ref, n_sub):
    outs = []
    for j in range(2):
        parts = [jnp.zeros((n_sub, KV_LANES), F32) for _ in range(CMP_RATIO)]
        for l in range(CMP_STRIDE):
            x = x_of(l, j)
            for r in range(CMP_RATIO):
                row = (j * CMP_RATIO + r) * CMP_STRIDE + l
                a = (x + pe_ref[row:row + 1, :]).astype(BF16)
                parts[r] = parts[r] + jnp.dot(a, w_ref[j, r, l], preferred_element_type=F32)
        comp = parts[0]
        for r in range(1, CMP_RATIO):
            comp = comp + pltpu.roll(parts[r], n_sub - r, axis=0)
        outs.append(comp)
    return outs


def _compress_kernel(x_ref, w_ref, pe_ref, ck_ref, cvt_ref):
    n_sub = x_ref.shape[1]
    x_of = lambda l, j: x_ref[0, :, l * KV_COLS + j * KV_LANES:l * KV_COLS + (j + 1) * KV_LANES]
    ck, cv = _compress_tokens(x_of, w_ref, pe_ref, n_sub)
    ck_ref[0] = ck.astype(BF16)
    cvt_ref[0] = cv.T.astype(BF16)


def _compress_prompt(kvc, w_bd, pe_rows):
    b, t, _ = kvc.shape
    n_sub = t // CMP_STRIDE
    x = kvc.reshape(b, n_sub, CMP_STRIDE * KV_COLS)
    full = lambda a: pl.BlockSpec(a.shape, lambda i: (0,) * a.ndim)
    return pl.pallas_call(
        _compress_kernel,
        grid=(b,),
        in_specs=[pl.BlockSpec((1, n_sub, CMP_STRIDE * KV_COLS), lambda i: (i, 0, 0)), full(w_bd), full(pe_rows)],
        out_specs=[pl.BlockSpec((1, n_sub, KV_LANES), lambda i: (i, 0, 0)),
                   pl.BlockSpec((1, KV_LANES, n_sub), lambda i: (i, 0, 0))],
        out_shape=[jax.ShapeDtypeStruct((b, n_sub, KV_LANES), BF16), jax.ShapeDtypeStruct((b, KV_LANES, n_sub), BF16)],
        compiler_params=pltpu.CompilerParams(dimension_semantics=("parallel",), vmem_limit_bytes=VMEM_LIMIT),
    )(x, w_bd, pe_rows)


def _split_dot(m, p):
    hi = p.astype(BF16)
    lo = (p - hi.astype(F32)).astype(BF16)
    return jnp.dot(m, hi, preferred_element_type=F32) + jnp.dot(m, lo, preferred_element_type=F32)


def _col_max(s):
    rows = s.shape[0]
    parts = [s[r:r + 64] for r in range(0, rows, 64)] if rows % 64 == 0 and rows > 64 else [s]
    while len(parts) > 1:
        parts = [jnp.maximum(a, b) for a, b in zip(parts[0::2], parts[1::2])] + ([parts[-1]] if len(parts) % 2 else [])
    return jnp.max(parts[0], axis=0, keepdims=True)


def _pool_mix(ext_ref, u, pos, wpool_ref, pscale_ref, rows):
    outs = []
    for k, w in enumerate(POOL_WINDOWS):
        ln = slice(k * POOL_GROUP_DIM, (k + 1) * POOL_GROUP_DIM)
        ws = ext_ref[POOL_HALO:POOL_HALO + rows, ln]
        for s in range(1, w):
            ws = ws + ext_ref[POOL_HALO - s:POOL_HALO - s + rows, ln]
        cnt = jnp.minimum(pos + 1, w).astype(F32)
        pooled = (ws / cnt - u[:, ln]).astype(BF16)
        o = jnp.dot(pooled, wpool_ref[k], preferred_element_type=F32) * pscale_ref[:, ln]
        outs.append(o.astype(BF16))
    return outs


def _prompt_attn_kernel(q_ref, gate_ref, u_ref, uh_ref, pst_ref, ksel_ref, kwin_ref, vst_ref, vwt_ref,
                        ck_ref, cvt_ref, delta_ref, mimp_ref, eblk_ref, wpool_ref, pscale_ref,
                        mix_ref,
                        sc_s, qa_s, qf_s, acc_s, m_s, ext_s):
    qb = pl.program_id(1)
    t_len = ksel_ref.shape[1]
    n_cmp = ck_ref.shape[1]
    n_sel = t_len // SEL_BLOCK
    cols = N_HEADS * TQ
    blk_per_q = TQ // SEL_BLOCK
    n_chain = cols // CHAIN_COLS
    chains = [slice(c * CHAIN_COLS, (c + 1) * CHAIN_COLS) for c in range(n_chain)]

    q = q_ref[0].astype(F32)
    zeros = jnp.zeros((TQ, HEAD_DIM), F32)
    blocks = []
    for h in range(N_HEADS):
        qh = q[:, h * HEAD_DIM:(h + 1) * HEAD_DIM]
        halves = [zeros] * N_KV
        halves[h // GROUP] = qh
        blocks.append(jnp.concatenate(halves, axis=1))
    qbd = jnp.concatenate(blocks, axis=0).astype(BF16)

    def scores(k_rows):
        return lax.dot_general(k_rows, qbd, NT_DIMS, preferred_element_type=F32)

    sc = scores(ck_ref[0])
    crow = lax.broadcasted_iota(jnp.int32, (n_cmp, cols), 0)
    sc = jnp.where(crow >= 8 * qb + 8, NEG, sc)
    sc_s[0:16, :] = jnp.zeros((16, cols), F32)
    sc_s[16 + n_cmp:16 + n_cmp + 8, :] = jnp.zeros((8, cols), F32)
    sc_s[16:16 + n_cmp, :] = sc
    near = pl.ds(pl.multiple_of(8 * qb, 8), DC_ROWS)
    sc_s[near, :] = sc_s[near, :] + delta_ref[DC0:DC0 + DC_ROWS, :]
    sc = sc_s[16:16 + n_cmp, :]
    m_c = jnp.max(sc, axis=0, keepdims=True)
    p = jnp.exp2(sc - m_c)
    l_c = jnp.sum(p, axis=0, keepdims=True)
    pn = p * jnp.where(m_c > VALID_MIN, 1.0 / l_c, 0.0)
    o_cmp = jnp.dot(cvt_ref[0], pn.astype(BF16), preferred_element_type=F32)

    jidx = lax.broadcasted_iota(jnp.int32, (n_sel, TQ), 0)
    qi = lax.broadcasted_iota(jnp.int32, (n_sel, TQ), 1)
    cur = blk_per_q * qb + qi // SEL_BLOCK
    forced = (jidx == 0) | (jidx == cur) | (jidx == cur - 1)
    prev_blk = blk_per_q * jnp.maximum(qb - 1, 0)
    pad = jnp.zeros((LANE - n_sel, TQ), F32)
    imp_g = []
    for g in range(N_KV):
        ps = pn[:, (g * GROUP) * TQ:(g * GROUP + 1) * TQ]
        for r in range(1, GROUP):
            ps = ps + pn[:, (g * GROUP + r) * TQ:(g * GROUP + r + 1) * TQ]
        imp_g.append(_split_dot(mimp_ref[...], ps))

    n_wc = WINDOW // TQ + 1
    w_pos0 = [(qb - (n_wc - 1) + c) * TQ for c in range(n_wc)]
    w_k0 = [pl.multiple_of(jnp.maximum(p0, 0), TQ) for p0 in w_pos0]
    w_keys = [kwin_ref[0, pl.ds(k0, TQ), :] for k0 in w_k0]
    w_vt = [vwt_ref[0, :, pl.ds(k0, TQ)] for k0 in w_k0]
    w_delta = {0: DW0, n_wc - 2: DD0, n_wc - 1: DD0 + TQ}
    s_win = []
    for cs in chains:
        s_w = []
        for c in range(n_wc):
            s = lax.dot_general(w_keys[c], qbd[cs], NT_DIMS, preferred_element_type=F32)
            if c in w_delta:
                s = s + delta_ref[w_delta[c]:w_delta[c] + TQ, cs]
            if c < n_wc - 1:
                s = s + jnp.where(w_pos0[c] < 0, NEG, 0.0)
            s_w.append(s)
        s_win.append(s_w)

    ns_t = []
    for imp in imp_g:
        score = jnp.where(forced, jnp.inf, imp)
        score = jnp.where(jidx > cur, -jnp.inf, score)
        cnt = jnp.zeros((n_sel, TQ), jnp.int32)
        for jp in range(n_sel):
            row = score[jp:jp + 1, :]
            cnt = cnt + jnp.where(jidx > jp, jnp.where(row >= score, 1, 0), jnp.where(row > score, 1, 0))
        sel = (cnt < min(N_SELECT, n_sel)) & (jidx <= cur)
        ns_all = jnp.where(sel, 0.0, NEG)
        ns_far = jnp.where(jidx < prev_blk, ns_all, NEG)
        ns_t.append([jnp.concatenate([x, pad], axis=0).T.astype(BF16) for x in (ns_all, ns_far)])
    for h in range(N_HEADS):
        rows_h = slice(h * TQ, (h + 1) * TQ)
        qa_s[rows_h, 0:LANE] = qbd[rows_h]
        qf_s[rows_h, 0:LANE] = qbd[rows_h]
        qa_s[rows_h, LANE:2 * LANE] = ns_t[h // GROUP][0]
        qf_s[rows_h, LANE:2 * LANE] = ns_t[h // GROUP][1]

    m_s[...] = jnp.full((1, cols), NEG, F32)
    acc_s[...] = jnp.zeros((VT_ROWS, cols), F32)

    def online_update(cs, s, vt):
        m_old = m_s[:, cs]
        m_new = jnp.maximum(m_old, _col_max(s))
        alpha = jnp.exp2(m_old - m_new)
        p = jnp.exp2(s - m_new)
        acc_s[:, cs] = alpha * acc_s[:, cs] + jnp.dot(vt, p.astype(BF16), preferred_element_type=F32)
        m_s[:, cs] = m_new

    def keys_with_block(k0, n):
        return jnp.concatenate([ksel_ref[0, pl.ds(k0, n), :], eblk_ref[pl.ds(k0, n), :]], axis=1)

    def masked_scores(kcat, q_s, cs):
        return lax.dot_general(kcat, q_s[cs, :], NT_DIMS, preferred_element_type=F32)

    prev_key = TQ * jnp.maximum(qb - 1, 0)
    n_far = (prev_key + FAR_TILE - 1) // FAR_TILE

    def far_scores(t):
        kcat = keys_with_block(pl.multiple_of(t * FAR_TILE, FAR_TILE), FAR_TILE)
        return tuple(masked_scores(kcat, qf_s, cs) for cs in chains)

    def far_update(t, s_all):
        vt = vst_ref[0, :, pl.ds(pl.multiple_of(t * FAR_TILE, FAR_TILE), FAR_TILE)]
        for cs, s in zip(chains, s_all):
            online_update(cs, s, vt)

    def far_tiles(t0, n):
        s_tiles = [far_scores(t0 + i) for i in range(n)]
        for i, s_all in enumerate(s_tiles):
            far_update(t0 + i, s_all)

    odd = lax.rem(n_far, 2)

    @pl.when(odd == 1)
    def _():
        far_tiles(0, 1)

    def pair_body(i, carry):
        far_tiles(odd + 2 * i, 2)
        return carry

    lax.fori_loop(0, lax.div(n_far, 2), pair_body, 0)

    k_prev = pl.multiple_of(prev_key, TQ)
    k_diag = pl.multiple_of(qb * TQ, TQ)
    no_prev = jnp.where(qb == 0, NEG, 0.0)
    kcat_prev = keys_with_block(k_prev, TQ)
    kcat_diag = keys_with_block(k_diag, TQ)
    vt_near = jnp.concatenate([vst_ref[0, :, pl.ds(k_prev, TQ)], vst_ref[0, :, pl.ds(k_diag, TQ)]], axis=1)
    s_near = []
    for cs in chains:
        s_prev = masked_scores(kcat_prev, qa_s, cs) + delta_ref[DD0:DD0 + TQ, cs] + no_prev
        s_diag = masked_scores(kcat_diag, qa_s, cs) + delta_ref[DD0 + TQ:DD0 + 2 * TQ, cs]
        s_near.append(jnp.concatenate([s_prev, s_diag], axis=0))

    for cs, s in zip(chains, s_near):
        online_update(cs, s, vt_near)
    l_sel = acc_s[KV_LANES:KV_LANES + 1, :]
    o_sel = acc_s[0:KV_LANES, :]

    o_w = []
    for s_w in s_win:
        m_w = s_w[0].max(axis=0, keepdims=True)
        for s in s_w[1:]:
            m_w = jnp.maximum(m_w, s.max(axis=0, keepdims=True))
        o_c = jnp.zeros((VT_ROWS, CHAIN_COLS), F32)
        for s, vt in zip(s_w, w_vt):
            o_c = o_c + jnp.dot(vt, jnp.exp2(s - m_w).astype(BF16), preferred_element_type=F32)
        o_w.append(o_c)
    o_win = jnp.concatenate(o_w, axis=1)
    l_win = o_win[KV_LANES:KV_LANES + 1, :]
    o_win = o_win[0:KV_LANES, :]

    gt = gate_ref[0].T
    gate_row = lambda x: jnp.concatenate([gt[N_BRANCH * h + x:N_BRANCH * h + x + 1, :] for h in range(N_HEADS)], axis=1)
    o_t = o_cmp * gate_row(0) + o_sel * (gate_row(1) / l_sel) + o_win * (gate_row(2) / l_win)
    pieces = []
    for h in range(N_HEADS):
        g = h // GROUP
        pieces.append(o_t[g * HEAD_DIM:(g + 1) * HEAD_DIM, h * TQ:(h + 1) * TQ])
    mix_ref[0, :, 0:ATT_WIDTH] = jnp.concatenate(pieces, axis=0).T.astype(BF16)

    u = u_ref[0]
    ext_s[0:POOL_HALO, :] = jnp.where(qb == 0, pst_ref[0], uh_ref[0])
    ext_s[POOL_HALO:POOL_HALO + TQ, :] = u
    pos = qb * TQ + lax.broadcasted_iota(jnp.int32, (TQ, 1), 0)
    for k, o in enumerate(_pool_mix(ext_s, u, pos, wpool_ref, pscale_ref, TQ)):
        mix_ref[0, :, ATT_WIDTH + k * POOL_GROUP_DIM:ATT_WIDTH + (k + 1) * POOL_GROUP_DIM] = o


def _prompt_attention(q, gates, u, pool_prev, kb, vst, vwt, ck, cvt, delta, mimp, eblk, wpool, pscale):
    b, t, _ = q.shape
    assert t % FAR_TILE == 0 and t >= WINDOW + TQ and t // SEL_BLOCK <= LANE
    n_cmp = ck.shape[1]
    cols = N_HEADS * TQ
    halo_per_q = TQ // POOL_HALO
    full = lambda a: pl.BlockSpec(a.shape, lambda i, j: (0,) * a.ndim)
    qblk = lambda c: pl.BlockSpec((1, TQ, c), lambda i, j: (i, j, 0))
    seq = lambda c: pl.BlockSpec((1, t, KV_LANES), lambda i, j: (i, 0, c))
    per_b = lambda a: pl.BlockSpec((1,) + a.shape[1:], lambda i, j: (i,) + (0,) * (a.ndim - 1))
    return pl.pallas_call(
        _prompt_attn_kernel,
        grid=(b, t // TQ),
        in_specs=[qblk(ATT_WIDTH), qblk(LANE), qblk(POOL_WIDTH),
                  pl.BlockSpec((1, POOL_HALO, POOL_WIDTH), lambda i, j: (i, jnp.maximum(j * halo_per_q - 1, 0), 0)),
                  per_b(pool_prev), seq(0), seq(1), per_b(vst), per_b(vwt), per_b(ck), per_b(cvt),
                  full(delta), full(mimp), full(eblk), full(wpool), full(pscale)],
        out_specs=pl.BlockSpec((1, TQ, D_MODEL), lambda i, j: (i, j, 0)),
        out_shape=jax.ShapeDtypeStruct((b, t, D_MODEL), BF16),
        scratch_shapes=[pltpu.VMEM((16 + n_cmp + 8, cols), F32),
                        pltpu.VMEM((cols, 2 * LANE), BF16), pltpu.VMEM((cols, 2 * LANE), BF16),
                        pltpu.VMEM((VT_ROWS, cols), F32), pltpu.VMEM((1, cols), F32),
                        pltpu.VMEM((POOL_HALO + TQ, POOL_WIDTH), F32)],
        compiler_params=pltpu.CompilerParams(dimension_semantics=("parallel", "arbitrary"),
                                             vmem_limit_bytes=VMEM_LIMIT),
    )(q, gates, u, u, pool_prev, kb, kb, vst, vwt, ck, cvt, delta, mimp, eblk, wpool, pscale)


SAMPLE_SEQS = 2
SD_LAST = 0
SD_NEW = LANE
SD_EDGE = 2 * LANE
SD_CMP = 3 * LANE


def _sample_attn_kernel(pt_ref, q_ref, gate_ref, u_ref, kvs_ref, kvw_ref, swin_ref, spool_ref,
                        ccache_ref, scache_ref, nwin_in_ref, wcmp_ref, pe_ref, delta_ref, mimp_ref, expand_ref,
                        wpool_ref, pscale_ref,
                        mix_ref, nwin_ref, npool_ref,
                        cbuf, sbuf, xk_s, xv_s, sem, ext_s, *, layer):
    del nwin_in_ref
    b = pl.program_id(0)
    nb = pl.num_programs(0)
    n_pages = pt_ref.shape[1]
    past = n_pages * PAGE_SIZE
    n_sub = past // CMP_STRIDE
    n_seq, tq = q_ref.shape[0], q_ref.shape[1]
    wb = swin_ref.shape[3]
    n_sel = past // SEL_BLOCK + 1
    pitch = xk_s.shape[1] // CMP_STRIDE
    sub_per_page = PAGE_SIZE // CMP_STRIDE
    slot = lax.rem(b, 2)

    def page_copies(step, slt):
        cps = []
        for s in range(n_seq):
            for p in range(n_pages):
                pg = pt_ref[step * n_seq + s, p]
                dst = (slice(None), pl.ds(p * PAGE_SIZE, PAGE_SIZE))
                cps.append(pltpu.make_async_copy(ccache_ref.at[layer, pg], cbuf.at[slt, s].at[dst], sem.at[0, slt]))
                cps.append(pltpu.make_async_copy(scache_ref.at[layer, pg], sbuf.at[slt, s].at[dst], sem.at[1, slt]))
        return cps

    @pl.when(b == 0)
    def _():
        for cp in page_copies(0, 0):
            cp.start()

    @pl.when(b + 1 < nb)
    def _():
        for cp in page_copies(b + 1, 1 - slot):
            cp.start()

    for cp in page_copies(b, slot):
        cp.wait()

    val = [dict() for _ in range(n_seq)]

    def scores(qbd, k_rows):
        return lax.dot_general(qbd, k_rows, NT_DIMS, preferred_element_type=F32)

    def scores_t(qbd, k_t):
        return jnp.dot(qbd, k_t.astype(BF16), preferred_element_type=F32)

    def pad_rows(x):
        return jnp.concatenate([x, jnp.zeros((LANE - tq, KV_LANES), F32)], axis=0)

    pad_new = lambda x: pad_rows(x).astype(BF16)
    d_last = delta_ref[:, SD_LAST:SD_LAST + LANE]
    d_new = delta_ref[:, SD_NEW:SD_NEW + LANE]

    def softmax_pv(s_old, vt_old, s_new, v_new):
        m = jnp.maximum(s_old.max(axis=1, keepdims=True), s_new.max(axis=1, keepdims=True))
        p_old = jnp.exp2(s_old - m)
        p_new = jnp.exp2(s_new - m)
        l = jnp.sum(p_old, axis=1, keepdims=True) + jnp.sum(p_new, axis=1, keepdims=True)
        o = (lax.dot_general(p_old.astype(BF16), vt_old.astype(BF16), NT_DIMS, preferred_element_type=F32)
             + jnp.dot(p_new.astype(BF16), v_new, preferred_element_type=F32))
        return o, l

    def stage_scores(s):
        v = val[s]
        q = q_ref[s].astype(F32)
        zeros = jnp.zeros((tq, HEAD_DIM), F32)
        blocks = []
        for h in range(N_HEADS):
            halves = [zeros] * N_KV
            halves[h // GROUP] = q[:, h * HEAD_DIM:(h + 1) * HEAD_DIM]
            blocks.append(jnp.concatenate(halves, axis=1))
        qbd = jnp.concatenate(blocks, axis=0).astype(BF16)
        v["qbd"] = qbd
        kvw_new = kvw_ref[s]
        s_first = scores_t(qbd, swin_ref[0, s, 0:KV_LANES, 0:LANE]) + delta_ref[:, SD_EDGE:SD_EDGE + LANE]
        s_mid = scores_t(qbd, swin_ref[0, s, 0:KV_LANES, LANE:wb - LANE])
        s_lastw = scores_t(qbd, swin_ref[0, s, 0:KV_LANES, wb - LANE:wb]) + d_last
        v["s_wold"] = jnp.concatenate([s_first, s_mid, s_lastw], axis=1)
        v["s_wnew"] = scores(qbd, pad_new(kvw_new[:, 0:KV_LANES])) + d_new
        kvs_new = kvs_ref[s]
        s_far = scores_t(qbd, sbuf[slot, s, 0:KV_LANES, 0:past - LANE])
        s_last = scores_t(qbd, sbuf[slot, s, 0:KV_LANES, past - LANE:past]) + d_last
        v["s_past"] = jnp.concatenate([s_far, s_last], axis=1)
        v["s_new"] = scores(qbd, pad_new(kvs_new[:, 0:KV_LANES])) + d_new

    def stage_regroup(s):
        for p in range(n_pages):
            tok = slice(p * PAGE_SIZE, (p + 1) * PAGE_SIZE)
            for x_s, feat in ((xk_s, slice(0, KV_LANES)), (xv_s, slice(KV_LANES, KV_COLS))):
                xt = cbuf[slot, s, feat, tok].T
                for r in range(PAGE_SIZE // 8):
                    l0 = (8 * r) % CMP_STRIDE
                    sub = p * sub_per_page + (8 * r) // CMP_STRIDE
                    x_s[s, pl.ds(l0 * pitch + sub, 8, stride=pitch), :] = xt[8 * r:8 * r + 8, :]

    def stage_window(s):
        v = val[s]
        kvw_new = kvw_ref[s]
        v["o_win"], v["l_win"] = softmax_pv(v["s_wold"], swin_ref[0, s, KV_LANES:KV_COLS, :],
                                            v["s_wnew"], pad_new(kvw_new[:, KV_LANES:KV_COLS]))
        shifted = pltpu.roll(swin_ref[0, s], wb - tq, axis=1)
        new_t = jnp.concatenate([pad_rows(kvw_new[:, 0:KV_LANES]).T, pad_rows(kvw_new[:, KV_LANES:KV_COLS]).T], axis=0)
        new_t = pltpu.roll(new_t, LANE - tq, axis=1)
        lane = lax.broadcasted_iota(jnp.int32, (KV_COLS, LANE), 1)
        nwin_ref[0, s, :, 0:wb - LANE] = shifted[:, 0:wb - LANE]
        nwin_ref[0, s, :, wb - LANE:wb] = jnp.where(lane >= LANE - tq, new_t, shifted[:, wb - LANE:wb])

    def stage_compress(s):
        v = val[s]
        x_of = lambda l, j: (xk_s, xv_s)[j][s, l * pitch:l * pitch + n_sub, :]
        ck, cv = _compress_tokens(x_of, wcmp_ref, pe_ref, n_sub)
        v["cv"] = cv.astype(BF16)
        v["sc"] = scores(v["qbd"], ck.astype(BF16)) + delta_ref[:, SD_CMP:SD_CMP + n_sub]

    def stage_compressed(s):
        v = val[s]
        sc = v["sc"]
        m_c = jnp.max(sc, axis=1, keepdims=True)
        p = jnp.exp2(sc - m_c)
        l_c = jnp.sum(p, axis=1, keepdims=True)
        pn = p * jnp.where(m_c > VALID_MIN, 1.0 / l_c, 0.0)
        v["o_cmp"] = jnp.dot(pn.astype(BF16), v["cv"], preferred_element_type=F32)
        v["imp"] = []
        for g in range(N_KV):
            ps = pn[g * GROUP * tq:(g * GROUP + 1) * tq, :]
            for r in range(1, GROUP):
                ps = ps + pn[(g * GROUP + r) * tq:(g * GROUP + r + 1) * tq, :]
            hi = ps.astype(BF16)
            lo = (ps - hi.astype(F32)).astype(BF16)
            v["imp"].append(jnp.dot(hi, mimp_ref[...], preferred_element_type=F32)
                            + jnp.dot(lo, mimp_ref[...], preferred_element_type=F32))

    jidx = lax.broadcasted_iota(jnp.int32, (tq, LANE), 1)
    cur = (past + lax.broadcasted_iota(jnp.int32, (tq, LANE), 0)) // SEL_BLOCK
    forced = (jidx == 0) | (jidx == cur) | (jidx == cur - 1)

    def stage_select(s):
        v = val[s]
        sel_g = []
        for imp in v["imp"]:
            score = jnp.where(forced, jnp.inf, imp)
            score = jnp.where((jidx > cur) | (jidx >= n_sel), -jnp.inf, score)
            cnt = jnp.zeros((tq, LANE), jnp.int32)
            for jp in range(n_sel):
                col = score[:, jp:jp + 1]
                cnt = cnt + jnp.where(jidx > jp, jnp.where(col >= score, 1, 0), jnp.where(col > score, 1, 0))
            sel = (cnt < min(N_SELECT, n_sel)) & (jidx <= cur) & (jidx < n_sel)
            sel_g.append(jnp.where(sel, 1.0, 0.0))
        sel_rows = jnp.concatenate([sel_g[h // GROUP] for h in range(N_HEADS)], axis=0).astype(BF16)
        v["mask"] = jnp.dot(sel_rows, expand_ref[...], preferred_element_type=F32)

    def stage_selected(s):
        v = val[s]
        s_past = jnp.where(v["mask"] > 0.5, v["s_past"], NEG)
        v["o_sel"], v["l_sel"] = softmax_pv(s_past, sbuf[slot, s, KV_LANES:KV_COLS, :], v["s_new"],
                                            pad_new(kvs_ref[s][:, KV_LANES:KV_COLS]))

    def stage_output(s):
        v = val[s]
        gates = gate_ref[s]
        gate_col = lambda x: jnp.concatenate([gates[:, N_BRANCH * h + x:N_BRANCH * h + x + 1] for h in range(N_HEADS)],
                                             axis=0)
        o = (v["o_cmp"] * gate_col(0) + v["o_sel"] * (gate_col(1) / v["l_sel"])
             + v["o_win"] * (gate_col(2) / v["l_win"]))
        pieces = []
        for h in range(N_HEADS):
            g = h // GROUP
            pieces.append(o[h * tq:(h + 1) * tq, g * HEAD_DIM:(g + 1) * HEAD_DIM])
        mix_ref[s, :, 0:ATT_WIDTH] = jnp.concatenate(pieces, axis=1).astype(BF16)
        u = u_ref[s]
        ext = ext_s.at[s]
        ext[0:1, :] = jnp.zeros((1, POOL_WIDTH), F32)
        ext[1:POOL_HALO, :] = spool_ref[0, s]
        ext[POOL_HALO:POOL_HALO + tq, :] = u
        pos = past + lax.broadcasted_iota(jnp.int32, (tq, 1), 0)
        for k, o_k in enumerate(_pool_mix(ext, u, pos, wpool_ref, pscale_ref, tq)):
            mix_ref[s, :, ATT_WIDTH + k * POOL_GROUP_DIM:ATT_WIDTH + (k + 1) * POOL_GROUP_DIM] = o_k
        npool_ref[s] = ext[POOL_HALO + tq - POOL_STATE:POOL_HALO + tq, :]

    for stage in (stage_scores, stage_regroup, stage_window, stage_compress, stage_compressed, stage_select,
                  stage_selected, stage_output):
        for s in range(n_seq):
            stage(s)


def _sample_attention(layer, page_table, q, gates, u, kvs, kvw, swin_t, state_pool, ccache_t, scache_t, nwin_all,
                      wcmp, pe_rows, delta, mimp, expand, wpool, pscale):
    nb, tq, _ = q.shape
    n_pages = page_table.shape[1]
    past = n_pages * PAGE_SIZE
    n_sub = past // CMP_STRIDE
    wb = swin_t.shape[3]
    assert wb == WINDOW and wb >= 3 * LANE and tq <= CMP_STRIDE and POOL_HALO + tq - POOL_STATE >= 0
    assert past // SEL_BLOCK + 1 <= LANE and past % LANE == 0 and n_sub <= LANE
    ns = SAMPLE_SEQS if nb % SAMPLE_SEQS == 0 else 1
    full = lambda a: pl.BlockSpec(a.shape, lambda i, pt: (0,) * a.ndim)
    per_b = lambda c: pl.BlockSpec((ns, tq, c), lambda i, pt: (i, 0, 0))
    layer_b = lambda a: pl.BlockSpec((1, ns) + a.shape[2:], lambda i, pt: (layer, i) + (0,) * (a.ndim - 2))
    hbm = pl.BlockSpec(memory_space=pl.ANY)
    grid_spec = pltpu.PrefetchScalarGridSpec(
        num_scalar_prefetch=1,
        grid=(nb // ns,),
        in_specs=[per_b(ATT_WIDTH), per_b(LANE), per_b(POOL_WIDTH), per_b(KV_COLS), per_b(KV_COLS),
                  layer_b(swin_t), layer_b(state_pool), hbm, hbm, hbm,
                  full(wcmp), full(pe_rows), full(delta), full(mimp), full(expand), full(wpool), full(pscale)],
        out_specs=[pl.BlockSpec((ns, tq, D_MODEL), lambda i, pt: (i, 0, 0)),
                   layer_b(nwin_all),
                   pl.BlockSpec((ns, POOL_STATE, POOL_WIDTH), lambda i, pt: (i, 0, 0))],
        scratch_shapes=[pltpu.VMEM((2, ns, KV_COLS, past), F32), pltpu.VMEM((2, ns, KV_COLS, past), F32),
                        pltpu.VMEM((ns, CMP_STRIDE * (n_sub + 8), KV_LANES), F32),
                        pltpu.VMEM((ns, CMP_STRIDE * (n_sub + 8), KV_LANES), F32),
                        pltpu.SemaphoreType.DMA((2, 2)), pltpu.VMEM((ns, POOL_HALO + tq, POOL_WIDTH), F32)],
    )
    return pl.pallas_call(
        functools.partial(_sample_attn_kernel, layer=layer),
        grid_spec=grid_spec,
        out_shape=[jax.ShapeDtypeStruct((nb, tq, D_MODEL), BF16),
                   jax.ShapeDtypeStruct(nwin_all.shape, F32),
                   jax.ShapeDtypeStruct((nb, POOL_STATE, POOL_WIDTH), F32)],
        input_output_aliases={10: 1},
        compiler_params=pltpu.CompilerParams(dimension_semantics=("arbitrary",), vmem_limit_bytes=VMEM_LIMIT),
    )(page_table, q, gates, u, kvs, kvw, swin_t, state_pool, ccache_t, scache_t, nwin_all,
      wcmp, pe_rows, delta, mimp, expand, wpool, pscale)


def _ffn_kernel(x_ref, mix_ref, wout_ref, gn_ref, win_ref, wd_ref, gf_ref, o_ref, *, final_norm):
    xm = x_ref[...] + jnp.dot(mix_ref[...], wout_ref[...], preferred_element_type=F32)
    h = _rms(xm, gn_ref[...]).astype(BF16)
    d_ff = wd_ref.shape[0]
    cuts = list(range(0, d_ff, FFN_SLICE)) + [d_ff]
    slices = list(zip(cuts[:-1], cuts[1:]))

    def gate_up(a, b):
        return (jnp.dot(h, win_ref[:, a:b], preferred_element_type=F32),
                jnp.dot(h, win_ref[:, d_ff + a:d_ff + b], preferred_element_type=F32))

    y = xm
    pending = gate_up(*slices[0])
    for j, (a, b) in enumerate(slices):
        gate, up = pending
        if j + 1 < len(slices):
            pending = gate_up(*slices[j + 1])
        act = (gate * _sigmoid(gate) * up).astype(BF16)
        y = y + jnp.dot(act, wd_ref[a:b, :], preferred_element_type=F32)
    o_ref[...] = _rms(y, gf_ref[...]) if final_norm else y


def _ffn(x, mix, wout, gn, wffn_in, wffn_out, gf, final_norm):
    n = x.shape[0]
    tm = min(ROW_TILE, n)
    assert n % tm == 0 and wffn_out.shape[0] % LANE == 0
    once = lambda a: pl.BlockSpec(a.shape, lambda i: (0,) * a.ndim, pipeline_mode=pl.Buffered(1))
    row = lambda cdim: pl.BlockSpec((tm, cdim), lambda i: (i, 0))
    return pl.pallas_call(
        functools.partial(_ffn_kernel, final_norm=final_norm),
        grid=(n // tm,),
        in_specs=[row(D_MODEL), row(D_MODEL), once(wout), once(gn), once(wffn_in), once(wffn_out), once(gf)],
        out_specs=row(D_MODEL),
        out_shape=jax.ShapeDtypeStruct((n, D_MODEL), F32),
        compiler_params=pltpu.CompilerParams(dimension_semantics=("parallel",), vmem_limit_bytes=VMEM_LIMIT),
    )(x, mix, wout, gn, wffn_in, wffn_out, gf)


def _importance_matrix(n_sel, n_cmp_rows, n_cmp):
    spb = SEL_BLOCK // CMP_STRIDE
    j = np.arange(n_sel)[:, None]
    c = np.arange(n_cmp_rows)[None, :]
    return ((c >= spb * j - (CMP_RATIO - 1)) & (c <= spb * j + spb - 1) & (c < n_cmp)).astype(np.float32)


def kernel(x_prompt, x_sample, cache_cmp, cache_sel, state_win, state_pool, page_table, rel_bias, norm_mix, norm_ffn,
           norm_final, w_in, w_out, cmp_pos, w_cmp, w_pool, pool_scale, w_ffn_in, w_ffn_out):
    bp, t, _ = x_prompt.shape
    bs, tq, _ = x_sample.shape
    depth = w_in.shape[0]
    n_pages = page_table.shape[1]
    past = n_pages * PAGE_SIZE
    n_phys = cache_cmp.shape[1]
    wb = state_win.shape[2]
    _check_far_bucket(max(t, past + tq) + WINDOW)

    rb_prompt = jnp.repeat(rel_bias.astype(F32), TQ, axis=1)
    delta_p = _bias_tiles(_prompt_bucket_table(), rb_prompt)
    rb_sample = jnp.pad(jnp.repeat(rel_bias.astype(F32), tq, axis=1), ((0, 0), (0, LANE - N_HEADS * tq)))
    delta_s = _bias_tiles(_sample_bucket_table(past, tq, wb), rb_sample).T[:N_HEADS * tq]

    n_sub_p = t // CMP_STRIDE
    mimp_p = jnp.asarray(_importance_matrix(t // SEL_BLOCK, n_sub_p, n_sub_p - CMP_RATIO + 1), BF16)
    eblk_p = jnp.asarray((np.arange(t)[:, None] // SEL_BLOCK == np.arange(LANE)[None, :]).astype(np.float32), BF16)
    n_sub_s = past // CMP_STRIDE
    mimp_s =jnp.asarray(_importance_matrix(LANE, n_sub_s, n_sub_s - CMP_RATIO + 1).T, BF16)
    expand = jnp.asarray((np.arange(LANE)[:, None] == np.arange(past)[None, :] // SEL_BLOCK).astype(np.float32), BF16)

    pool_zero = jnp.zeros((bp, POOL_HALO, POOL_WIDTH), F32)
    feature_major = lambda a: jnp.moveaxis(a, 2, -1).reshape(a.shape[0], a.shape[1], KV_COLS, a.shape[2])
    token_major = lambda a: jnp.moveaxis(a.reshape(a.shape[:2] + (2, N_KV, HEAD_DIM, a.shape[3])), -1, 2)
    ccache_t = feature_major(cache_cmp)
    scache_t = feature_major(cache_sel)
    swin_t = feature_major(state_win)
    kvct_all = jnp.zeros((depth, bp, KV_COLS, t), F32)
    kvst_all = jnp.zeros((depth, bp, KV_COLS, t), F32)
    nwin_all = jnp.zeros((depth, bs, KV_COLS, wb), F32)

    xp = x_prompt.reshape(bp * t, D_MODEL)
    xs = x_sample.reshape(bs * tq, D_MODEL)
    outs = {k: [] for k in ("p_win", "p_pool", "s_cmp", "s_sel", "s_pool")}
    kv_shape = lambda b, n: (b, n, 2, N_KV, HEAD_DIM)
    off_g = ATT_WIDTH + N_BRANCH * KV_COLS
    for l in range(depth):
        wl = w_in[l]
        w_all = jnp.concatenate([wl[:, :ATT_WIDTH] * (HEAD_DIM ** -0.5 * LOG2E), wl[:, ATT_WIDTH:off_g],
                                 wl[:, off_g + GATE_COLS:], wl[:, off_g:off_g + GATE_COLS],
                                 jnp.zeros((D_MODEL, LANE - GATE_COLS), F32)], axis=1).astype(BF16)
        g_mix = norm_mix[l].reshape(1, D_MODEL)
        g_ffn = norm_ffn[l].reshape(1, D_MODEL)
        g_fin = norm_final.reshape(1, D_MODEL)
        wc = w_cmp[l].reshape(2, CMP_RATIO, CMP_STRIDE, HEAD_DIM, HEAD_DIM)
        zero = jnp.zeros_like(wc)
        w_bd = jnp.concatenate([jnp.concatenate([wc, zero], axis=-1), jnp.concatenate([zero, wc], axis=-1)],
                               axis=-2).astype(BF16)
        pe = cmp_pos[l].reshape(CMP_RATIO, CMP_STRIDE, 2, HEAD_DIM).transpose(2, 0, 1, 3)
        pe_rows = jnp.tile(pe.reshape(2 * CMP_RATIO * CMP_STRIDE, HEAD_DIM), (1, N_KV)).astype(F32)
        wpool = w_pool[l].astype(BF16)
        pscale = pool_scale[l].reshape(1, POOL_WIDTH)
        wout = w_out[l].astype(BF16)
        wffn_in = w_ffn_in[l].astype(BF16)
        wffn_out = w_ffn_out[l].astype(BF16)
        last = l == depth - 1

        q, u, gates, kvc, kb, vst, vwt, kvct_all, kvst_all, kvwt = _inproj_seq(l, xp, g_mix, w_all, kvct_all, kvst_all)
        ck, cvt = _compress_prompt(kvc.reshape(bp, t, KV_COLS), w_bd, pe_rows)
        mix = _prompt_attention(q.reshape(bp, t, ATT_WIDTH), gates.reshape(bp, t, LANE), u.reshape(bp, t, POOL_WIDTH),
                                pool_zero, kb.reshape(bp, t, KV_COLS), vst, vwt, ck, cvt, delta_p, mimp_p, eblk_p,
                                wpool, pscale)
        xp = _ffn(xp, mix.reshape(bp * t, D_MODEL), wout, g_ffn, wffn_in, wffn_out, g_fin, last)
        outs["p_win"].append(kvwt[:, :, t - wb:])
        outs["p_pool"].append(u.reshape(bp, t, POOL_WIDTH)[:, t - POOL_STATE:])

        q, u, gates, kvc, kvs, kvw = _inproj_rows(xs, g_mix, w_all)
        mix, nwin_all, npool = _sample_attention(
            l, page_table, q.reshape(bs, tq, ATT_WIDTH), gates.reshape(bs, tq, LANE), u.reshape(bs, tq, POOL_WIDTH),
            kvs.reshape(bs, tq, KV_COLS), kvw.reshape(bs, tq, KV_COLS), swin_t, state_pool, ccache_t, scache_t,
            nwin_all, w_bd, pe_rows, delta_s, mimp_s, expand, wpool, pscale)
        xs = _ffn(xs, mix.reshape(bs * tq, D_MODEL), wout, g_ffn, wffn_in, wffn_out, g_fin, last)
        outs["s_cmp"].append(kvc.reshape(kv_shape(bs, tq)))
        outs["s_sel"].append(kvs.reshape(kv_shape(bs, tq)))
        outs["s_pool"].append(npool)

    return (xp.reshape(bp, t, D_MODEL), xs.reshape(bs, tq, D_MODEL),
            token_major(kvct_all), token_major(kvst_all), token_major(jnp.stack(outs["p_win"])),
            jnp.stack(outs["p_pool"]), jnp.stack(outs["s_cmp"]), jnp.stack(outs["s_sel"]), token_major(nwin_all),
            jnp.stack(outs["s_pool"]))
```

```python
import functools
import math

import numpy as np
import jax
import jax.numpy as jnp
from jax import lax
from jax.experimental import pallas as pl
from jax.experimental.pallas import tpu as pltpu

D_MODEL = 1024
HEAD_DIM = 64
N_HEADS = 8
N_KV = 2
GROUP = N_HEADS // N_KV
ATT_WIDTH = N_HEADS * HEAD_DIM
POOL_WIDTH = D_MODEL - ATT_WIDTH
KV_COLS = 2 * N_KV * HEAD_DIM
KV_LANES = N_KV * HEAD_DIM
VT_ROWS = KV_LANES + 16
CMP_LEN = 32
CMP_STRIDE = 16
CMP_RATIO = CMP_LEN // CMP_STRIDE
SEL_BLOCK = 64
N_SELECT = 16
WINDOW = 512
N_BRANCH = 3
GATE_COLS = N_HEADS * N_BRANCH
POOL_WINDOWS = (2, 4, 8, 16)
POOL_GROUP_DIM = POOL_WIDTH // len(POOL_WINDOWS)
POOL_STATE = max(POOL_WINDOWS) - 1
POOL_HALO = 16
N_BUCKETS = 32
MAX_DISTANCE = 128
PAGE_SIZE = 128
EPS = 1e-6
NEG = -1e30
VALID_MIN = -1e29

LANE = 128
TQ = 128
PROMPT_QBLOCKS = 2
FAR_TILE = 512
CHAIN_COLS = 256
LOG2E = 1.4426950408889634
ROW_TILE = 512
FFN_SLICE = 512
VMEM_LIMIT = 52 * 1024 * 1024

F32 = jnp.float32
BF16 = jnp.bfloat16
NT_DIMS = (((1,), (1,)), ((), ()))


def _bucket_np(dist):
    n = np.maximum(dist, 0)
    max_exact = N_BUCKETS // 2
    nf = np.maximum(n, 1).astype(np.float32)
    large = max_exact + (np.log(nf / max_exact) / math.log(MAX_DISTANCE / max_exact)
                         * (N_BUCKETS - max_exact)).astype(np.int32)
    return np.where(n < max_exact, n, np.minimum(large, N_BUCKETS - 1)).astype(np.int32)


def _bucket_or_masked(dist, valid):
    return np.where(valid, _bucket_np(dist), -1).astype(np.int32)


FAR_DIST = TQ + 1


def _check_far_bucket(max_dist):
    assert (_bucket_np(np.arange(FAR_DIST, max_dist + 1)) == N_BUCKETS - 1).all()


def _bias_tile_kernel(bucket_ref, rb_ref, out_ref):
    bucket = bucket_ref[...]
    acc = jnp.zeros(bucket.shape, F32)
    for b in range(N_BUCKETS):
        acc = acc + jnp.where(bucket == b, rb_ref[b:b + 1, :], 0.0)
    acc = (acc - rb_ref[N_BUCKETS - 1:N_BUCKETS, :]) * LOG2E
    out_ref[...] = jnp.where(bucket < 0, NEG, acc)


def _bias_tiles(bucket_np, rb_cols):
    rows, cols = bucket_np.shape
    return pl.pallas_call(
        _bias_tile_kernel,
        out_shape=jax.ShapeDtypeStruct((rows, cols), F32),
    )(jnp.asarray(bucket_np), rb_cols)


DD0 = 0
DC0 = 2 * TQ
DC_ROWS = 24
DW0 = DC0 + 32
PROMPT_DELTA_ROWS = DW0 + TQ


def _prompt_bucket_table():
    i = np.arange(TQ)[None, :]
    kk = np.arange(2 * TQ)[:, None]
    d = i + TQ - kk
    dd = _bucket_or_masked(d, d >= 0)
    cc = np.arange(32)[:, None] - 16
    d = i - CMP_STRIDE * cc - (CMP_LEN - 1)
    dc = _bucket_or_masked(d, (d >= 0) & (cc < 8))
    j = np.arange(TQ)[:, None]
    d = i + WINDOW - j
    dw = _bucket_or_masked(d, (d >= 0) & (d < WINDOW))
    tab = np.concatenate([dd, dc, dw], axis=0)
    return np.tile(tab, (1, N_HEADS))


def _sample_bucket_table(past, tq, wb):
    col = np.arange(LANE)
    t = (col % tq)[None, :]
    colok = (col < N_HEADS * tq)[None, :]
    kk = np.arange(LANE)[:, None]
    d = LANE + t - kk
    dlast = _bucket_or_masked(d, colok & (d >= 0))
    d = t - kk
    dnew = _bucket_or_masked(d, colok & (d >= 0) & (kk < tq))
    d = wb + t - kk
    dedge = _bucket_or_masked(d, colok & (d >= 0) & (d < WINDOW))
    nsub = past // CMP_STRIDE
    c = np.arange(nsub)[:, None]
    d = past + t - CMP_STRIDE * c - (CMP_LEN - 1)
    dcmp = _bucket_or_masked(d, colok & (d >= 0) & (c < nsub - CMP_RATIO + 1))
    return np.concatenate([dlast, dnew, dedge, dcmp], axis=0)


IN_Q = 0
IN_KV = ATT_WIDTH
IN_U = IN_KV + N_BRANCH * KV_COLS
IN_G = IN_U + POOL_WIDTH
IN_COLS_PAD = IN_G + LANE


def _rms(x, g):
    return x * lax.rsqrt(jnp.mean(x * x, axis=-1, keepdims=True) + EPS) * g


def _sigmoid(x):
    return 1.0 / (1.0 + jnp.exp(-x))


def _inproj_common(x_ref, g_ref, w_ref, q_ref, u_ref, gate_ref):
    h = _rms(x_ref[...], g_ref[...]).astype(BF16)
    z = jnp.dot(h, w_ref[...], preferred_element_type=F32)
    q_ref[...] = z[:, IN_Q:IN_KV].astype(BF16)
    u_ref[...] = z[:, IN_U:IN_G]
    gate_ref[...] = _sigmoid(z[:, IN_G:IN_COLS_PAD])
    return [z[:, IN_KV + i * KV_COLS:IN_KV + (i + 1) * KV_COLS] for i in range(N_BRANCH)]


def _inproj_rows_kernel(x_ref, g_ref, w_ref, q_ref, u_ref, gate_ref, kvc_ref, kvs_ref, kvw_ref):
    kvc, kvs, kvw = _inproj_common(x_ref, g_ref, w_ref, q_ref, u_ref, gate_ref)
    kvc_ref[...] = kvc
    kvs_ref[...] = kvs
    kvw_ref[...] = kvw


def _inproj_seq_kernel(x_ref, g_ref, w_ref, cbuf_ref, sbuf_ref, q_ref, u_ref, gate_ref, kvc_ref, kb_ref,
                       vst_ref, vwt_ref, kvct_ref, kvst_ref, kvwt_ref):
    del cbuf_ref, sbuf_ref
    kvc, kvs, kvw = _inproj_common(x_ref, g_ref, w_ref, q_ref, u_ref, gate_ref)
    kvc_ref[...] = kvc
    kb_ref[...] = jnp.concatenate([kvs[:, 0:KV_LANES], kvw[:, 0:KV_LANES]], axis=1).astype(BF16)
    kvst = kvs.T
    kvwt = kvw.T
    kvct_ref[0, 0] = kvc.T
    kvst_ref[0, 0] = kvst
    kvwt_ref[0] = kvwt
    ones = jnp.ones((VT_ROWS - KV_LANES, kvst.shape[1]), F32)
    vst_ref[0] = jnp.concatenate([kvst[KV_LANES:KV_COLS], ones], axis=0).astype(BF16)
    vwt_ref[0] = jnp.concatenate([kvwt[KV_LANES:KV_COLS], ones], axis=0).astype(BF16)


def _inproj_rows(x, g, w):
    n = x.shape[0]
    tm = min(ROW_TILE, n)
    row = lambda c: pl.BlockSpec((tm, c), lambda i: (i, 0))
    full = lambda a: pl.BlockSpec(a.shape, lambda i: (0,) * a.ndim)
    outs = [(ATT_WIDTH, BF16), (POOL_WIDTH, F32), (LANE, F32), (KV_COLS, F32), (KV_COLS, F32), (KV_COLS, F32)]
    return pl.pallas_call(
        _inproj_rows_kernel,
        grid=(n // tm,),
        in_specs=[row(D_MODEL), full(g), full(w)],
        out_specs=[row(c) for c, _ in outs],
        out_shape=[jax.ShapeDtypeStruct((n, c), dt) for c, dt in outs],
        compiler_params=pltpu.CompilerParams(dimension_semantics=("parallel",), vmem_limit_bytes=VMEM_LIMIT),
    )(x, g, w)


def _inproj_seq(layer, x, g, w, kvct_all, kvst_all):
    n = x.shape[0]
    _, b, _, t = kvct_all.shape
    tm = min(ROW_TILE, t)
    tiles = t // tm
    row = lambda c: pl.BlockSpec((tm, c), lambda i: (i, 0))
    full = lambda a: pl.BlockSpec(a.shape, lambda i: (0,) * a.ndim)
    hbm = pl.BlockSpec(memory_space=pl.ANY)
    seq_t = lambda r: pl.BlockSpec((1, r, tm), lambda i: (i // tiles, 0, i % tiles))
    all_t = pl.BlockSpec((1, 1, KV_COLS, tm), lambda i: (layer, i // tiles, 0, i % tiles))
    rows = [(ATT_WIDTH, BF16), (POOL_WIDTH, F32), (LANE, F32), (KV_COLS, F32), (KV_COLS, BF16)]
    return pl.pallas_call(
        _inproj_seq_kernel,
        grid=(n // tm,),
        in_specs=[row(D_MODEL), full(g), full(w), hbm, hbm],
        out_specs=[row(c) for c, _ in rows] + [seq_t(VT_ROWS), seq_t(VT_ROWS), all_t, all_t, seq_t(KV_COLS)],
        out_shape=[jax.ShapeDtypeStruct((n, c), dt) for c, dt in rows]
        + [jax.ShapeDtypeStruct((b, VT_ROWS, t), BF16), jax.ShapeDtypeStruct((b, VT_ROWS, t), BF16),
           jax.ShapeDtypeStruct(kvct_all.shape, F32), jax.ShapeDtypeStruct(kvst_all.shape, F32),
           jax.ShapeDtypeStruct((b, KV_COLS, t), F32)],
        input_output_aliases={3: 7, 4: 8},
        compiler_params=pltpu.CompilerParams(dimension_semantics=("parallel",), vmem_limit_bytes=VMEM_LIMIT),
    )(x, g, w, kvct_all, kvst_all)


def _compress_tokens(x_of, w_ref, pe_ref, n_sub):
    outs = []
    for j in range(2):
        parts = [jnp.zeros((n_sub, KV_LANES), F32) for _ in range(CMP_RATIO)]
        for l in range(CMP_STRIDE):
            x = x_of(l, j)
            for r in range(CMP_RATIO):
                row = (j * CMP_RATIO + r) * CMP_STRIDE + l
                a = (x + pe_ref[row:row + 1, :]).astype(BF16)
                parts[r] = parts[r] + jnp.dot(a, w_ref[j, r, l], preferred_element_type=F32)
        comp = parts[0]
        for r in range(1, CMP_RATIO):
            comp = comp + pltpu.roll(parts[r], n_sub - r, axis=0)
        outs.append(comp)
    return outs


def _compress_kernel(x_ref, w_ref, pe_ref, ck_ref, cvt_ref):
    n_sub = x_ref.shape[1]
    x_of = lambda l, j: x_ref[0, :, l * KV_COLS + j * KV_LANES:l * KV_COLS + (j + 1) * KV_LANES]
    ck, cv = _compress_tokens(x_of, w_ref, pe_ref, n_sub)
    ck_ref[0] = ck.astype(BF16)
    cvt_ref[0] = cv.T.astype(BF16)


def _compress_prompt(kvc, w_bd, pe_rows):
    b, t, _ = kvc.shape
    n_sub = t // CMP_STRIDE
    x = kvc.reshape(b, n_sub, CMP_STRIDE * KV_COLS)
    full = lambda a: pl.BlockSpec(a.shape, lambda i: (0,) * a.ndim)
    return pl.pallas_call(
        _compress_kernel,
        grid=(b,),
        in_specs=[pl.BlockSpec((1, n_sub, CMP_STRIDE * KV_COLS), lambda i: (i, 0, 0)), full(w_bd), full(pe_rows)],
        out_specs=[pl.BlockSpec((1, n_sub, KV_LANES), lambda i: (i, 0, 0)),
                   pl.BlockSpec((1, KV_LANES, n_sub), lambda i: (i, 0, 0))],
        out_shape=[jax.ShapeDtypeStruct((b, n_sub, KV_LANES), BF16), jax.ShapeDtypeStruct((b, KV_LANES, n_sub), BF16)],
        compiler_params=pltpu.CompilerParams(dimension_semantics=("parallel",), vmem_limit_bytes=VMEM_LIMIT),
    )(x, w_bd, pe_rows)


def _split_dot(m, p):
    hi = p.astype(BF16)
    lo = (p - hi.astype(F32)).astype(BF16)
    return jnp.dot(m, hi, preferred_element_type=F32) + jnp.dot(m, lo, preferred_element_type=F32)


def _col_max(s):
    rows = s.shape[0]
    parts = [s[r:r + 64] for r in range(0, rows, 64)] if rows % 64 == 0 and rows > 64 else [s]
    while len(parts) > 1:
        parts = [jnp.maximum(a, b) for a, b in zip(parts[0::2], parts[1::2])] + ([parts[-1]] if len(parts) % 2 else [])
    return jnp.max(parts[0], axis=0, keepdims=True)


def _pool_mix(ext_ref, u, pos, wpool_ref, pscale_ref, rows):
    outs = []
    for k, w in enumerate(POOL_WINDOWS):
        ln = slice(k * POOL_GROUP_DIM, (k + 1) * POOL_GROUP_DIM)
        ws = ext_ref[POOL_HALO:POOL_HALO + rows, ln]
        for s in range(1, w):
            ws = ws + ext_ref[POOL_HALO - s:POOL_HALO - s + rows, ln]
        cnt = jnp.minimum(pos + 1, w).astype(F32)
        pooled = (ws / cnt - u[:, ln]).astype(BF16)
        o = jnp.dot(pooled, wpool_ref[k], preferred_element_type=F32) * pscale_ref[:, ln]
        outs.append(o.astype(BF16))
    return outs


def _prompt_attn_kernel(q_ref, gate_ref, u_ref, uh_ref, pst_ref, ksel_ref, kwin_ref, vst_ref, vwt_ref,
                        ck_ref, cvt_ref, delta_ref, mimp_ref, eblk_ref, wpool_ref, pscale_ref,
                        mix_ref,
                        sc_s, qa_s, qf_s, acc_s, m_s, ext_s):
    step = pl.program_id(1)
    n_blk = q_ref.shape[1] // TQ
    t_len = ksel_ref.shape[1]
    n_cmp = ck_ref.shape[1]
    n_sel = t_len // SEL_BLOCK
    cols = N_HEADS * TQ
    blk_per_q = TQ // SEL_BLOCK
    n_chain = cols // CHAIN_COLS
    chains = [slice(c * CHAIN_COLS, (c + 1) * CHAIN_COLS) for c in range(n_chain)]
    n_wc = WINDOW // TQ + 1
    w_delta = {0: DW0, n_wc - 2: DD0, n_wc - 1: DD0 + TQ}

    val = [dict() for _ in range(n_blk)]
    qb_of = lambda a: n_blk * step + a
    tok = lambda a: slice(a * TQ, (a + 1) * TQ)

    def stage_compressed(a):
        v, qb = val[a], qb_of(a)
        q = q_ref[0, tok(a), :].astype(F32)
        zeros = jnp.zeros((TQ, HEAD_DIM), F32)
        blocks = []
        for h in range(N_HEADS):
            halves = [zeros] * N_KV
            halves[h // GROUP] = q[:, h * HEAD_DIM:(h + 1) * HEAD_DIM]
            blocks.append(jnp.concatenate(halves, axis=1))
        qbd = jnp.concatenate(blocks, axis=0).astype(BF16)
        v["qbd"] = qbd
        sc = lax.dot_general(ck_ref[0], qbd, NT_DIMS, preferred_element_type=F32)
        crow = lax.broadcasted_iota(jnp.int32, (n_cmp, cols), 0)
        sc = jnp.where(crow >= 8 * qb + 8, NEG, sc)
        sc_s[a, 0:16, :] = jnp.zeros((16, cols), F32)
        sc_s[a, 16 + n_cmp:16 + n_cmp + 8, :] = jnp.zeros((8, cols), F32)
        sc_s[a, 16:16 + n_cmp, :] = sc
        near = pl.ds(pl.multiple_of(8 * qb, 8), DC_ROWS)
        sc_s[a, near, :] = sc_s[a, near, :] + delta_ref[DC0:DC0 + DC_ROWS, :]
        sc = sc_s[a, 16:16 + n_cmp, :]
        m_c = jnp.max(sc, axis=0, keepdims=True)
        p = jnp.exp2(sc - m_c)
        l_c = jnp.sum(p, axis=0, keepdims=True)
        pn = p * jnp.where(m_c > VALID_MIN, 1.0 / l_c, 0.0)
        v["o_cmp"] = jnp.dot(cvt_ref[0], pn.astype(BF16), preferred_element_type=F32)
        v["imp"] = []
        for g in range(N_KV):
            ps = pn[:, (g * GROUP) * TQ:(g * GROUP + 1) * TQ]
            for r in range(1, GROUP):
                ps = ps + pn[:, (g * GROUP + r) * TQ:(g * GROUP + r + 1) * TQ]
            v["imp"].append(_split_dot(mimp_ref[...], ps))

    def stage_window_scores(a):
        v, qb = val[a], qb_of(a)
        w_pos0 = [(qb - (n_wc - 1) + c) * TQ for c in range(n_wc)]
        w_k0 = [pl.multiple_of(jnp.maximum(p0, 0), TQ) for p0 in w_pos0]
        w_keys = [kwin_ref[0, pl.ds(k0, TQ), :] for k0 in w_k0]
        v["w_vt"] = [vwt_ref[0, :, pl.ds(k0, TQ)] for k0 in w_k0]
        v["s_win"] = []
        for cs in chains:
            s_w = []
            for c in range(n_wc):
                s = lax.dot_general(w_keys[c], v["qbd"][cs], NT_DIMS, preferred_element_type=F32)
                if c in w_delta:
                    s = s + delta_ref[w_delta[c]:w_delta[c] + TQ, cs]
                if c < n_wc - 1:
                    s = s + jnp.where(w_pos0[c] < 0, NEG, 0.0)
                s_w.append(s)
            v["s_win"].append(s_w)

    jidx = lax.broadcasted_iota(jnp.int32, (n_sel, TQ), 0)
    qi = lax.broadcasted_iota(jnp.int32, (n_sel, TQ), 1)
    pad = jnp.zeros((LANE - n_sel, TQ), F32)

    def stage_select(a):
        v, qb = val[a], qb_of(a)
        cur = blk_per_q * qb + qi // SEL_BLOCK
        forced = (jidx == 0) | (jidx == cur) | (jidx == cur - 1)
        prev_blk = blk_per_q * jnp.maximum(qb - 1, 0)
        ns_t = []
        for imp in v["imp"]:
            score = jnp.where(forced, jnp.inf, imp)
            score = jnp.where(jidx > cur, -jnp.inf, score)
            cnt = jnp.zeros((n_sel, TQ), jnp.int32)
            for jp in range(n_sel):
                row = score[jp:jp + 1, :]
                cnt = cnt + jnp.where(jidx > jp, jnp.where(row >= score, 1, 0), jnp.where(row > score, 1, 0))
            sel = (cnt < min(N_SELECT, n_sel)) & (jidx <= cur)
            ns_all = jnp.where(sel, 0.0, NEG)
            ns_far = jnp.where(jidx < prev_blk, ns_all, NEG)
            ns_t.append([jnp.concatenate([x, pad], axis=0).T.astype(BF16) for x in (ns_all, ns_far)])
        for h in range(N_HEADS):
            rows_h = slice(h * TQ, (h + 1) * TQ)
            qa_s[a, rows_h, 0:LANE] = v["qbd"][rows_h]
            qf_s[a, rows_h, 0:LANE] = v["qbd"][rows_h]
            qa_s[a, rows_h, LANE:2 * LANE] = ns_t[h // GROUP][0]
            qf_s[a, rows_h, LANE:2 * LANE] = ns_t[h // GROUP][1]
        m_s[a] = jnp.full((1, cols), NEG, F32)
        acc_s[a] = jnp.zeros((VT_ROWS, cols), F32)

    def online_update(a, cs, s, vt):
        m_old = m_s[a, :, cs]
        m_new = jnp.maximum(m_old, _col_max(s))
        alpha = jnp.exp2(m_old - m_new)
        p = jnp.exp2(s - m_new)
        acc_s[a, :, cs] = alpha * acc_s[a, :, cs] + jnp.dot(vt, p.astype(BF16), preferred_element_type=F32)
        m_s[a, :, cs] = m_new

    def keys_with_block(k0, n):
        return jnp.concatenate([ksel_ref[0, pl.ds(k0, n), :], eblk_ref[pl.ds(k0, n), :]], axis=1)

    def masked_scores(kcat, q_s, a, cs):
        return lax.dot_general(kcat, q_s[a, cs, :], NT_DIMS, preferred_element_type=F32)

    n_far = (TQ * jnp.maximum(qb_of(n_blk - 1) - 1, 0) + FAR_TILE - 1) // FAR_TILE

    def far_tiles(t0, n):
        k0 = [pl.multiple_of((t0 + i) * FAR_TILE, FAR_TILE) for i in range(n)]
        s_tiles = []
        for i in range(n):
            kcat = keys_with_block(k0[i], FAR_TILE)
            s_tiles.append([[masked_scores(kcat, qf_s, a, cs) for cs in chains] for a in range(n_blk)])
        for i in range(n):
            vt = vst_ref[0, :, pl.ds(k0[i], FAR_TILE)]
            for a in range(n_blk):
                for cs, s in zip(chains, s_tiles[i][a]):
                    online_update(a, cs, s, vt)

    def stage_near_scores(a):
        v, qb = val[a], qb_of(a)
        k_prev = pl.multiple_of(TQ * jnp.maximum(qb - 1, 0), TQ)
        k_diag = pl.multiple_of(qb * TQ, TQ)
        no_prev = jnp.where(qb == 0, NEG, 0.0)
        kcat_prev = keys_with_block(k_prev, TQ)
        kcat_diag = keys_with_block(k_diag, TQ)
        v["vt_near"] = jnp.concatenate([vst_ref[0, :, pl.ds(k_prev, TQ)], vst_ref[0, :, pl.ds(k_diag, TQ)]], axis=1)
        v["s_near"] = []
        for cs in chains:
            s_prev = masked_scores(kcat_prev, qa_s, a, cs) + delta_ref[DD0:DD0 + TQ, cs] + no_prev
            s_diag = masked_scores(kcat_diag, qa_s, a, cs) + delta_ref[DD0 + TQ:DD0 + 2 * TQ, cs]
            v["s_near"].append(jnp.concatenate([s_prev, s_diag], axis=0))

    def stage_near_update(a):
        v = val[a]
        for cs, s in zip(chains, v["s_near"]):
            online_update(a, cs, s, v["vt_near"])

    def stage_window(a):
        v = val[a]
        o_w = []
        for s_w in v["s_win"]:
            m_w = s_w[0].max(axis=0, keepdims=True)
            for s in s_w[1:]:
                m_w = jnp.maximum(m_w, s.max(axis=0, keepdims=True))
            o_c = jnp.zeros((VT_ROWS, CHAIN_COLS), F32)
            for s, vt in zip(s_w, v["w_vt"]):
                o_c = o_c + jnp.dot(vt, jnp.exp2(s - m_w).astype(BF16), preferred_element_type=F32)
            o_w.append(o_c)
        v["o_win"] = jnp.concatenate(o_w, axis=1)

    def stage_output(a):
        v, qb = val[a], qb_of(a)
        l_sel = acc_s[a, KV_LANES:KV_LANES + 1, :]
        o_sel = acc_s[a, 0:KV_LANES, :]
        l_win = v["o_win"][KV_LANES:KV_LANES + 1, :]
        o_win = v["o_win"][0:KV_LANES, :]
        gt = gate_ref[0, tok(a), :].T
        gate_row = lambda x: jnp.concatenate([gt[N_BRANCH * h + x:N_BRANCH * h + x + 1, :] for h in range(N_HEADS)],
                                             axis=1)
        o_t = v["o_cmp"] * gate_row(0) + o_sel * (gate_row(1) / l_sel) + o_win * (gate_row(2) / l_win)
        pieces = []
        for h in range(N_HEADS):
            g = h // GROUP
            pieces.append(o_t[g * HEAD_DIM:(g + 1) * HEAD_DIM, h * TQ:(h + 1) * TQ])
        mix_ref[0, tok(a), 0:ATT_WIDTH] = jnp.concatenate(pieces, axis=0).T.astype(BF16)
        u = u_ref[0, tok(a), :]
        ext = ext_s.at[a]
        if a == 0:
            ext[0:POOL_HALO, :] = jnp.where(qb == 0, pst_ref[0], uh_ref[0])
        else:
            ext[0:POOL_HALO, :] = u_ref[0, a * TQ - POOL_HALO:a * TQ, :]
        ext[POOL_HALO:POOL_HALO + TQ, :] = u
        pos = qb * TQ + lax.broadcasted_iota(jnp.int32, (TQ, 1), 0)
        for k, o in enumerate(_pool_mix(ext, u, pos, wpool_ref, pscale_ref, TQ)):
            mix_ref[0, tok(a), ATT_WIDTH + k * POOL_GROUP_DIM:ATT_WIDTH + (k + 1) * POOL_GROUP_DIM] = o

    def run(stage):
        for a in range(n_blk):
            stage(a)

    run(stage_compressed)
    run(stage_window_scores)
    run(stage_select)

    odd = lax.rem(n_far, 2)

    @pl.when(odd == 1)
    def _():
        far_tiles(0, 1)

    def pair_body(i, carry):
        far_tiles(odd + 2 * i, 2)
        return carry

    lax.fori_loop(0, lax.div(n_far, 2), pair_body, 0)

    run(stage_near_scores)
    run(stage_near_update)
    run(stage_window)
    run(stage_output)


def _prompt_attention(q, gates, u, pool_prev, kb, vst, vwt, ck, cvt, delta, mimp, eblk, wpool, pscale):
    b, t, _ = q.shape
    assert t % FAR_TILE == 0 and t >= WINDOW + TQ and t // SEL_BLOCK <= LANE
    n_cmp = ck.shape[1]
    cols = N_HEADS * TQ
    nq = PROMPT_QBLOCKS if (t // TQ) % PROMPT_QBLOCKS == 0 else 1
    rows = nq * TQ
    halo_per_step = rows // POOL_HALO
    full = lambda a: pl.BlockSpec(a.shape, lambda i, j: (0,) * a.ndim)
    qblk = lambda c: pl.BlockSpec((1, rows, c), lambda i, j: (i, j, 0))
    seq = lambda c: pl.BlockSpec((1, t, KV_LANES), lambda i, j: (i, 0, c))
    per_b = lambda a: pl.BlockSpec((1,) + a.shape[1:], lambda i, j: (i,) + (0,) * (a.ndim - 1))
    return pl.pallas_call(
        _prompt_attn_kernel,
        grid=(b, t // rows),
        in_specs=[qblk(ATT_WIDTH), qblk(LANE), qblk(POOL_WIDTH),
                  pl.BlockSpec((1, POOL_HALO, POOL_WIDTH), lambda i, j: (i, jnp.maximum(j * halo_per_step - 1, 0), 0)),
                  per_b(pool_prev), seq(0), seq(1), per_b(vst), per_b(vwt), per_b(ck), per_b(cvt),
                  full(delta), full(mimp), full(eblk), full(wpool), full(pscale)],
        out_specs=pl.BlockSpec((1, rows, D_MODEL), lambda i, j: (i, j, 0)),
        out_shape=jax.ShapeDtypeStruct((b, t, D_MODEL), BF16),
        scratch_shapes=[pltpu.VMEM((nq, 16 + n_cmp + 8, cols), F32),
                        pltpu.VMEM((nq, cols, 2 * LANE), BF16), pltpu.VMEM((nq, cols, 2 * LANE), BF16),
                        pltpu.VMEM((nq, VT_ROWS, cols), F32), pltpu.VMEM((nq, 1, cols), F32),
                        pltpu.VMEM((nq, POOL_HALO + TQ, POOL_WIDTH), F32)],
        compiler_params=pltpu.CompilerParams(dimension_semantics=("parallel", "arbitrary"),
                                             vmem_limit_bytes=VMEM_LIMIT),
    )(q, gates, u, u, pool_prev, kb, kb, vst, vwt, ck, cvt, delta, mimp, eblk, wpool, pscale)


SAMPLE_SEQS = 2
SD_LAST = 0
SD_NEW = LANE
SD_EDGE = 2 * LANE
SD_CMP = 3 * LANE


def _sample_attn_kernel(pt_ref, q_ref, gate_ref, u_ref, kvs_ref, kvw_ref, swin_ref, spool_ref,
                        ccache_ref, scache_ref, nwin_in_ref, wcmp_ref, pe_ref, delta_ref, mimp_ref, expand_ref,
                        wpool_ref, pscale_ref,
                        mix_ref, nwin_ref, npool_ref,
                        cbuf, sbuf, xk_s, xv_s, sem, ext_s, *, layer):
    del nwin_in_ref
    b = pl.program_id(0)
    nb = pl.num_programs(0)
    n_pages = pt_ref.shape[1]
    past = n_pages * PAGE_SIZE
    n_sub = past // CMP_STRIDE
    n_seq, tq = q_ref.shape[0], q_ref.shape[1]
    wb = swin_ref.shape[3]
    n_sel = past // SEL_BLOCK + 1
    pitch = xk_s.shape[1] // CMP_STRIDE
    sub_per_page = PAGE_SIZE // CMP_STRIDE
    slot = lax.rem(b, 2)

    def page_copies(step, slt):
        cps = []
        for s in range(n_seq):
            for p in range(n_pages):
                pg = pt_ref[step * n_seq + s, p]
                dst = (slice(None), pl.ds(p * PAGE_SIZE, PAGE_SIZE))
                cps.append(pltpu.make_async_copy(ccache_ref.at[layer, pg], cbuf.at[slt, s].at[dst], sem.at[0, slt]))
                cps.append(pltpu.make_async_copy(scache_ref.at[layer, pg], sbuf.at[slt, s].at[dst], sem.at[1, slt]))
        return cps

    @pl.when(b == 0)
    def _():
        for cp in page_copies(0, 0):
            cp.start()

    @pl.when(b + 1 < nb)
    def _():
        for cp in page_copies(b + 1, 1 - slot):
            cp.start()

    for cp in page_copies(b, slot):
        cp.wait()

    val = [dict() for _ in range(n_seq)]

    def scores(qbd, k_rows):
        return lax.dot_general(qbd, k_rows, NT_DIMS, preferred_element_type=F32)

    def scores_t(qbd, k_t):
        return jnp.dot(qbd, k_t.astype(BF16), preferred_element_type=F32)

    def pad_rows(x):
        return jnp.concatenate([x, jnp.zeros((LANE - tq, KV_LANES), F32)], axis=0)

    pad_new = lambda x: pad_rows(x).astype(BF16)
    d_last = delta_ref[:, SD_LAST:SD_LAST + LANE]
    d_new = delta_ref[:, SD_NEW:SD_NEW + LANE]

    def softmax_pv(s_old, vt_old, s_new, v_new):
        m = jnp.maximum(s_old.max(axis=1, keepdims=True), s_new.max(axis=1, keepdims=True))
        p_old = jnp.exp2(s_old - m)
        p_new = jnp.exp2(s_new - m)
        l = jnp.sum(p_old, axis=1, keepdims=True) + jnp.sum(p_new, axis=1, keepdims=True)
        o = (lax.dot_general(p_old.astype(BF16), vt_old.astype(BF16), NT_DIMS, preferred_element_type=F32)
             + jnp.dot(p_new.astype(BF16), v_new, preferred_element_type=F32))
        return o, l

    def stage_scores(s):
        v = val[s]
        q = q_ref[s].astype(F32)
        zeros = jnp.zeros((tq, HEAD_DIM), F32)
        blocks = []
        for h in range(N_HEADS):
            halves = [zeros] * N_KV
            halves[h // GROUP] = q[:, h * HEAD_DIM:(h + 1) * HEAD_DIM]
            blocks.append(jnp.concatenate(halves, axis=1))
        qbd = jnp.concatenate(blocks, axis=0).astype(BF16)
        v["qbd"] = qbd
        kvw_new = kvw_ref[s]
        s_first = scores_t(qbd, swin_ref[0, s, 0:KV_LANES, 0:LANE]) + delta_ref[:, SD_EDGE:SD_EDGE + LANE]
        s_mid = scores_t(qbd, swin_ref[0, s, 0:KV_LANES, LANE:wb - LANE])
        s_lastw = scores_t(qbd, swin_ref[0, s, 0:KV_LANES, wb - LANE:wb]) + d_last
        v["s_wold"] = jnp.concatenate([s_first, s_mid, s_lastw], axis=1)
        v["s_wnew"] = scores(qbd, pad_new(kvw_new[:, 0:KV_LANES])) + d_new
        kvs_new = kvs_ref[s]
        s_far = scores_t(qbd, sbuf[slot, s, 0:KV_LANES, 0:past - LANE])
        s_last = scores_t(qbd, sbuf[slot, s, 0:KV_LANES, past - LANE:past]) + d_last
        v["s_past"] = jnp.concatenate([s_far, s_last], axis=1)
        v["s_new"] = scores(qbd, pad_new(kvs_new[:, 0:KV_LANES])) + d_new

    def stage_regroup(s):
        for p in range(n_pages):
            tok = slice(p * PAGE_SIZE, (p + 1) * PAGE_SIZE)
            for x_s, feat in ((xk_s, slice(0, KV_LANES)), (xv_s, slice(KV_LANES, KV_COLS))):
                xt = cbuf[slot, s, feat, tok].T
                for r in range(PAGE_SIZE // 8):
                    l0 = (8 * r) % CMP_STRIDE
                    sub = p * sub_per_page + (8 * r) // CMP_STRIDE
                    x_s[s, pl.ds(l0 * pitch + sub, 8, stride=pitch), :] = xt[8 * r:8 * r + 8, :]

    def stage_window(s):
        v = val[s]
        kvw_new = kvw_ref[s]
        v["o_win"], v["l_win"] = softmax_pv(v["s_wold"], swin_ref[0, s, KV_LANES:KV_COLS, :],
                                            v["s_wnew"], pad_new(kvw_new[:, KV_LANES:KV_COLS]))
        shifted = pltpu.roll(swin_ref[0, s], wb - tq, axis=1)
        new_t = jnp.concatenate([pad_rows(kvw_new[:, 0:KV_LANES]).T, pad_rows(kvw_new[:, KV_LANES:KV_COLS]).T], axis=0)
        new_t = pltpu.roll(new_t, LANE - tq, axis=1)
        lane = lax.broadcasted_iota(jnp.int32, (KV_COLS, LANE), 1)
        nwin_ref[0, s, :, 0:wb - LANE] = shifted[:, 0:wb - LANE]
        nwin_ref[0, s, :, wb - LANE:wb] = jnp.where(lane >= LANE - tq, new_t, shifted[:, wb - LANE:wb])

    def stage_compress(s):
        v = val[s]
        x_of = lambda l, j: (xk_s, xv_s)[j][s, l * pitch:l * pitch + n_sub, :]
        ck, cv = _compress_tokens(x_of, wcmp_ref, pe_ref, n_sub)
        v["cv"] = cv.astype(BF16)
        v["sc"] = scores(v["qbd"], ck.astype(BF16)) + delta_ref[:, SD_CMP:SD_CMP + n_sub]

    def stage_compressed(s):
        v = val[s]
        sc = v["sc"]
        m_c = jnp.max(sc, axis=1, keepdims=True)
        p = jnp.exp2(sc - m_c)
        l_c = jnp.sum(p, axis=1, keepdims=True)
        pn = p * jnp.where(m_c > VALID_MIN, 1.0 / l_c, 0.0)
        v["o_cmp"] = jnp.dot(pn.astype(BF16), v["cv"], preferred_element_type=F32)
        v["imp"] = []
        for g in range(N_KV):
            ps = pn[g * GROUP * tq:(g * GROUP + 1) * tq, :]
            for r in range(1, GROUP):
                ps = ps + pn[(g * GROUP + r) * tq:(g * GROUP + r + 1) * tq, :]
            hi = ps.astype(BF16)
            lo = (ps - hi.astype(F32)).astype(BF16)
            v["imp"].append(jnp.dot(hi, mimp_ref[...], preferred_element_type=F32)
                            + jnp.dot(lo, mimp_ref[...], preferred_element_type=F32))

    jidx = lax.broadcasted_iota(jnp.int32, (tq, LANE), 1)
    cur = (past + lax.broadcasted_iota(jnp.int32, (tq, LANE), 0)) // SEL_BLOCK
    forced = (jidx == 0) | (jidx == cur) | (jidx == cur - 1)

    def stage_select(s):
        v = val[s]
        sel_g = []
        for imp in v["imp"]:
            score = jnp.where(forced, jnp.inf, imp)
            score = jnp.where((jidx > cur) | (jidx >= n_sel), -jnp.inf, score)
            cnt = jnp.zeros((tq, LANE), jnp.int32)
            for jp in range(n_sel):
                col = score[:, jp:jp + 1]
                cnt = cnt + jnp.where(jidx > jp, jnp.where(col >= score, 1, 0), jnp.where(col > score, 1, 0))
            sel = (cnt < min(N_SELECT, n_sel)) & (jidx <= cur) & (jidx < n_sel)
            sel_g.append(jnp.where(sel, 1.0, 0.0))
        sel_rows = jnp.concatenate([sel_g[h // GROUP] for h in range(N_HEADS)], axis=0).astype(BF16)
        v["mask"] = jnp.dot(sel_rows, expand_ref[...], preferred_element_type=F32)

    def stage_selected(s):
        v = val[s]
        s_past = jnp.where(v["mask"] > 0.5, v["s_past"], NEG)
        v["o_sel"], v["l_sel"] = softmax_pv(s_past, sbuf[slot, s, KV_LANES:KV_COLS, :], v["s_new"],
                                            pad_new(kvs_ref[s][:, KV_LANES:KV_COLS]))

    def stage_output(s):
        v = val[s]
        gates = gate_ref[s]
        gate_col = lambda x: jnp.concatenate([gates[:, N_BRANCH * h + x:N_BRANCH * h + x + 1] for h in range(N_HEADS)],
                                             axis=0)
        o = (v["o_cmp"] * gate_col(0) + v["o_sel"] * (gate_col(1) / v["l_sel"])
             + v["o_win"] * (gate_col(2) / v["l_win"]))
        pieces = []
        for h in range(N_HEADS):
            g = h // GROUP
            pieces.append(o[h * tq:(h + 1) * tq, g * HEAD_DIM:(g + 1) * HEAD_DIM])
        mix_ref[s, :, 0:ATT_WIDTH] = jnp.concatenate(pieces, axis=1).astype(BF16)
        u = u_ref[s]
        ext = ext_s.at[s]
        ext[0:1, :] = jnp.zeros((1, POOL_WIDTH), F32)
        ext[1:POOL_HALO, :] = spool_ref[0, s]
        ext[POOL_HALO:POOL_HALO + tq, :] = u
        pos = past + lax.broadcasted_iota(jnp.int32, (tq, 1), 0)
        for k, o_k in enumerate(_pool_mix(ext, u, pos, wpool_ref, pscale_ref, tq)):
            mix_ref[s, :, ATT_WIDTH + k * POOL_GROUP_DIM:ATT_WIDTH + (k + 1) * POOL_GROUP_DIM] = o_k
        npool_ref[s] = ext[POOL_HALO + tq - POOL_STATE:POOL_HALO + tq, :]

    for stage in (stage_scores, stage_regroup, stage_window, stage_compress, stage_compressed, stage_select,
                  stage_selected, stage_output):
        for s in range(n_seq):
            stage(s)


def _sample_attention(layer, page_table, q, gates, u, kvs, kvw, swin_t, state_pool, ccache_t, scache_t, nwin_all,
                      wcmp, pe_rows, delta, mimp, expand, wpool, pscale):
    nb, tq, _ = q.shape
    n_pages = page_table.shape[1]
    past = n_pages * PAGE_SIZE
    n_sub = past // CMP_STRIDE
    wb = swin_t.shape[3]
    assert wb == WINDOW and wb >= 3 * LANE and tq <= CMP_STRIDE and POOL_HALO + tq - POOL_STATE >= 0
    assert past // SEL_BLOCK + 1 <= LANE and past % LANE == 0 and n_sub <= LANE
    ns = SAMPLE_SEQS if nb % SAMPLE_SEQS == 0 else 1
    full = lambda a: pl.BlockSpec(a.shape, lambda i, pt: (0,) * a.ndim)
    per_b = lambda c: pl.BlockSpec((ns, tq, c), lambda i, pt: (i, 0, 0))
    layer_b = lambda a: pl.BlockSpec((1, ns) + a.shape[2:], lambda i, pt: (layer, i) + (0,) * (a.ndim - 2))
    hbm = pl.BlockSpec(memory_space=pl.ANY)
    grid_spec = pltpu.PrefetchScalarGridSpec(
        num_scalar_prefetch=1,
        grid=(nb // ns,),
        in_specs=[per_b(ATT_WIDTH), per_b(LANE), per_b(POOL_WIDTH), per_b(KV_COLS), per_b(KV_COLS),
                  layer_b(swin_t), layer_b(state_pool), hbm, hbm, hbm,
                  full(wcmp), full(pe_rows), full(delta), full(mimp), full(expand), full(wpool), full(pscale)],
        out_specs=[pl.BlockSpec((ns, tq, D_MODEL), lambda i, pt: (i, 0, 0)),
                   layer_b(nwin_all),
                   pl.BlockSpec((ns, POOL_STATE, POOL_WIDTH), lambda i, pt: (i, 0, 0))],
        scratch_shapes=[pltpu.VMEM((2, ns, KV_COLS, past), F32), pltpu.VMEM((2, ns, KV_COLS, past), F32),
                        pltpu.VMEM((ns, CMP_STRIDE * (n_sub + 8), KV_LANES), F32),
                        pltpu.VMEM((ns, CMP_STRIDE * (n_sub + 8), KV_LANES), F32),
                        pltpu.SemaphoreType.DMA((2, 2)), pltpu.VMEM((ns, POOL_HALO + tq, POOL_WIDTH), F32)],
    )
    return pl.pallas_call(
        functools.partial(_sample_attn_kernel, layer=layer),
        grid_spec=grid_spec,
        out_shape=[jax.ShapeDtypeStruct((nb, tq, D_MODEL), BF16),
                   jax.ShapeDtypeStruct(nwin_all.shape, F32),
                   jax.ShapeDtypeStruct((nb, POOL_STATE, POOL_WIDTH), F32)],
        input_output_aliases={10: 1},
        compiler_params=pltpu.CompilerParams(dimension_semantics=("arbitrary",), vmem_limit_bytes=VMEM_LIMIT),
    )(page_table, q, gates, u, kvs, kvw, swin_t, state_pool, ccache_t, scache_t, nwin_all,
      wcmp, pe_rows, delta, mimp, expand, wpool, pscale)


def _ffn_kernel(x_ref, mix_ref, wout_ref, gn_ref, win_ref, wd_ref, gf_ref, o_ref, *, final_norm):
    xm = x_ref[...] + jnp.dot(mix_ref[...], wout_ref[...], preferred_element_type=F32)
    h = _rms(xm, gn_ref[...]).astype(BF16)
    d_ff = wd_ref.shape[0]
    cuts = list(range(0, d_ff, FFN_SLICE)) + [d_ff]
    slices = list(zip(cuts[:-1], cuts[1:]))

    def gate_up(a, b):
        return (jnp.dot(h, win_ref[:, a:b], preferred_element_type=F32),
                jnp.dot(h, win_ref[:, d_ff + a:d_ff + b], preferred_element_type=F32))

    y = xm
    pending = gate_up(*slices[0])
    for j, (a, b) in enumerate(slices):
        gate, up = pending
        if j + 1 < len(slices):
            pending = gate_up(*slices[j + 1])
        act = (gate * _sigmoid(gate) * up).astype(BF16)
        y = y + jnp.dot(act, wd_ref[a:b, :], preferred_element_type=F32)
    o_ref[...] = _rms(y, gf_ref[...]) if final_norm else y


def _ffn(x, mix, wout, gn, wffn_in, wffn_out, gf, final_norm):
    n = x.shape[0]
    tm = min(ROW_TILE, n)
    assert n % tm == 0 and wffn_out.shape[0] % LANE == 0
    once = lambda a: pl.BlockSpec(a.shape, lambda i: (0,) * a.ndim, pipeline_mode=pl.Buffered(1))
    row = lambda cdim: pl.BlockSpec((tm, cdim), lambda i: (i, 0))
    return pl.pallas_call(
        functools.partial(_ffn_kernel, final_norm=final_norm),
        grid=(n // tm,),
        in_specs=[row(D_MODEL), row(D_MODEL), once(wout), once(gn), once(wffn_in), once(wffn_out), once(gf)],
        out_specs=row(D_MODEL),
        out_shape=jax.ShapeDtypeStruct((n, D_MODEL), F32),
        compiler_params=pltpu.CompilerParams(dimension_semantics=("parallel",), vmem_limit_bytes=VMEM_LIMIT),
    )(x, mix, wout, gn, wffn_in, wffn_out, gf)


def _importance_matrix(n_sel, n_cmp_rows, n_cmp):
    spb = SEL_BLOCK // CMP_STRIDE
    j = np.arange(n_sel)[:, None]
    c = np.arange(n_cmp_rows)[None, :]
    return ((c >= spb * j - (CMP_RATIO - 1)) & (c <= spb * j + spb - 1) & (c < n_cmp)).astype(np.float32)


def kernel(x_prompt, x_sample, cache_cmp, cache_sel, state_win, state_pool, page_table, rel_bias, norm_mix, norm_ffn,
           norm_final, w_in, w_out, cmp_pos, w_cmp, w_pool, pool_scale, w_ffn_in, w_ffn_out):
    bp, t, _ = x_prompt.shape
    bs, tq, _ = x_sample.shape
    depth = w_in.shape[0]
    n_pages = page_table.shape[1]
    past = n_pages * PAGE_SIZE
    n_phys = cache_cmp.shape[1]
    wb = state_win.shape[2]
    _check_far_bucket(max(t, past + tq) + WINDOW)

    rb_prompt = jnp.repeat(rel_bias.astype(F32), TQ, axis=1)
    delta_p = _bias_tiles(_prompt_bucket_table(), rb_prompt)
    rb_sample = jnp.pad(jnp.repeat(rel_bias.astype(F32), tq, axis=1), ((0, 0), (0, LANE - N_HEADS * tq)))
    delta_s = _bias_tiles(_sample_bucket_table(past, tq, wb), rb_sample).T[:N_HEADS * tq]

    n_sub_p = t // CMP_STRIDE
    mimp_p = jnp.asarray(_importance_matrix(t // SEL_BLOCK, n_sub_p, n_sub_p - CMP_RATIO + 1), BF16)
    eblk_p = jnp.asarray((np.arange(t)[:, None] // SEL_BLOCK == np.arange(LANE)[None, :]).astype(np.float32), BF16)
    n_sub_s = past // CMP_STRIDE
    mimp_s =jnp.asarray(_importance_matrix(LANE, n_sub_s, n_sub_s - CMP_RATIO + 1).T, BF16)
    expand = jnp.asarray((np.arange(LANE)[:, None] == np.arange(past)[None, :] // SEL_BLOCK).astype(np.float32), BF16)

    pool_zero = jnp.zeros((bp, POOL_HALO, POOL_WIDTH), F32)
    feature_major = lambda a: jnp.moveaxis(a, 2, -1).reshape(a.shape[0], a.shape[1], KV_COLS, a.shape[2])
    token_major = lambda a: jnp.moveaxis(a.reshape(a.shape[:2] + (2, N_KV, HEAD_DIM, a.shape[3])), -1, 2)
    ccache_t = feature_major(cache_cmp)
    scache_t = feature_major(cache_sel)
    swin_t = feature_major(state_win)
    kvct_all = jnp.zeros((depth, bp, KV_COLS, t), F32)
    kvst_all = jnp.zeros((depth, bp, KV_COLS, t), F32)
    nwin_all = jnp.zeros((depth, bs, KV_COLS, wb), F32)

    xp = x_prompt.reshape(bp * t, D_MODEL)
    xs = x_sample.reshape(bs * tq, D_MODEL)
    outs = {k: [] for k in ("p_win", "p_pool", "s_cmp", "s_sel", "s_pool")}
    kv_shape = lambda b, n: (b, n, 2, N_KV, HEAD_DIM)
    off_g = ATT_WIDTH + N_BRANCH * KV_COLS
    for l in range(depth):
        wl = w_in[l]
        w_all = jnp.concatenate([wl[:, :ATT_WIDTH] * (HEAD_DIM ** -0.5 * LOG2E), wl[:, ATT_WIDTH:off_g],
                                 wl[:, off_g + GATE_COLS:], wl[:, off_g:off_g + GATE_COLS],
                                 jnp.zeros((D_MODEL, LANE - GATE_COLS), F32)], axis=1).astype(BF16)
        g_mix = norm_mix[l].reshape(1, D_MODEL)
        g_ffn = norm_ffn[l].reshape(1, D_MODEL)
        g_fin = norm_final.reshape(1, D_MODEL)
        wc = w_cmp[l].reshape(2, CMP_RATIO, CMP_STRIDE, HEAD_DIM, HEAD_DIM)
        zero = jnp.zeros_like(wc)
        w_bd = jnp.concatenate([jnp.concatenate([wc, zero], axis=-1), jnp.concatenate([zero, wc], axis=-1)],
                               axis=-2).astype(BF16)
        pe = cmp_pos[l].reshape(CMP_RATIO, CMP_STRIDE, 2, HEAD_DIM).transpose(2, 0, 1, 3)
        pe_rows = jnp.tile(pe.reshape(2 * CMP_RATIO * CMP_STRIDE, HEAD_DIM), (1, N_KV)).astype(F32)
        wpool = w_pool[l].astype(BF16)
        pscale = pool_scale[l].reshape(1, POOL_WIDTH)
        wout = w_out[l].astype(BF16)
        wffn_in = w_ffn_in[l].astype(BF16)
        wffn_out = w_ffn_out[l].astype(BF16)
        last = l == depth - 1

        q, u, gates, kvc, kb, vst, vwt, kvct_all, kvst_all, kvwt = _inproj_seq(l, xp, g_mix, w_all, kvct_all, kvst_all)
        ck, cvt = _compress_prompt(kvc.reshape(bp, t, KV_COLS), w_bd, pe_rows)
        mix = _prompt_attention(q.reshape(bp, t, ATT_WIDTH), gates.reshape(bp, t, LANE), u.reshape(bp, t, POOL_WIDTH),
                                pool_zero, kb.reshape(bp, t, KV_COLS), vst, vwt, ck, cvt, delta_p, mimp_p, eblk_p,
                                wpool, pscale)
        xp = _ffn(xp, mix.reshape(bp * t, D_MODEL), wout, g_ffn, wffn_in, wffn_out, g_fin, last)
        outs["p_win"].append(kvwt[:, :, t - wb:])
        outs["p_pool"].append(u.reshape(bp, t, POOL_WIDTH)[:, t - POOL_STATE:])

        q, u, gates, kvc, kvs, kvw = _inproj_rows(xs, g_mix, w_all)
        mix, nwin_all, npool = _sample_attention(
            l, page_table, q.reshape(bs, tq, ATT_WIDTH), gates.reshape(bs, tq, LANE), u.reshape(bs, tq, POOL_WIDTH),
            kvs.reshape(bs, tq, KV_COLS), kvw.reshape(bs, tq, KV_COLS), swin_t, state_pool, ccache_t, scache_t,
            nwin_all, w_bd, pe_rows, delta_s, mimp_s, expand, wpool, pscale)
        xs = _ffn(xs, mix.reshape(bs * tq, D_MODEL), wout, g_ffn, wffn_in, wffn_out, g_fin, last)
        outs["s_cmp"].append(kvc.reshape(kv_shape(bs, tq)))
        outs["s_sel"].append(kvs.reshape(kv_shape(bs, tq)))
        outs["s_pool"].append(npool)

    return (xp.reshape(bp, t, D_MODEL), xs.reshape(bs, tq, D_MODEL),
            token_major(kvct_all), token_major(kvst_all), token_major(jnp.stack(outs["p_win"])),
            jnp.stack(outs["p_pool"]), jnp.stack(outs["s_cmp"]), jnp.stack(outs["s_sel"]), token_major(nwin_all),
            jnp.stack(outs["s_pool"]))
```

```python
import functools
import math

import numpy as np
import jax
import jax.numpy as jnp
from jax import lax
from jax.experimental import pallas as pl
from jax.experimental.pallas import tpu as pltpu

D_MODEL = 1024
HEAD_DIM = 64
N_HEADS = 8
N_KV = 2
GROUP = N_HEADS // N_KV
ATT_WIDTH = N_HEADS * HEAD_DIM
POOL_WIDTH = D_MODEL - ATT_WIDTH
KV_COLS = 2 * N_KV * HEAD_DIM
KV_LANES = N_KV * HEAD_DIM
VT_ROWS = KV_LANES + 16
CMP_LEN = 32
CMP_STRIDE = 16
CMP_RATIO = CMP_LEN // CMP_STRIDE
SEL_BLOCK = 64
N_SELECT = 16
WINDOW = 512
N_BRANCH = 3
GATE_COLS = N_HEADS * N_BRANCH
POOL_WINDOWS = (2, 4, 8, 16)
POOL_GROUP_DIM = POOL_WIDTH // len(POOL_WINDOWS)
POOL_STATE = max(POOL_WINDOWS) - 1
POOL_HALO = 16
N_BUCKETS = 32
MAX_DISTANCE = 128
PAGE_SIZE = 128
EPS = 1e-6
NEG = -1e30
VALID_MIN = -1e29

LANE = 128
TQ = 128
PROMPT_QBLOCKS = 2
FAR_TILE = 512
CHAIN_COLS = 256
LOG2E = 1.4426950408889634
ROW_TILE = 512
FFN_SLICE = 512
VMEM_LIMIT = 52 * 1024 * 1024

F32 = jnp.float32
BF16 = jnp.bfloat16
NT_DIMS = (((1,), (1,)), ((), ()))


def _bucket_np(dist):
    n = np.maximum(dist, 0)
    max_exact = N_BUCKETS // 2
    nf = np.maximum(n, 1).astype(np.float32)
    large = max_exact + (np.log(nf / max_exact) / math.log(MAX_DISTANCE / max_exact)
                         * (N_BUCKETS - max_exact)).astype(np.int32)
    return np.where(n < max_exact, n, np.minimum(large, N_BUCKETS - 1)).astype(np.int32)


def _bucket_or_masked(dist, valid):
    return np.where(valid, _bucket_np(dist), -1).astype(np.int32)


FAR_DIST = TQ + 1


def _check_far_bucket(max_dist):
    assert (_bucket_np(np.arange(FAR_DIST, max_dist + 1)) == N_BUCKETS - 1).all()


def _bias_tile_kernel(bucket_ref, rb_ref, out_ref):
    bucket = bucket_ref[...]
    acc = jnp.zeros(bucket.shape, F32)
    for b in range(N_BUCKETS):
        acc = acc + jnp.where(bucket == b, rb_ref[b:b + 1, :], 0.0)
    acc = (acc - rb_ref[N_BUCKETS - 1:N_BUCKETS, :]) * LOG2E
    out_ref[...] = jnp.where(bucket < 0, NEG, acc)


def _bias_tiles(bucket_np, rb_cols):
    rows, cols = bucket_np.shape
    return pl.pallas_call(
        _bias_tile_kernel,
        out_shape=jax.ShapeDtypeStruct((rows, cols), F32),
    )(jnp.asarray(bucket_np), rb_cols)


DD0 = 0
DC0 = 2 * TQ
DC_ROWS = 24
DW0 = DC0 + 32
PROMPT_DELTA_ROWS = DW0 + TQ


def _prompt_bucket_table():
    i = np.arange(TQ)[None, :]
    kk = np.arange(2 * TQ)[:, None]
    d = i + TQ - kk
    dd = _bucket_or_masked(d, d >= 0)
    cc = np.arange(32)[:, None] - 16
    d = i - CMP_STRIDE * cc - (CMP_LEN - 1)
    dc = _bucket_or_masked(d, (d >= 0) & (cc < 8))
    j = np.arange(TQ)[:, None]
    d = i + WINDOW - j
    dw = _bucket_or_masked(d, (d >= 0) & (d < WINDOW))
    tab = np.concatenate([dd, dc, dw], axis=0)
    return np.tile(tab, (1, N_HEADS))


def _sample_bucket_table(past, tq, wb):
    col = np.arange(LANE)
    t = (col % tq)[None, :]
    colok = (col < N_HEADS * tq)[None, :]
    kk = np.arange(LANE)[:, None]
    d = LANE + t - kk
    dlast = _bucket_or_masked(d, colok & (d >= 0))
    d = t - kk
    dnew = _bucket_or_masked(d, colok & (d >= 0) & (kk < tq))
    d = wb + t - kk
    dedge = _bucket_or_masked(d, colok & (d >= 0) & (d < WINDOW))
    nsub = past // CMP_STRIDE
    c = np.arange(nsub)[:, None]
    d = past + t - CMP_STRIDE * c - (CMP_LEN - 1)
    dcmp = _bucket_or_masked(d, colok & (d >= 0) & (c < nsub - CMP_RATIO + 1))
    return np.concatenate([dlast, dnew, dedge, dcmp], axis=0)


IN_Q = 0
IN_KV = ATT_WIDTH
IN_U = IN_KV + N_BRANCH * KV_COLS
IN_G = IN_U + POOL_WIDTH
IN_COLS_PAD = IN_G + LANE


def _rms(x, g):
    return x * lax.rsqrt(jnp.mean(x * x, axis=-1, keepdims=True) + EPS) * g


def _sigmoid(x):
    return 1.0 / (1.0 + jnp.exp(-x))


def _inproj_common(x_ref, g_ref, w_ref, q_ref, u_ref, gate_ref):
    h = _rms(x_ref[...], g_ref[...]).astype(BF16)
    proj = lambda a, b: jnp.dot(h, w_ref[:, a:b], preferred_element_type=F32)
    q_ref[...] = proj(IN_Q, IN_KV).astype(BF16)
    kv = proj(IN_KV, IN_U)
    u_ref[...] = proj(IN_U, IN_G)
    gate_ref[...] = _sigmoid(proj(IN_G, IN_COLS_PAD))
    return [kv[:, i * KV_COLS:(i + 1) * KV_COLS] for i in range(N_BRANCH)]


def _inproj_rows_kernel(x_ref, g_ref, w_ref, q_ref, u_ref, gate_ref, kvc_ref, kvs_ref, kvw_ref):
    kvc, kvs, kvw = _inproj_common(x_ref, g_ref, w_ref, q_ref, u_ref, gate_ref)
    kvc_ref[...] = kvc
    kvs_ref[...] = kvs
    kvw_ref[...] = kvw


def _inproj_seq_kernel(x_ref, g_ref, w_ref, *refs):
    q_ref, u_ref, gate_ref, kb_ref, vst_ref, vwt_ref, kvct_ref, kvst_ref, kvwt_ref = refs[-9:]
    kvc, kvs, kvw = _inproj_common(x_ref, g_ref, w_ref, q_ref, u_ref, gate_ref)
    kb_ref[...] = jnp.concatenate([kvs[:, 0:KV_LANES], kvw[:, 0:KV_LANES]], axis=1).astype(BF16)
    kvst = kvs.T
    kvwt = kvw.T
    kvct_ref[0, 0] = kvc.T
    kvst_ref[0, 0] = kvst
    kvwt_ref[0] = kvwt
    ones = jnp.ones((VT_ROWS - KV_LANES, kvst.shape[1]), F32)
    vst_ref[0] = jnp.concatenate([kvst[KV_LANES:KV_COLS], ones], axis=0).astype(BF16)
    vwt_ref[0] = jnp.concatenate([kvwt[KV_LANES:KV_COLS], ones], axis=0).astype(BF16)


def _inproj_rows(x, g, w):
    n = x.shape[0]
    tm = min(ROW_TILE, n)
    row = lambda c: pl.BlockSpec((tm, c), lambda i: (i, 0))
    full = lambda a: pl.BlockSpec(a.shape, lambda i: (0,) * a.ndim)
    outs = [(ATT_WIDTH, BF16), (POOL_WIDTH, F32), (LANE, F32), (KV_COLS, F32), (KV_COLS, F32), (KV_COLS, F32)]
    return pl.pallas_call(
        _inproj_rows_kernel,
        grid=(n // tm,),
        in_specs=[row(D_MODEL), full(g), full(w)],
        out_specs=[row(c) for c, _ in outs],
        out_shape=[jax.ShapeDtypeStruct((n, c), dt) for c, dt in outs],
        compiler_params=pltpu.CompilerParams(dimension_semantics=("parallel",), vmem_limit_bytes=VMEM_LIMIT),
    )(x, g, w)


def _inproj_seq(layer, x, g, w, kvct_all, kvst_all):
    n = x.shape[0]
    all_shape = kvct_all.shape
    _, b, _, t = all_shape
    tm = min(ROW_TILE, t)
    tiles = t // tm
    row = lambda c: pl.BlockSpec((tm, c), lambda i: (i, 0))
    full = lambda a: pl.BlockSpec(a.shape, lambda i: (0,) * a.ndim)
    hbm = pl.BlockSpec(memory_space=pl.ANY)
    seq_t = lambda r: pl.BlockSpec((1, r, tm), lambda i: (i // tiles, 0, i % tiles))
    all_t = pl.BlockSpec((1, 1, KV_COLS, tm), lambda i: (layer, i // tiles, 0, i % tiles))
    rows = [(ATT_WIDTH, BF16), (POOL_WIDTH, F32), (LANE, F32), (KV_COLS, BF16)]
    carried = [kvct_all, kvst_all]
    first_all = len(rows) + 2
    return pl.pallas_call(
        _inproj_seq_kernel,
        grid=(n // tm,),
        in_specs=[row(D_MODEL), full(g), full(w)] + [hbm] * len(carried),
        out_specs=[row(c) for c, _ in rows] + [seq_t(VT_ROWS), seq_t(VT_ROWS), all_t, all_t, seq_t(KV_COLS)],
        out_shape=[jax.ShapeDtypeStruct((n, c), dt) for c, dt in rows]
        + [jax.ShapeDtypeStruct((b, VT_ROWS, t), BF16), jax.ShapeDtypeStruct((b, VT_ROWS, t), BF16),
           jax.ShapeDtypeStruct(all_shape, F32), jax.ShapeDtypeStruct(all_shape, F32),
           jax.ShapeDtypeStruct((b, KV_COLS, t), F32)],
        input_output_aliases={3 + k: first_all + k for k in range(len(carried))},
        compiler_params=pltpu.CompilerParams(dimension_semantics=("parallel",), vmem_limit_bytes=VMEM_LIMIT),
    )(x, g, w, *carried)


def _compress_tokens(x_of, w_ref, pe_ref, n_sub):
    outs = []
    for j in range(2):
        parts = [jnp.zeros((n_sub, KV_LANES), F32) for _ in range(CMP_RATIO)]
        for l in range(CMP_STRIDE):
            x = x_of(l, j)
            for r in range(CMP_RATIO):
                row = (j * CMP_RATIO + r) * CMP_STRIDE + l
                a = (x + pe_ref[row:row + 1, :]).astype(BF16)
                parts[r] = parts[r] + jnp.dot(a, w_ref[j, r, l], preferred_element_type=F32)
        comp = parts[0]
        for r in range(1, CMP_RATIO):
            comp = comp + pltpu.roll(parts[r], n_sub - r, axis=0)
        outs.append(comp)
    return outs


def _regroup_tokens(feat_tok, x_s, first_sub):
    pitch = x_s.shape[0] // CMP_STRIDE
    xt = feat_tok.T
    for r in range(xt.shape[0] // 8):
        l0 = (8 * r) % CMP_STRIDE
        sub = first_sub + (8 * r) // CMP_STRIDE
        x_s[pl.ds(l0 * pitch + sub, 8, stride=pitch), :] = xt[8 * r:8 * r + 8, :]


def _regrouped_rows(x_refs, n_sub):
    def x_of(l, j):
        pitch = x_refs[j].shape[0] // CMP_STRIDE
        return x_refs[j][l * pitch:l * pitch + n_sub, :]
    return x_of


def _compress_kernel(xt_ref, w_ref, pe_ref, ck_ref, cvt_ref, xk_s, xv_s):
    t = xt_ref.shape[3]
    n_sub = t // CMP_STRIDE
    for p in range(t // LANE):
        tok = slice(p * LANE, (p + 1) * LANE)
        _regroup_tokens(xt_ref[0, 0, 0:KV_LANES, tok], xk_s, p * (LANE // CMP_STRIDE))
        _regroup_tokens(xt_ref[0, 0, KV_LANES:KV_COLS, tok], xv_s, p * (LANE // CMP_STRIDE))
    ck, cv = _compress_tokens(_regrouped_rows((xk_s, xv_s), n_sub), w_ref, pe_ref, n_sub)
    ck_ref[0] = ck.astype(BF16)
    cvt_ref[0] = cv.T.astype(BF16)


def _compress_prompt(layer, kvct_all, w_bd, pe_rows):
    _, b, _, t = kvct_all.shape
    n_sub = t // CMP_STRIDE
    full = lambda a: pl.BlockSpec(a.shape, lambda i: (0,) * a.ndim)
    slabs = pltpu.VMEM((CMP_STRIDE * (n_sub + 8), KV_LANES), F32)
    return pl.pallas_call(
        _compress_kernel,
        grid=(b,),
        in_specs=[pl.BlockSpec((1, 1, KV_COLS, t), lambda i: (layer, i, 0, 0)), full(w_bd), full(pe_rows)],
        out_specs=[pl.BlockSpec((1, n_sub, KV_LANES), lambda i: (i, 0, 0)),
                   pl.BlockSpec((1, KV_LANES, n_sub), lambda i: (i, 0, 0))],
        out_shape=[jax.ShapeDtypeStruct((b, n_sub, KV_LANES), BF16), jax.ShapeDtypeStruct((b, KV_LANES, n_sub), BF16)],
        scratch_shapes=[slabs, slabs],
        compiler_params=pltpu.CompilerParams(dimension_semantics=("parallel",), vmem_limit_bytes=VMEM_LIMIT),
    )(kvct_all, w_bd, pe_rows)


def _split_dot(m, p):
    hi = p.astype(BF16)
    lo = (p - hi.astype(F32)).astype(BF16)
    return jnp.dot(m, hi, preferred_element_type=F32) + jnp.dot(m, lo, preferred_element_type=F32)


def _col_max(s):
    rows = s.shape[0]
    parts = [s[r:r + 64] for r in range(0, rows, 64)] if rows % 64 == 0 and rows > 64 else [s]
    while len(parts) > 1:
        parts = [jnp.maximum(a, b) for a, b in zip(parts[0::2], parts[1::2])] + ([parts[-1]] if len(parts) % 2 else [])
    return jnp.max(parts[0], axis=0, keepdims=True)


def _pool_mix(ext_ref, u, pos, wpool_ref, pscale_ref, rows):
    outs = []
    for k, w in enumerate(POOL_WINDOWS):
        ln = slice(k * POOL_GROUP_DIM, (k + 1) * POOL_GROUP_DIM)
        ws = ext_ref[POOL_HALO:POOL_HALO + rows, ln]
        for s in range(1, w):
            ws = ws + ext_ref[POOL_HALO - s:POOL_HALO - s + rows, ln]
        cnt = jnp.minimum(pos + 1, w).astype(F32)
        pooled = (ws / cnt - u[:, ln]).astype(BF16)
        o = jnp.dot(pooled, wpool_ref[k], preferred_element_type=F32) * pscale_ref[:, ln]
        outs.append(o.astype(BF16))
    return outs


def _prompt_attn_kernel(q_ref, gate_ref, u_ref, uh_ref, pst_ref, ksel_ref, kwin_ref, vst_ref, vwt_ref,
                        ck_ref, cvt_ref, delta_ref, mimp_ref, eblk_ref, wpool_ref, pscale_ref,
                        mix_ref,
                        sc_s, qa_s, qf_s, acc_s, m_s, ext_s):
    step = pl.program_id(1)
    n_blk = q_ref.shape[1] // TQ
    t_len = ksel_ref.shape[1]
    n_cmp = ck_ref.shape[1]
    n_sel = t_len // SEL_BLOCK
    cols = N_HEADS * TQ
    blk_per_q = TQ // SEL_BLOCK
    n_chain = cols // CHAIN_COLS
    chains = [slice(c * CHAIN_COLS, (c + 1) * CHAIN_COLS) for c in range(n_chain)]
    n_wc = WINDOW // TQ + 1
    w_delta = {0: DW0, n_wc - 2: DD0, n_wc - 1: DD0 + TQ}

    val = [dict() for _ in range(n_blk)]
    qb_of = lambda a: n_blk * step + a
    tok = lambda a: slice(a * TQ, (a + 1) * TQ)

    def stage_compressed(a):
        v, qb = val[a], qb_of(a)
        q = q_ref[0, tok(a), :].astype(F32)
        zeros = jnp.zeros((TQ, HEAD_DIM), F32)
        blocks = []
        for h in range(N_HEADS):
            halves = [zeros] * N_KV
            halves[h // GROUP] = q[:, h * HEAD_DIM:(h + 1) * HEAD_DIM]
            blocks.append(jnp.concatenate(halves, axis=1))
        qbd = jnp.concatenate(blocks, axis=0).astype(BF16)
        v["qbd"] = qbd
        sc = lax.dot_general(ck_ref[0], qbd, NT_DIMS, preferred_element_type=F32)
        crow = lax.broadcasted_iota(jnp.int32, (n_cmp, cols), 0)
        sc = jnp.where(crow >= 8 * qb + 8, NEG, sc)
        sc_s[a, 0:16, :] = jnp.zeros((16, cols), F32)
        sc_s[a, 16 + n_cmp:16 + n_cmp + 8, :] = jnp.zeros((8, cols), F32)
        sc_s[a, 16:16 + n_cmp, :] = sc
        near = pl.ds(pl.multiple_of(8 * qb, 8), DC_ROWS)
        sc_s[a, near, :] = sc_s[a, near, :] + delta_ref[DC0:DC0 + DC_ROWS, :]
        sc = sc_s[a, 16:16 + n_cmp, :]
        m_c = jnp.max(sc, axis=0, keepdims=True)
        p = jnp.exp2(sc - m_c)
        l_c = jnp.sum(p, axis=0, keepdims=True)
        pn = p * jnp.where(m_c > VALID_MIN, 1.0 / l_c, 0.0)
        v["o_cmp"] = jnp.dot(cvt_ref[0], pn.astype(BF16), preferred_element_type=F32)
        v["imp"] = []
        for g in range(N_KV):
            ps = pn[:, (g * GROUP) * TQ:(g * GROUP + 1) * TQ]
            for r in range(1, GROUP):
                ps = ps + pn[:, (g * GROUP + r) * TQ:(g * GROUP + r + 1) * TQ]
            v["imp"].append(_split_dot(mimp_ref[...], ps))

    def stage_window_scores(a):
        v, qb = val[a], qb_of(a)
        w_pos0 = [(qb - (n_wc - 1) + c) * TQ for c in range(n_wc)]
        w_k0 = [pl.multiple_of(jnp.maximum(p0, 0), TQ) for p0 in w_pos0]
        w_keys = [kwin_ref[0, pl.ds(k0, TQ), :] for k0 in w_k0]
        v["w_vt"] = [vwt_ref[0, :, pl.ds(k0, TQ)] for k0 in w_k0]
        v["s_win"] = []
        for cs in chains:
            s_w = []
            for c in range(n_wc):
                s = lax.dot_general(w_keys[c], v["qbd"][cs], NT_DIMS, preferred_element_type=F32)
                if c in w_delta:
                    s = s + delta_ref[w_delta[c]:w_delta[c] + TQ, cs]
                if c < n_wc - 1:
                    s = s + jnp.where(w_pos0[c] < 0, NEG, 0.0)
                s_w.append(s)
            v["s_win"].append(s_w)

    jidx = lax.broadcasted_iota(jnp.int32, (n_sel, TQ), 0)
    qi = lax.broadcasted_iota(jnp.int32, (n_sel, TQ), 1)
    pad = jnp.zeros((LANE - n_sel, TQ), F32)

    def stage_select(a):
        v, qb = val[a], qb_of(a)
        cur = blk_per_q * qb + qi // SEL_BLOCK
        forced = (jidx == 0) | (jidx == cur) | (jidx == cur - 1)
        prev_blk = blk_per_q * jnp.maximum(qb - 1, 0)
        ns_t = []
        for imp in v["imp"]:
            score = jnp.where(forced, jnp.inf, imp)
            score = jnp.where(jidx > cur, -jnp.inf, score)
            cnt = jnp.zeros((n_sel, TQ), jnp.int32)
            for jp in range(n_sel):
                row = score[jp:jp + 1, :]
                cnt = cnt + jnp.where(jidx > jp, jnp.where(row >= score, 1, 0), jnp.where(row > score, 1, 0))
            sel = (cnt < min(N_SELECT, n_sel)) & (jidx <= cur)
            ns_all = jnp.where(sel, 0.0, NEG)
            ns_far = jnp.where(jidx < prev_blk, ns_all, NEG)
            ns_t.append([jnp.concatenate([x, pad], axis=0).T.astype(BF16) for x in (ns_all, ns_far)])
        for h in range(N_HEADS):
            rows_h = slice(h * TQ, (h + 1) * TQ)
            qa_s[a, rows_h, 0:LANE] = v["qbd"][rows_h]
            qf_s[a, rows_h, 0:LANE] = v["qbd"][rows_h]
            qa_s[a, rows_h, LANE:2 * LANE] = ns_t[h // GROUP][0]
            qf_s[a, rows_h, LANE:2 * LANE] = ns_t[h // GROUP][1]
        m_s[a] = jnp.full((1, cols), NEG, F32)
        acc_s[a] = jnp.zeros((VT_ROWS, cols), F32)

    def online_update(a, cs, s, vt):
        m_old = m_s[a, :, cs]
        m_new = jnp.maximum(m_old, _col_max(s))
        alpha = jnp.exp2(m_old - m_new)
        p = jnp.exp2(s - m_new)
        acc_s[a, :, cs] = alpha * acc_s[a, :, cs] + jnp.dot(vt, p.astype(BF16), preferred_element_type=F32)
        m_s[a, :, cs] = m_new

    def keys_with_block(k0, n):
        return jnp.concatenate([ksel_ref[0, pl.ds(k0, n), :], eblk_ref[pl.ds(k0, n), :]], axis=1)

    def masked_scores(kcat, q_s, a, cs):
        return lax.dot_general(kcat, q_s[a, cs, :], NT_DIMS, preferred_element_type=F32)

    n_far = (TQ * jnp.maximum(qb_of(n_blk - 1) - 1, 0) + FAR_TILE - 1) // FAR_TILE

    def far_tiles(t0, n):
        k0 = [pl.multiple_of((t0 + i) * FAR_TILE, FAR_TILE) for i in range(n)]
        s_tiles = []
        for i in range(n):
            kcat = keys_with_block(k0[i], FAR_TILE)
            s_tiles.append([[masked_scores(kcat, qf_s, a, cs) for cs in chains] for a in range(n_blk)])
        for i in range(n):
            vt = vst_ref[0, :, pl.ds(k0[i], FAR_TILE)]
            for a in range(n_blk):
                for cs, s in zip(chains, s_tiles[i][a]):
                    online_update(a, cs, s, vt)

    def stage_near_scores(a):
        v, qb = val[a], qb_of(a)
        k_prev = pl.multiple_of(TQ * jnp.maximum(qb - 1, 0), TQ)
        k_diag = pl.multiple_of(qb * TQ, TQ)
        no_prev = jnp.where(qb == 0, NEG, 0.0)
        kcat_prev = keys_with_block(k_prev, TQ)
        kcat_diag = keys_with_block(k_diag, TQ)
        v["vt_near"] = jnp.concatenate([vst_ref[0, :, pl.ds(k_prev, TQ)], vst_ref[0, :, pl.ds(k_diag, TQ)]], axis=1)
        v["s_near"] = []
        for cs in chains:
            s_prev = masked_scores(kcat_prev, qa_s, a, cs) + delta_ref[DD0:DD0 + TQ, cs] + no_prev
            s_diag = masked_scores(kcat_diag, qa_s, a, cs) + delta_ref[DD0 + TQ:DD0 + 2 * TQ, cs]
            v["s_near"].append(jnp.concatenate([s_prev, s_diag], axis=0))

    def stage_near_update(a):
        v = val[a]
        for cs, s in zip(chains, v["s_near"]):
            online_update(a, cs, s, v["vt_near"])

    def stage_window(a):
        v = val[a]
        o_w = []
        for s_w in v["s_win"]:
            m_w = s_w[0].max(axis=0, keepdims=True)
            for s in s_w[1:]:
                m_w = jnp.maximum(m_w, s.max(axis=0, keepdims=True))
            o_c = jnp.zeros((VT_ROWS, CHAIN_COLS), F32)
            for s, vt in zip(s_w, v["w_vt"]):
                o_c = o_c + jnp.dot(vt, jnp.exp2(s - m_w).astype(BF16), preferred_element_type=F32)
            o_w.append(o_c)
        v["o_win"] = jnp.concatenate(o_w, axis=1)

    def stage_output(a):
        v, qb = val[a], qb_of(a)
        l_sel = acc_s[a, KV_LANES:KV_LANES + 1, :]
        o_sel = acc_s[a, 0:KV_LANES, :]
        l_win = v["o_win"][KV_LANES:KV_LANES + 1, :]
        o_win = v["o_win"][0:KV_LANES, :]
        gt = gate_ref[0, tok(a), :].T
        gate_row = lambda x: jnp.concatenate([gt[N_BRANCH * h + x:N_BRANCH * h + x + 1, :] for h in range(N_HEADS)],
                                             axis=1)
        o_t = v["o_cmp"] * gate_row(0) + o_sel * (gate_row(1) / l_sel) + o_win * (gate_row(2) / l_win)
        pieces = []
        for h in range(N_HEADS):
            g = h // GROUP
            pieces.append(o_t[g * HEAD_DIM:(g + 1) * HEAD_DIM, h * TQ:(h + 1) * TQ])
        mix_ref[0, tok(a), 0:ATT_WIDTH] = jnp.concatenate(pieces, axis=0).T.astype(BF16)
        u = u_ref[0, tok(a), :]
        ext = ext_s.at[a]
        if a == 0:
            ext[0:POOL_HALO, :] = jnp.where(qb == 0, pst_ref[0], uh_ref[0])
        else:
            ext[0:POOL_HALO, :] = u_ref[0, a * TQ - POOL_HALO:a * TQ, :]
        ext[POOL_HALO:POOL_HALO + TQ, :] = u
        pos = qb * TQ + lax.broadcasted_iota(jnp.int32, (TQ, 1), 0)
        for k, o in enumerate(_pool_mix(ext, u, pos, wpool_ref, pscale_ref, TQ)):
            mix_ref[0, tok(a), ATT_WIDTH + k * POOL_GROUP_DIM:ATT_WIDTH + (k + 1) * POOL_GROUP_DIM] = o

    def run(stage):
        for a in range(n_blk):
            stage(a)

    run(stage_compressed)
    run(stage_window_scores)
    run(stage_select)

    odd = lax.rem(n_far, 2)

    @pl.when(odd == 1)
    def _():
        far_tiles(0, 1)

    def pair_body(i, carry):
        far_tiles(odd + 2 * i, 2)
        return carry

    lax.fori_loop(0, lax.div(n_far, 2), pair_body, 0)

    run(stage_near_scores)
    run(stage_near_update)
    run(stage_window)
    run(stage_output)


def _prompt_attention(q, gates, u, pool_prev, kb, vst, vwt, ck, cvt, delta, mimp, eblk, wpool, pscale):
    b, t, _ = q.shape
    assert t % FAR_TILE == 0 and t >= WINDOW + TQ and t // SEL_BLOCK <= LANE
    n_cmp = ck.shape[1]
    cols = N_HEADS * TQ
    nq = PROMPT_QBLOCKS if (t // TQ) % PROMPT_QBLOCKS == 0 else 1
    rows = nq * TQ
    halo_per_step = rows // POOL_HALO
    full = lambda a: pl.BlockSpec(a.shape, lambda i, j: (0,) * a.ndim)
    qblk = lambda c: pl.BlockSpec((1, rows, c), lambda i, j: (i, j, 0))
    seq = lambda c: pl.BlockSpec((1, t, KV_LANES), lambda i, j: (i, 0, c))
    per_b = lambda a: pl.BlockSpec((1,) + a.shape[1:], lambda i, j: (i,) + (0,) * (a.ndim - 1))
    return pl.pallas_call(
        _prompt_attn_kernel,
        grid=(b, t // rows),
        in_specs=[qblk(ATT_WIDTH), qblk(LANE), qblk(POOL_WIDTH),
                  pl.BlockSpec((1, POOL_HALO, POOL_WIDTH), lambda i, j: (i, jnp.maximum(j * halo_per_step - 1, 0), 0)),
                  per_b(pool_prev), seq(0), seq(1), per_b(vst), per_b(vwt), per_b(ck), per_b(cvt),
                  full(delta), full(mimp), full(eblk), full(wpool), full(pscale)],
        out_specs=pl.BlockSpec((1, rows, D_MODEL), lambda i, j: (i, j, 0)),
        out_shape=jax.ShapeDtypeStruct((b, t, D_MODEL), BF16),
        scratch_shapes=[pltpu.VMEM((nq, 16 + n_cmp + 8, cols), F32),
                        pltpu.VMEM((nq, cols, 2 * LANE), BF16), pltpu.VMEM((nq, cols, 2 * LANE), BF16),
                        pltpu.VMEM((nq, VT_ROWS, cols), F32), pltpu.VMEM((nq, 1, cols), F32),
                        pltpu.VMEM((nq, POOL_HALO + TQ, POOL_WIDTH), F32)],
        compiler_params=pltpu.CompilerParams(dimension_semantics=("parallel", "arbitrary"),
                                             vmem_limit_bytes=VMEM_LIMIT),
    )(q, gates, u, u, pool_prev, kb, kb, vst, vwt, ck, cvt, delta, mimp, eblk, wpool, pscale)


SAMPLE_SEQS = 2
SD_LAST = 0
SD_NEW = LANE
SD_EDGE = 2 * LANE
SD_CMP = 3 * LANE


def _sample_attn_kernel(pt_ref, q_ref, gate_ref, u_ref, kvs_ref, kvw_ref, swin_ref, spool_ref,
                        ccache_ref, scache_ref, *refs, layer):
    (wcmp_ref, pe_ref, delta_ref, mimp_ref, expand_ref, wpool_ref, pscale_ref,
     mix_ref, nwin_ref, npool_ref, cbuf, sbuf, xk_s, xv_s, sem, ext_s) = refs[-16:]
    b = pl.program_id(0)
    nb = pl.num_programs(0)
    n_pages = pt_ref.shape[1]
    past = n_pages * PAGE_SIZE
    n_sub = past // CMP_STRIDE
    n_seq, tq = q_ref.shape[0], q_ref.shape[1]
    wb = swin_ref.shape[3]
    n_sel = past // SEL_BLOCK + 1
    sub_per_page = PAGE_SIZE // CMP_STRIDE
    slot = lax.rem(b, 2)

    def page_copies(step, slt):
        cps = []
        for s in range(n_seq):
            for p in range(n_pages):
                pg = pt_ref[step * n_seq + s, p]
                dst = (slice(None), pl.ds(p * PAGE_SIZE, PAGE_SIZE))
                cps.append(pltpu.make_async_copy(ccache_ref.at[layer, pg], cbuf.at[slt, s].at[dst], sem.at[0, slt]))
                cps.append(pltpu.make_async_copy(scache_ref.at[layer, pg], sbuf.at[slt, s].at[dst], sem.at[1, slt]))
        return cps

    @pl.when(b == 0)
    def _():
        for cp in page_copies(0, 0):
            cp.start()

    @pl.when(b + 1 < nb)
    def _():
        for cp in page_copies(b + 1, 1 - slot):
            cp.start()

    for cp in page_copies(b, slot):
        cp.wait()

    val = [dict() for _ in range(n_seq)]

    def scores(qbd, k_rows):
        return lax.dot_general(qbd, k_rows, NT_DIMS, preferred_element_type=F32)

    def scores_t(qbd, k_t):
        return jnp.dot(qbd, k_t.astype(BF16), preferred_element_type=F32)

    def pad_rows(x):
        return jnp.concatenate([x, jnp.zeros((LANE - tq, KV_LANES), F32)], axis=0)

    pad_new = lambda x: pad_rows(x).astype(BF16)
    d_last = delta_ref[:, SD_LAST:SD_LAST + LANE]
    d_new = delta_ref[:, SD_NEW:SD_NEW + LANE]

    def softmax_pv(s_old, vt_old, s_new, v_new):
        m = jnp.maximum(s_old.max(axis=1, keepdims=True), s_new.max(axis=1, keepdims=True))
        p_old = jnp.exp2(s_old - m)
        p_new = jnp.exp2(s_new - m)
        l = jnp.sum(p_old, axis=1, keepdims=True) + jnp.sum(p_new, axis=1, keepdims=True)
        o = (lax.dot_general(p_old.astype(BF16), vt_old.astype(BF16), NT_DIMS, preferred_element_type=F32)
             + jnp.dot(p_new.astype(BF16), v_new, preferred_element_type=F32))
        return o, l

    def stage_scores(s):
        v = val[s]
        q = q_ref[s].astype(F32)
        zeros = jnp.zeros((tq, HEAD_DIM), F32)
        blocks = []
        for h in range(N_HEADS):
            halves = [zeros] * N_KV
            halves[h // GROUP] = q[:, h * HEAD_DIM:(h + 1) * HEAD_DIM]
            blocks.append(jnp.concatenate(halves, axis=1))
        qbd = jnp.concatenate(blocks, axis=0).astype(BF16)
        v["qbd"] = qbd
        kvw_new = kvw_ref[s]
        s_first = scores_t(qbd, swin_ref[0, s, 0:KV_LANES, 0:LANE]) + delta_ref[:, SD_EDGE:SD_EDGE + LANE]
        s_mid = scores_t(qbd, swin_ref[0, s, 0:KV_LANES, LANE:wb - LANE])
        s_lastw = scores_t(qbd, swin_ref[0, s, 0:KV_LANES, wb - LANE:wb]) + d_last
        v["s_wold"] = jnp.concatenate([s_first, s_mid, s_lastw], axis=1)
        v["s_wnew"] = scores(qbd, pad_new(kvw_new[:, 0:KV_LANES])) + d_new
        kvs_new = kvs_ref[s]
        s_far = scores_t(qbd, sbuf[slot, s, 0:KV_LANES, 0:past - LANE])
        s_last = scores_t(qbd, sbuf[slot, s, 0:KV_LANES, past - LANE:past]) + d_last
        v["s_past"] = jnp.concatenate([s_far, s_last], axis=1)
        v["s_new"] = scores(qbd, pad_new(kvs_new[:, 0:KV_LANES])) + d_new

    def stage_regroup(s):
        for p in range(n_pages):
            tok = slice(p * PAGE_SIZE, (p + 1) * PAGE_SIZE)
            _regroup_tokens(cbuf[slot, s, 0:KV_LANES, tok], xk_s.at[s], p * sub_per_page)
            _regroup_tokens(cbuf[slot, s, KV_LANES:KV_COLS, tok], xv_s.at[s], p * sub_per_page)

    def stage_window(s):
        v = val[s]
        kvw_new = kvw_ref[s]
        v["o_win"], v["l_win"] = softmax_pv(v["s_wold"], swin_ref[0, s, KV_LANES:KV_COLS, :],
                                            v["s_wnew"], pad_new(kvw_new[:, KV_LANES:KV_COLS]))
        shifted = pltpu.roll(swin_ref[0, s], wb - tq, axis=1)
        new_t = jnp.concatenate([pad_rows(kvw_new[:, 0:KV_LANES]).T, pad_rows(kvw_new[:, KV_LANES:KV_COLS]).T], axis=0)
        new_t = pltpu.roll(new_t, LANE - tq, axis=1)
        lane = lax.broadcasted_iota(jnp.int32, (KV_COLS, LANE), 1)
        nwin_ref[0, s, :, 0:wb - LANE] = shifted[:, 0:wb - LANE]
        nwin_ref[0, s, :, wb - LANE:wb] = jnp.where(lane >= LANE - tq, new_t, shifted[:, wb - LANE:wb])

    def stage_compress(s):
        v = val[s]
        ck, cv = _compress_tokens(_regrouped_rows((xk_s.at[s], xv_s.at[s]), n_sub), wcmp_ref, pe_ref, n_sub)
        v["cv"] = cv.astype(BF16)
        v["sc"] = scores(v["qbd"], ck.astype(BF16)) + delta_ref[:, SD_CMP:SD_CMP + n_sub]

    def stage_compressed(s):
        v = val[s]
        sc = v["sc"]
        m_c = jnp.max(sc, axis=1, keepdims=True)
        p = jnp.exp2(sc - m_c)
        l_c = jnp.sum(p, axis=1, keepdims=True)
        pn = p * jnp.where(m_c > VALID_MIN, 1.0 / l_c, 0.0)
        v["o_cmp"] = jnp.dot(pn.astype(BF16), v["cv"], preferred_element_type=F32)
        v["imp"] = []
        for g in range(N_KV):
            ps = pn[g * GROUP * tq:(g * GROUP + 1) * tq, :]
            for r in range(1, GROUP):
                ps = ps + pn[(g * GROUP + r) * tq:(g * GROUP + r + 1) * tq, :]
            hi = ps.astype(BF16)
            lo = (ps - hi.astype(F32)).astype(BF16)
            v["imp"].append(jnp.dot(hi, mimp_ref[...], preferred_element_type=F32)
                            + jnp.dot(lo, mimp_ref[...], preferred_element_type=F32))

    jidx = lax.broadcasted_iota(jnp.int32, (tq, LANE), 1)
    cur = (past + lax.broadcasted_iota(jnp.int32, (tq, LANE), 0)) // SEL_BLOCK
    forced = (jidx == 0) | (jidx == cur) | (jidx == cur - 1)

    def stage_select(s):
        v = val[s]
        sel_g = []
        for imp in v["imp"]:
            score = jnp.where(forced, jnp.inf, imp)
            score = jnp.where((jidx > cur) | (jidx >= n_sel), -jnp.inf, score)
            cnt = jnp.zeros((tq, LANE), jnp.int32)
            for jp in range(n_sel):
                col = score[:, jp:jp + 1]
                cnt = cnt + jnp.where(jidx > jp, jnp.where(col >= score, 1, 0), jnp.where(col > score, 1, 0))
            sel = (cnt < min(N_SELECT, n_sel)) & (jidx <= cur) & (jidx < n_sel)
            sel_g.append(jnp.where(sel, 1.0, 0.0))
        sel_rows = jnp.concatenate([sel_g[h // GROUP] for h in range(N_HEADS)], axis=0).astype(BF16)
        v["mask"] = jnp.dot(sel_rows, expand_ref[...], preferred_element_type=F32)

    def stage_selected(s):
        v = val[s]
        s_past = jnp.where(v["mask"] > 0.5, v["s_past"], NEG)
        v["o_sel"], v["l_sel"] = softmax_pv(s_past, sbuf[slot, s, KV_LANES:KV_COLS, :], v["s_new"],
                                            pad_new(kvs_ref[s][:, KV_LANES:KV_COLS]))

    def stage_output(s):
        v = val[s]
        gates = gate_ref[s]
        gate_col = lambda x: jnp.concatenate([gates[:, N_BRANCH * h + x:N_BRANCH * h + x + 1] for h in range(N_HEADS)],
                                             axis=0)
        o = (v["o_cmp"] * gate_col(0) + v["o_sel"] * (gate_col(1) / v["l_sel"])
             + v["o_win"] * (gate_col(2) / v["l_win"]))
        pieces = []
        for h in range(N_HEADS):
            g = h // GROUP
            pieces.append(o[h * tq:(h + 1) * tq, g * HEAD_DIM:(g + 1) * HEAD_DIM])
        mix_ref[s, :, 0:ATT_WIDTH] = jnp.concatenate(pieces, axis=1).astype(BF16)
        u = u_ref[s]
        ext = ext_s.at[s]
        ext[0:1, :] = jnp.zeros((1, POOL_WIDTH), F32)
        ext[1:POOL_HALO, :] = spool_ref[0, s]
        ext[POOL_HALO:POOL_HALO + tq, :] = u
        pos = past + lax.broadcasted_iota(jnp.int32, (tq, 1), 0)
        for k, o_k in enumerate(_pool_mix(ext, u, pos, wpool_ref, pscale_ref, tq)):
            mix_ref[s, :, ATT_WIDTH + k * POOL_GROUP_DIM:ATT_WIDTH + (k + 1) * POOL_GROUP_DIM] = o_k
        npool_ref[s] = ext[POOL_HALO + tq - POOL_STATE:POOL_HALO + tq, :]

    for stage in (stage_scores, stage_regroup, stage_window, stage_compress, stage_compressed, stage_select,
                  stage_selected, stage_output):
        for s in range(n_seq):
            stage(s)


def _sample_attention(layer, page_table, q, gates, u, kvs, kvw, swin_t, state_pool, ccache_t, scache_t, nwin_all,
                      wcmp, pe_rows, delta, mimp, expand, wpool, pscale):
    nb, tq, _ = q.shape
    carried = [nwin_all]
    n_pages = page_table.shape[1]
    past = n_pages * PAGE_SIZE
    n_sub = past // CMP_STRIDE
    wb = swin_t.shape[3]
    assert wb == WINDOW and wb >= 3 * LANE and tq <= CMP_STRIDE and POOL_HALO + tq - POOL_STATE >= 0
    assert past // SEL_BLOCK + 1 <= LANE and past % LANE == 0 and n_sub <= LANE
    ns = SAMPLE_SEQS if nb % SAMPLE_SEQS == 0 else 1
    full = lambda a: pl.BlockSpec(a.shape, lambda i, pt: (0,) * a.ndim)
    per_b = lambda c: pl.BlockSpec((ns, tq, c), lambda i, pt: (i, 0, 0))
    layer_b = lambda a: pl.BlockSpec((1, ns) + a.shape[2:], lambda i, pt: (layer, i) + (0,) * (a.ndim - 2))
    hbm = pl.BlockSpec(memory_space=pl.ANY)
    grid_spec = pltpu.PrefetchScalarGridSpec(
        num_scalar_prefetch=1,
        grid=(nb // ns,),
        in_specs=[per_b(ATT_WIDTH), per_b(LANE), per_b(POOL_WIDTH), per_b(KV_COLS), per_b(KV_COLS),
                  layer_b(swin_t), layer_b(state_pool), hbm, hbm] + [hbm] * len(carried)
        + [full(wcmp), full(pe_rows), full(delta), full(mimp), full(expand), full(wpool), full(pscale)],
        out_specs=[pl.BlockSpec((ns, tq, D_MODEL), lambda i, pt: (i, 0, 0)),
                   layer_b(swin_t),
                   pl.BlockSpec((ns, POOL_STATE, POOL_WIDTH), lambda i, pt: (i, 0, 0))],
        scratch_shapes=[pltpu.VMEM((2, ns, KV_COLS, past), F32), pltpu.VMEM((2, ns, KV_COLS, past), F32),
                        pltpu.VMEM((ns, CMP_STRIDE * (n_sub + 8), KV_LANES), F32),
                        pltpu.VMEM((ns, CMP_STRIDE * (n_sub + 8), KV_LANES), F32),
                        pltpu.SemaphoreType.DMA((2, 2)), pltpu.VMEM((ns, POOL_HALO + tq, POOL_WIDTH), F32)],
    )
    return pl.pallas_call(
        functools.partial(_sample_attn_kernel, layer=layer),
        grid_spec=grid_spec,
        out_shape=[jax.ShapeDtypeStruct((nb, tq, D_MODEL), BF16),
                   jax.ShapeDtypeStruct(swin_t.shape, F32),
                   jax.ShapeDtypeStruct((nb, POOL_STATE, POOL_WIDTH), F32)],
        input_output_aliases={10: 1},
        compiler_params=pltpu.CompilerParams(dimension_semantics=("arbitrary",), vmem_limit_bytes=VMEM_LIMIT),
    )(page_table, q, gates, u, kvs, kvw, swin_t, state_pool, ccache_t, scache_t, *carried,
      wcmp, pe_rows, delta, mimp, expand, wpool, pscale)


def _ffn_kernel(x_ref, mix_ref, wout_ref, gn_ref, win_ref, wd_ref, gf_ref, o_ref, *, final_norm):
    xm = x_ref[...] + jnp.dot(mix_ref[...], wout_ref[...], preferred_element_type=F32)
    h = _rms(xm, gn_ref[...]).astype(BF16)
    d_ff = wd_ref.shape[0]
    cuts = list(range(0, d_ff, FFN_SLICE)) + [d_ff]
    slices = list(zip(cuts[:-1], cuts[1:]))

    def gate_up(a, b):
        return (jnp.dot(h, win_ref[:, a:b], preferred_element_type=F32),
                jnp.dot(h, win_ref[:, d_ff + a:d_ff + b], preferred_element_type=F32))

    y = xm
    pending = gate_up(*slices[0])
    for j, (a, b) in enumerate(slices):
        gate, up = pending
        if j + 1 < len(slices):
            pending = gate_up(*slices[j + 1])
        act = (gate * _sigmoid(gate) * up).astype(BF16)
        y = y + jnp.dot(act, wd_ref[a:b, :], preferred_element_type=F32)
    o_ref[...] = _rms(y, gf_ref[...]) if final_norm else y


def _ffn(x, mix, wout, gn, wffn_in, wffn_out, gf, final_norm):
    n = x.shape[0]
    tm = min(ROW_TILE, n)
    assert n % tm == 0 and wffn_out.shape[0] % LANE == 0
    once = lambda a: pl.BlockSpec(a.shape, lambda i: (0,) * a.ndim, pipeline_mode=pl.Buffered(1))
    row = lambda cdim: pl.BlockSpec((tm, cdim), lambda i: (i, 0))
    return pl.pallas_call(
        functools.partial(_ffn_kernel, final_norm=final_norm),
        grid=(n // tm,),
        in_specs=[row(D_MODEL), row(D_MODEL), once(wout), once(gn), once(wffn_in), once(wffn_out), once(gf)],
        out_specs=row(D_MODEL),
        out_shape=jax.ShapeDtypeStruct((n, D_MODEL), F32),
        compiler_params=pltpu.CompilerParams(dimension_semantics=("parallel",), vmem_limit_bytes=VMEM_LIMIT),
    )(x, mix, wout, gn, wffn_in, wffn_out, gf)


def _importance_matrix(n_sel, n_cmp_rows, n_cmp):
    spb = SEL_BLOCK // CMP_STRIDE
    j = np.arange(n_sel)[:, None]
    c = np.arange(n_cmp_rows)[None, :]
    return ((c >= spb * j - (CMP_RATIO - 1)) & (c <= spb * j + spb - 1) & (c < n_cmp)).astype(np.float32)


def kernel(x_prompt, x_sample, cache_cmp, cache_sel, state_win, state_pool, page_table, rel_bias, norm_mix, norm_ffn,
           norm_final, w_in, w_out, cmp_pos, w_cmp, w_pool, pool_scale, w_ffn_in, w_ffn_out):
    bp, t, _ = x_prompt.shape
    bs, tq, _ = x_sample.shape
    depth = w_in.shape[0]
    n_pages = page_table.shape[1]
    past = n_pages * PAGE_SIZE
    n_phys = cache_cmp.shape[1]
    wb = state_win.shape[2]
    _check_far_bucket(max(t, past + tq) + WINDOW)

    rb_prompt = jnp.repeat(rel_bias.astype(F32), TQ, axis=1)
    delta_p = _bias_tiles(_prompt_bucket_table(), rb_prompt)
    rb_sample = jnp.pad(jnp.repeat(rel_bias.astype(F32), tq, axis=1), ((0, 0), (0, LANE - N_HEADS * tq)))
    delta_s = _bias_tiles(_sample_bucket_table(past, tq, wb), rb_sample).T[:N_HEADS * tq]

    n_sub_p = t // CMP_STRIDE
    mimp_p = jnp.asarray(_importance_matrix(t // SEL_BLOCK, n_sub_p, n_sub_p - CMP_RATIO + 1), BF16)
    eblk_p = jnp.asarray((np.arange(t)[:, None] // SEL_BLOCK == np.arange(LANE)[None, :]).astype(np.float32), BF16)
    n_sub_s = past // CMP_STRIDE
    mimp_s =jnp.asarray(_importance_matrix(LANE, n_sub_s, n_sub_s - CMP_RATIO + 1).T, BF16)
    expand = jnp.asarray((np.arange(LANE)[:, None] == np.arange(past)[None, :] // SEL_BLOCK).astype(np.float32), BF16)

    pool_zero = jnp.zeros((bp, POOL_HALO, POOL_WIDTH), F32)
    feature_major = lambda a: jnp.moveaxis(a, 2, -1).reshape(a.shape[0], a.shape[1], KV_COLS, a.shape[2])
    token_major = lambda a: jnp.moveaxis(a.reshape(a.shape[:2] + (2, N_KV, HEAD_DIM, a.shape[3])), -1, 2)
    ccache_t = feature_major(cache_cmp)
    scache_t = feature_major(cache_sel)
    swin_t = feature_major(state_win)
    kvct_all = jnp.zeros((depth, bp, KV_COLS, t), F32)
    kvst_all = jnp.zeros((depth, bp, KV_COLS, t), F32)
    nwin_all = jnp.zeros((depth, bs, KV_COLS, wb), F32)

    xp = x_prompt.reshape(bp * t, D_MODEL)
    xs = x_sample.reshape(bs * tq, D_MODEL)
    outs = {k: [] for k in ("p_win", "p_pool", "s_cmp", "s_sel", "s_pool")}
    kv_shape = lambda b, n: (b, n, 2, N_KV, HEAD_DIM)
    off_g = ATT_WIDTH + N_BRANCH * KV_COLS
    for l in range(depth):
        wl = w_in[l]
        w_all = jnp.concatenate([wl[:, :ATT_WIDTH] * (HEAD_DIM ** -0.5 * LOG2E), wl[:, ATT_WIDTH:off_g],
                                 wl[:, off_g + GATE_COLS:], wl[:, off_g:off_g + GATE_COLS],
                                 jnp.zeros((D_MODEL, LANE - GATE_COLS), F32)], axis=1).astype(BF16)
        g_mix = norm_mix[l].reshape(1, D_MODEL)
        g_ffn = norm_ffn[l].reshape(1, D_MODEL)
        g_fin = norm_final.reshape(1, D_MODEL)
        wc = w_cmp[l].reshape(2, CMP_RATIO, CMP_STRIDE, HEAD_DIM, HEAD_DIM)
        zero = jnp.zeros_like(wc)
        w_bd = jnp.concatenate([jnp.concatenate([wc, zero], axis=-1), jnp.concatenate([zero, wc], axis=-1)],
                               axis=-2).astype(BF16)
        pe = cmp_pos[l].reshape(CMP_RATIO, CMP_STRIDE, 2, HEAD_DIM).transpose(2, 0, 1, 3)
        pe_rows = jnp.tile(pe.reshape(2 * CMP_RATIO * CMP_STRIDE, HEAD_DIM), (1, N_KV)).astype(F32)
        wpool = w_pool[l].astype(BF16)
        pscale = pool_scale[l].reshape(1, POOL_WIDTH)
        wout = w_out[l].astype(BF16)
        wffn_in = w_ffn_in[l].astype(BF16)
        wffn_out = w_ffn_out[l].astype(BF16)
        last = l == depth - 1

        q, u, gates, kb, vst, vwt, kvct_all, kvst_all, kvwt = _inproj_seq(l, xp, g_mix, w_all, kvct_all, kvst_all)
        ck, cvt = _compress_prompt(l, kvct_all, w_bd, pe_rows)
        mix = _prompt_attention(q.reshape(bp, t, ATT_WIDTH), gates.reshape(bp, t, LANE), u.reshape(bp, t, POOL_WIDTH),
                                pool_zero, kb.reshape(bp, t, KV_COLS), vst, vwt, ck, cvt, delta_p, mimp_p, eblk_p,
                                wpool, pscale)
        xp = _ffn(xp, mix.reshape(bp * t, D_MODEL), wout, g_ffn, wffn_in, wffn_out, g_fin, last)
        outs["p_win"].append(kvwt[:, :, t - wb:])
        outs["p_pool"].append(u.reshape(bp, t, POOL_WIDTH)[:, t - POOL_STATE:])

        q, u, gates, kvc, kvs, kvw = _inproj_rows(xs, g_mix, w_all)
        mix, nwin_all, npool = _sample_attention(
            l, page_table, q.reshape(bs, tq, ATT_WIDTH), gates.reshape(bs, tq, LANE), u.reshape(bs, tq, POOL_WIDTH),
            kvs.reshape(bs, tq, KV_COLS), kvw.reshape(bs, tq, KV_COLS), swin_t, state_pool, ccache_t, scache_t,
            nwin_all, w_bd, pe_rows, delta_s, mimp_s, expand, wpool, pscale)
        xs = _ffn(xs, mix.reshape(bs * tq, D_MODEL), wout, g_ffn, wffn_in, wffn_out, g_fin, last)
        outs["s_cmp"].append(kvc.reshape(kv_shape(bs, tq)))
        outs["s_sel"].append(kvs.reshape(kv_shape(bs, tq)))
        outs["s_pool"].append(npool)

    return (xp.reshape(bp, t, D_MODEL), xs.reshape(bs, tq, D_MODEL),
            token_major(kvct_all), token_major(kvst_all), token_major(jnp.stack(outs["p_win"])),
            jnp.stack(outs["p_pool"]), jnp.stack(outs["s_cmp"]), jnp.stack(outs["s_sel"]), token_major(nwin_all),
            jnp.stack(outs["s_pool"]))
```

```python
import functools
import math

import numpy as np
import jax
import jax.numpy as jnp
from jax import lax
from jax.experimental import pallas as pl
from jax.experimental.pallas import tpu as pltpu

D_MODEL = 1024
HEAD_DIM = 64
N_HEADS = 8
N_KV = 2
GROUP = N_HEADS // N_KV
ATT_WIDTH = N_HEADS * HEAD_DIM
POOL_WIDTH = D_MODEL - ATT_WIDTH
KV_COLS = 2 * N_KV * HEAD_DIM
KV_LANES = N_KV * HEAD_DIM
VT_ROWS = KV_LANES + 16
CMP_LEN = 32
CMP_STRIDE = 16
CMP_RATIO = CMP_LEN // CMP_STRIDE
SEL_BLOCK = 64
N_SELECT = 16
WINDOW = 512
N_BRANCH = 3
GATE_COLS = N_HEADS * N_BRANCH
POOL_WINDOWS = (2, 4, 8, 16)
POOL_GROUP_DIM = POOL_WIDTH // len(POOL_WINDOWS)
POOL_STATE = max(POOL_WINDOWS) - 1
POOL_HALO = 16
N_BUCKETS = 32
MAX_DISTANCE = 128
PAGE_SIZE = 128
EPS = 1e-6
NEG = -1e30
VALID_MIN = -1e29

LANE = 128
TQ = 128
PROMPT_QBLOCKS = 2
FAR_TILE = 512
CHAIN_COLS = 256
LOG2E = 1.4426950408889634
ROW_TILE = 512
FFN_SLICE = 512
VMEM_LIMIT = 52 * 1024 * 1024

F32 = jnp.float32
BF16 = jnp.bfloat16
NT_DIMS = (((1,), (1,)), ((), ()))


def _bucket_np(dist):
    n = np.maximum(dist, 0)
    max_exact = N_BUCKETS // 2
    nf = np.maximum(n, 1).astype(np.float32)
    large = max_exact + (np.log(nf / max_exact) / math.log(MAX_DISTANCE / max_exact)
                         * (N_BUCKETS - max_exact)).astype(np.int32)
    return np.where(n < max_exact, n, np.minimum(large, N_BUCKETS - 1)).astype(np.int32)


def _bucket_or_masked(dist, valid):
    return np.where(valid, _bucket_np(dist), -1).astype(np.int32)


FAR_DIST = TQ + 1


def _check_far_bucket(max_dist):
    assert (_bucket_np(np.arange(FAR_DIST, max_dist + 1)) == N_BUCKETS - 1).all()


def _bias_tile_kernel(bucket_ref, rb_ref, out_ref):
    bucket = bucket_ref[...]
    acc = jnp.zeros(bucket.shape, F32)
    for b in range(N_BUCKETS):
        acc = acc + jnp.where(bucket == b, rb_ref[b:b + 1, :], 0.0)
    acc = (acc - rb_ref[N_BUCKETS - 1:N_BUCKETS, :]) * LOG2E
    out_ref[...] = jnp.where(bucket < 0, NEG, acc)


def _bias_tiles(bucket_np, rb_cols):
    rows, cols = bucket_np.shape
    return pl.pallas_call(
        _bias_tile_kernel,
        out_shape=jax.ShapeDtypeStruct((rows, cols), F32),
    )(jnp.asarray(bucket_np), rb_cols)


DD0 = 0
DC0 = 2 * TQ
DC_ROWS = 24
DW0 = DC0 + 32


def _prompt_bucket_table():
    i = np.arange(TQ)[None, :]
    kk = np.arange(2 * TQ)[:, None]
    d = i + TQ - kk
    dd = _bucket_or_masked(d, d >= 0)
    cc = np.arange(32)[:, None] - 16
    d = i - CMP_STRIDE * cc - (CMP_LEN - 1)
    dc = _bucket_or_masked(d, (d >= 0) & (cc < 8))
    j = np.arange(TQ)[:, None]
    d = i + WINDOW - j
    dw = _bucket_or_masked(d, (d >= 0) & (d < WINDOW))
    tab = np.concatenate([dd, dc, dw], axis=0)
    return np.tile(tab, (1, N_HEADS))


def _sample_bucket_table(past, tq, wb):
    col = np.arange(LANE)
    t = (col % tq)[None, :]
    colok = (col < N_HEADS * tq)[None, :]
    kk = np.arange(LANE)[:, None]
    d = LANE + t - kk
    dlast = _bucket_or_masked(d, colok & (d >= 0))
    d = t - kk
    dnew = _bucket_or_masked(d, colok & (d >= 0) & (kk < tq))
    d = wb + t - kk
    dedge = _bucket_or_masked(d, colok & (d >= 0) & (d < WINDOW))
    nsub = past // CMP_STRIDE
    c = np.arange(nsub)[:, None]
    d = past + t - CMP_STRIDE * c - (CMP_LEN - 1)
    dcmp = _bucket_or_masked(d, colok & (d >= 0) & (c < nsub - CMP_RATIO + 1))
    return np.concatenate([dlast, dnew, dedge, dcmp], axis=0)


IN_Q = 0
IN_KV = ATT_WIDTH
IN_U = IN_KV + N_BRANCH * KV_COLS
IN_G = IN_U + POOL_WIDTH
IN_COLS_PAD = IN_G + LANE


def _rms(x, g):
    return x * lax.rsqrt(jnp.mean(x * x, axis=-1, keepdims=True) + EPS) * g


def _sigmoid(x):
    return 1.0 / (1.0 + jnp.exp(-x))


def _inproj_common(x_ref, g_ref, w_ref, q_ref, u_ref, gate_ref):
    h = _rms(x_ref[...], g_ref[...]).astype(BF16)
    proj = lambda a, b: jnp.dot(h, w_ref[:, a:b], preferred_element_type=F32)
    q_ref[...] = proj(IN_Q, IN_KV).astype(BF16)
    kv = proj(IN_KV, IN_U)
    u_ref[...] = proj(IN_U, IN_G)
    gate_ref[...] = _sigmoid(proj(IN_G, IN_COLS_PAD))
    return [kv[:, i * KV_COLS:(i + 1) * KV_COLS] for i in range(N_BRANCH)]


def _inproj_rows_kernel(x_ref, g_ref, w_ref, q_ref, u_ref, gate_ref, kvc_ref, kvs_ref, kvw_ref):
    kvc, kvs, kvw = _inproj_common(x_ref, g_ref, w_ref, q_ref, u_ref, gate_ref)
    kvc_ref[...] = kvc
    kvs_ref[...] = kvs
    kvw_ref[...] = kvw


def _inproj_seq_kernel(x_ref, g_ref, w_ref, *refs):
    q_ref, u_ref, gate_ref, kb_ref, vst_ref, vwt_ref, kvct_ref, kvst_ref, kvwt_ref = refs[-9:]
    kvc, kvs, kvw = _inproj_common(x_ref, g_ref, w_ref, q_ref, u_ref, gate_ref)
    kb_ref[...] = jnp.concatenate([kvs[:, 0:KV_LANES], kvw[:, 0:KV_LANES]], axis=1).astype(BF16)
    kvst = kvs.T
    kvwt = kvw.T
    kvct_ref[0, 0] = kvc.T
    kvst_ref[0, 0] = kvst
    kvwt_ref[0] = kvwt
    ones = jnp.ones((VT_ROWS - KV_LANES, kvst.shape[1]), F32)
    vst_ref[0] = jnp.concatenate([kvst[KV_LANES:KV_COLS], ones], axis=0).astype(BF16)
    vwt_ref[0] = jnp.concatenate([kvwt[KV_LANES:KV_COLS], ones], axis=0).astype(BF16)


def _inproj_rows(x, g, w):
    n = x.shape[0]
    tm = min(ROW_TILE, n)
    row = lambda c: pl.BlockSpec((tm, c), lambda i: (i, 0))
    full = lambda a: pl.BlockSpec(a.shape, lambda i: (0,) * a.ndim)
    outs = [(ATT_WIDTH, BF16), (POOL_WIDTH, F32), (LANE, F32), (KV_COLS, F32), (KV_COLS, F32), (KV_COLS, F32)]
    return pl.pallas_call(
        _inproj_rows_kernel,
        grid=(n // tm,),
        in_specs=[row(D_MODEL), full(g), full(w)],
        out_specs=[row(c) for c, _ in outs],
        out_shape=[jax.ShapeDtypeStruct((n, c), dt) for c, dt in outs],
        compiler_params=pltpu.CompilerParams(dimension_semantics=("parallel",), vmem_limit_bytes=VMEM_LIMIT),
    )(x, g, w)


def _inproj_seq(layer, x, g, w, kvct_all, kvst_all, wb):
    n = x.shape[0]
    all_shape = kvct_all.shape
    _, b, _, t = all_shape
    tm = min(ROW_TILE, t)
    tiles = t // tm
    assert wb % tm == 0 and wb <= t
    dropped = tiles - wb // tm
    row = lambda c: pl.BlockSpec((tm, c), lambda i: (i, 0))
    full = lambda a: pl.BlockSpec(a.shape, lambda i: (0,) * a.ndim)
    hbm = pl.BlockSpec(memory_space=pl.ANY)
    seq_t = lambda r: pl.BlockSpec((1, r, tm), lambda i: (i // tiles, 0, i % tiles))
    all_t = pl.BlockSpec((1, 1, KV_COLS, tm), lambda i: (layer, i // tiles, 0, i % tiles))
    tail_t = pl.BlockSpec((1, KV_COLS, tm), lambda i: (i // tiles, 0, jnp.maximum(i % tiles - dropped, 0)))
    rows = [(ATT_WIDTH, BF16), (POOL_WIDTH, F32), (LANE, F32), (KV_COLS, BF16)]
    carried = [kvct_all, kvst_all]
    first_all = len(rows) + 2
    return pl.pallas_call(
        _inproj_seq_kernel,
        grid=(n // tm,),
        in_specs=[row(D_MODEL), full(g), full(w)] + [hbm] * len(carried),
        out_specs=[row(c) for c, _ in rows] + [seq_t(VT_ROWS), seq_t(VT_ROWS), all_t, all_t, tail_t],
        out_shape=[jax.ShapeDtypeStruct((n, c), dt) for c, dt in rows]
        + [jax.ShapeDtypeStruct((b, VT_ROWS, t), BF16), jax.ShapeDtypeStruct((b, VT_ROWS, t), BF16),
           jax.ShapeDtypeStruct(all_shape, F32), jax.ShapeDtypeStruct(all_shape, F32),
           jax.ShapeDtypeStruct((b, KV_COLS, wb), F32)],
        input_output_aliases={3 + k: first_all + k for k in range(len(carried))},
        compiler_params=pltpu.CompilerParams(dimension_semantics=("arbitrary",), vmem_limit_bytes=VMEM_LIMIT),
    )(x, g, w, *carried)


def _compress_tokens(x_of, w_ref, pe_ref, n_sub):
    outs = []
    for j in range(2):
        parts = [jnp.zeros((n_sub, KV_LANES), F32) for _ in range(CMP_RATIO)]
        for l in range(CMP_STRIDE):
            x = x_of(l, j)
            for r in range(CMP_RATIO):
                row = (j * CMP_RATIO + r) * CMP_STRIDE + l
                a = (x + pe_ref[row:row + 1, :]).astype(BF16)
                parts[r] = parts[r] + jnp.dot(a, w_ref[j, r, l], preferred_element_type=F32)
        comp = parts[0]
        for r in range(1, CMP_RATIO):
            comp = comp + pltpu.roll(parts[r], n_sub - r, axis=0)
        outs.append(comp)
    return outs


def _regroup_tokens(feat_tok, x_s, first_sub):
    pitch = x_s.shape[0] // CMP_STRIDE
    xt = feat_tok.T
    for r in range(xt.shape[0] // 8):
        l0 = (8 * r) % CMP_STRIDE
        sub = first_sub + (8 * r) // CMP_STRIDE
        x_s[pl.ds(l0 * pitch + sub, 8, stride=pitch), :] = xt[8 * r:8 * r + 8, :]


def _regrouped_rows(x_refs, n_sub):
    def x_of(l, j):
        pitch = x_refs[j].shape[0] // CMP_STRIDE
        return x_refs[j][l * pitch:l * pitch + n_sub, :]
    return x_of


def _compress_kernel(xt_ref, w_ref, pe_ref, ck_ref, cvt_ref, xk_s, xv_s):
    t = xt_ref.shape[3]
    n_sub = t // CMP_STRIDE
    for p in range(t // LANE):
        tok = slice(p * LANE, (p + 1) * LANE)
        _regroup_tokens(xt_ref[0, 0, 0:KV_LANES, tok], xk_s, p * (LANE // CMP_STRIDE))
        _regroup_tokens(xt_ref[0, 0, KV_LANES:KV_COLS, tok], xv_s, p * (LANE // CMP_STRIDE))
    ck, cv = _compress_tokens(_regrouped_rows((xk_s, xv_s), n_sub), w_ref, pe_ref, n_sub)
    ck_ref[0] = ck.astype(BF16)
    cvt_ref[0] = cv.T.astype(BF16)


def _compress_prompt(layer, kvct_all, w_bd, pe_rows):
    _, b, _, t = kvct_all.shape
    n_sub = t // CMP_STRIDE
    full = lambda a: pl.BlockSpec(a.shape, lambda i: (0,) * a.ndim)
    slabs = pltpu.VMEM((CMP_STRIDE * (n_sub + 8), KV_LANES), F32)
    return pl.pallas_call(
        _compress_kernel,
        grid=(b,),
        in_specs=[pl.BlockSpec((1, 1, KV_COLS, t), lambda i: (layer, i, 0, 0)), full(w_bd), full(pe_rows)],
        out_specs=[pl.BlockSpec((1, n_sub, KV_LANES), lambda i: (i, 0, 0)),
                   pl.BlockSpec((1, KV_LANES, n_sub), lambda i: (i, 0, 0))],
        out_shape=[jax.ShapeDtypeStruct((b, n_sub, KV_LANES), BF16), jax.ShapeDtypeStruct((b, KV_LANES, n_sub), BF16)],
        scratch_shapes=[slabs, slabs],
        compiler_params=pltpu.CompilerParams(dimension_semantics=("parallel",), vmem_limit_bytes=VMEM_LIMIT),
    )(kvct_all, w_bd, pe_rows)


def _split_dot(m, p):
    hi = p.astype(BF16)
    lo = (p - hi.astype(F32)).astype(BF16)
    return jnp.dot(m, hi, preferred_element_type=F32) + jnp.dot(m, lo, preferred_element_type=F32)


def _col_max(s):
    rows = s.shape[0]
    parts = [s[r:r + 64] for r in range(0, rows, 64)] if rows % 64 == 0 and rows > 64 else [s]
    while len(parts) > 1:
        parts = [jnp.maximum(a, b) for a, b in zip(parts[0::2], parts[1::2])] + ([parts[-1]] if len(parts) % 2 else [])
    return jnp.max(parts[0], axis=0, keepdims=True)


def _pool_mix(ext_ref, u, pos, wpool_ref, pscale_ref, rows):
    outs = []
    for k, w in enumerate(POOL_WINDOWS):
        ln = slice(k * POOL_GROUP_DIM, (k + 1) * POOL_GROUP_DIM)
        ws = ext_ref[POOL_HALO:POOL_HALO + rows, ln]
        for s in range(1, w):
            ws = ws + ext_ref[POOL_HALO - s:POOL_HALO - s + rows, ln]
        cnt = jnp.minimum(pos + 1, w).astype(F32)
        pooled = (ws / cnt - u[:, ln]).astype(BF16)
        o = jnp.dot(pooled, wpool_ref[k], preferred_element_type=F32) * pscale_ref[:, ln]
        outs.append(o.astype(BF16))
    return outs


def _prompt_attn_kernel(q_ref, gate_ref, u_ref, uh_ref, pst_ref, ksel_ref, kwin_ref, vst_ref, vwt_ref,
                        ck_ref, cvt_ref, delta_ref, mimp_ref, eblk_ref, wpool_ref, pscale_ref,
                        mix_ref,
                        sc_s, qa_s, qf_s, acc_s, m_s, ext_s):
    step = pl.program_id(1)
    n_blk = q_ref.shape[1] // TQ
    t_len = ksel_ref.shape[1]
    n_cmp = ck_ref.shape[1]
    n_sel = t_len // SEL_BLOCK
    cols = N_HEADS * TQ
    blk_per_q = TQ // SEL_BLOCK
    n_chain = cols // CHAIN_COLS
    chains = [slice(c * CHAIN_COLS, (c + 1) * CHAIN_COLS) for c in range(n_chain)]
    n_wc = WINDOW // TQ + 1
    w_delta = {0: DW0, n_wc - 2: DD0, n_wc - 1: DD0 + TQ}

    val = [dict() for _ in range(n_blk)]
    qb_of = lambda a: n_blk * step + a
    tok = lambda a: slice(a * TQ, (a + 1) * TQ)

    def stage_compressed(a):
        v, qb = val[a], qb_of(a)
        q = q_ref[0, tok(a), :].astype(F32)
        zeros = jnp.zeros((TQ, HEAD_DIM), F32)
        blocks = []
        for h in range(N_HEADS):
            halves = [zeros] * N_KV
            halves[h // GROUP] = q[:, h * HEAD_DIM:(h + 1) * HEAD_DIM]
            blocks.append(jnp.concatenate(halves, axis=1))
        qbd = jnp.concatenate(blocks, axis=0).astype(BF16)
        v["qbd"] = qbd
        sc = lax.dot_general(ck_ref[0], qbd, NT_DIMS, preferred_element_type=F32)
        crow = lax.broadcasted_iota(jnp.int32, (n_cmp, cols), 0)
        sc = jnp.where(crow >= 8 * qb + 8, NEG, sc)
        sc_s[a, 0:16, :] = jnp.zeros((16, cols), F32)
        sc_s[a, 16 + n_cmp:16 + n_cmp + 8, :] = jnp.zeros((8, cols), F32)
        sc_s[a, 16:16 + n_cmp, :] = sc
        near = pl.ds(pl.multiple_of(8 * qb, 8), DC_ROWS)
        sc_s[a, near, :] = sc_s[a, near, :] + delta_ref[DC0:DC0 + DC_ROWS, :]
        sc = sc_s[a, 16:16 + n_cmp, :]
        m_c = jnp.max(sc, axis=0, keepdims=True)
        p = jnp.exp2(sc - m_c)
        l_c = jnp.sum(p, axis=0, keepdims=True)
        pn = p * jnp.where(m_c > VALID_MIN, 1.0 / l_c, 0.0)
        v["o_cmp"] = jnp.dot(cvt_ref[0], pn.astype(BF16), preferred_element_type=F32)
        v["imp"] = []
        for g in range(N_KV):
            ps = pn[:, (g * GROUP) * TQ:(g * GROUP + 1) * TQ]
            for r in range(1, GROUP):
                ps = ps + pn[:, (g * GROUP + r) * TQ:(g * GROUP + r + 1) * TQ]
            v["imp"].append(_split_dot(mimp_ref[...], ps))

    def stage_window_scores(a):
        v, qb = val[a], qb_of(a)
        w_pos0 = [(qb - (n_wc - 1) + c) * TQ for c in range(n_wc)]
        w_k0 = [pl.multiple_of(jnp.maximum(p0, 0), TQ) for p0 in w_pos0]
        w_keys = [kwin_ref[0, pl.ds(k0, TQ), :] for k0 in w_k0]
        v["w_vt"] = [vwt_ref[0, :, pl.ds(k0, TQ)] for k0 in w_k0]
        v["s_win"] = []
        for cs in chains:
            s_w = []
            for c in range(n_wc):
                s = lax.dot_general(w_keys[c], v["qbd"][cs], NT_DIMS, preferred_element_type=F32)
                if c in w_delta:
                    s = s + delta_ref[w_delta[c]:w_delta[c] + TQ, cs]
                if c < n_wc - 1:
                    s = s + jnp.where(w_pos0[c] < 0, NEG, 0.0)
                s_w.append(s)
            v["s_win"].append(s_w)

    jidx = lax.broadcasted_iota(jnp.int32, (n_sel, TQ), 0)
    qi = lax.broadcasted_iota(jnp.int32, (n_sel, TQ), 1)
    pad = jnp.zeros((LANE - n_sel, TQ), F32)

    def stage_select(a):
        v, qb = val[a], qb_of(a)
        cur = blk_per_q * qb + qi // SEL_BLOCK
        forced = (jidx == 0) | (jidx == cur) | (jidx == cur - 1)
        prev_blk = blk_per_q * jnp.maximum(qb - 1, 0)
        ns_t = []
        for imp in v["imp"]:
            score = jnp.where(forced, jnp.inf, imp)
            score = jnp.where(jidx > cur, -jnp.inf, score)
            cnt = jnp.zeros((n_sel, TQ), jnp.int32)
            for jp in range(n_sel):
                row = score[jp:jp + 1, :]
                cnt = cnt + jnp.where(jidx > jp, jnp.where(row >= score, 1, 0), jnp.where(row > score, 1, 0))
            sel = (cnt < min(N_SELECT, n_sel)) & (jidx <= cur)
            ns_all = jnp.where(sel, 0.0, NEG)
            ns_far = jnp.where(jidx < prev_blk, ns_all, NEG)
            ns_t.append([jnp.concatenate([x, pad], axis=0).T.astype(BF16) for x in (ns_all, ns_far)])
        for h in range(N_HEADS):
            rows_h = slice(h * TQ, (h + 1) * TQ)
            qa_s[a, rows_h, 0:LANE] = v["qbd"][rows_h]
            qf_s[a, rows_h, 0:LANE] = v["qbd"][rows_h]
            qa_s[a, rows_h, LANE:2 * LANE] = ns_t[h // GROUP][0]
            qf_s[a, rows_h, LANE:2 * LANE] = ns_t[h // GROUP][1]
        m_s[a] = jnp.full((1, cols), NEG, F32)
        acc_s[a] = jnp.zeros((VT_ROWS, cols), F32)

    def online_update(a, cs, s, vt):
        m_old = m_s[a, :, cs]
        m_new = jnp.maximum(m_old, _col_max(s))
        alpha = jnp.exp2(m_old - m_new)
        p = jnp.exp2(s - m_new)
        acc_s[a, :, cs] = alpha * acc_s[a, :, cs] + jnp.dot(vt, p.astype(BF16), preferred_element_type=F32)
        m_s[a, :, cs] = m_new

    def keys_with_block(k0, n):
        return jnp.concatenate([ksel_ref[0, pl.ds(k0, n), :], eblk_ref[pl.ds(k0, n), :]], axis=1)

    def masked_scores(kcat, q_s, a, cs):
        return lax.dot_general(kcat, q_s[a, cs, :], NT_DIMS, preferred_element_type=F32)

    n_far = (TQ * jnp.maximum(qb_of(n_blk - 1) - 1, 0) + FAR_TILE - 1) // FAR_TILE

    def far_tiles(t0, n):
        k0 = [pl.multiple_of((t0 + i) * FAR_TILE, FAR_TILE) for i in range(n)]
        s_tiles = []
        for i in range(n):
            kcat = keys_with_block(k0[i], FAR_TILE)
            s_tiles.append([[masked_scores(kcat, qf_s, a, cs) for cs in chains] for a in range(n_blk)])
        for i in range(n):
            vt = vst_ref[0, :, pl.ds(k0[i], FAR_TILE)]
            for a in range(n_blk):
                for cs, s in zip(chains, s_tiles[i][a]):
                    online_update(a, cs, s, vt)

    def stage_near_scores(a):
        v, qb = val[a], qb_of(a)
        k_prev = pl.multiple_of(TQ * jnp.maximum(qb - 1, 0), TQ)
        k_diag = pl.multiple_of(qb * TQ, TQ)
        no_prev = jnp.where(qb == 0, NEG, 0.0)
        kcat_prev = keys_with_block(k_prev, TQ)
        kcat_diag = keys_with_block(k_diag, TQ)
        v["vt_near"] = jnp.concatenate([vst_ref[0, :, pl.ds(k_prev, TQ)], vst_ref[0, :, pl.ds(k_diag, TQ)]], axis=1)
        v["s_near"] = []
        for cs in chains:
            s_prev = masked_scores(kcat_prev, qa_s, a, cs) + delta_ref[DD0:DD0 + TQ, cs] + no_prev
            s_diag = masked_scores(kcat_diag, qa_s, a, cs) + delta_ref[DD0 + TQ:DD0 + 2 * TQ, cs]
            v["s_near"].append(jnp.concatenate([s_prev, s_diag], axis=0))

    def stage_near_update(a):
        v = val[a]
        for cs, s in zip(chains, v["s_near"]):
            online_update(a, cs, s, v["vt_near"])

    def stage_window(a):
        v = val[a]
        o_w = []
        for s_w in v["s_win"]:
            m_w = s_w[0].max(axis=0, keepdims=True)
            for s in s_w[1:]:
                m_w = jnp.maximum(m_w, s.max(axis=0, keepdims=True))
            o_c = jnp.zeros((VT_ROWS, CHAIN_COLS), F32)
            for s, vt in zip(s_w, v["w_vt"]):
                o_c = o_c + jnp.dot(vt, jnp.exp2(s - m_w).astype(BF16), preferred_element_type=F32)
            o_w.append(o_c)
        v["o_win"] = jnp.concatenate(o_w, axis=1)

    def stage_output(a):
        v, qb = val[a], qb_of(a)
        l_sel = acc_s[a, KV_LANES:KV_LANES + 1, :]
        o_sel = acc_s[a, 0:KV_LANES, :]
        l_win = v["o_win"][KV_LANES:KV_LANES + 1, :]
        o_win = v["o_win"][0:KV_LANES, :]
        gt = gate_ref[0, tok(a), :].T
        gate_row = lambda x: jnp.concatenate([gt[N_BRANCH * h + x:N_BRANCH * h + x + 1, :] for h in range(N_HEADS)],
                                             axis=1)
        o_t = v["o_cmp"] * gate_row(0) + o_sel * (gate_row(1) / l_sel) + o_win * (gate_row(2) / l_win)
        pieces = []
        for h in range(N_HEADS):
            g = h // GROUP
            pieces.append(o_t[g * HEAD_DIM:(g + 1) * HEAD_DIM, h * TQ:(h + 1) * TQ])
        mix_ref[0, tok(a), 0:ATT_WIDTH] = jnp.concatenate(pieces, axis=0).T.astype(BF16)
        u = u_ref[0, tok(a), :]
        ext = ext_s.at[a]
        if a == 0:
            ext[0:POOL_HALO, :] = jnp.where(qb == 0, pst_ref[0], uh_ref[0])
        else:
            ext[0:POOL_HALO, :] = u_ref[0, a * TQ - POOL_HALO:a * TQ, :]
        ext[POOL_HALO:POOL_HALO + TQ, :] = u
        pos = qb * TQ + lax.broadcasted_iota(jnp.int32, (TQ, 1), 0)
        for k, o in enumerate(_pool_mix(ext, u, pos, wpool_ref, pscale_ref, TQ)):
            mix_ref[0, tok(a), ATT_WIDTH + k * POOL_GROUP_DIM:ATT_WIDTH + (k + 1) * POOL_GROUP_DIM] = o

    def run(stage):
        for a in range(n_blk):
            stage(a)

    run(stage_compressed)
    run(stage_window_scores)
    run(stage_select)

    odd = lax.rem(n_far, 2)

    @pl.when(odd == 1)
    def _():
        far_tiles(0, 1)

    def pair_body(i, carry):
        far_tiles(odd + 2 * i, 2)
        return carry

    lax.fori_loop(0, lax.div(n_far, 2), pair_body, 0)

    run(stage_near_scores)
    run(stage_near_update)
    run(stage_window)
    run(stage_output)


def _prompt_attention(q, gates, u, pool_prev, kb, vst, vwt, ck, cvt, delta, mimp, eblk, wpool, pscale):
    b, t, _ = q.shape
    assert t % FAR_TILE == 0 and t >= WINDOW + TQ and t // SEL_BLOCK <= LANE
    n_cmp = ck.shape[1]
    cols = N_HEADS * TQ
    nq = PROMPT_QBLOCKS if (t // TQ) % PROMPT_QBLOCKS == 0 else 1
    rows = nq * TQ
    halo_per_step = rows // POOL_HALO
    full = lambda a: pl.BlockSpec(a.shape, lambda i, j: (0,) * a.ndim)
    qblk = lambda c: pl.BlockSpec((1, rows, c), lambda i, j: (i, j, 0))
    seq = lambda c: pl.BlockSpec((1, t, KV_LANES), lambda i, j: (i, 0, c))
    per_b = lambda a: pl.BlockSpec((1,) + a.shape[1:], lambda i, j: (i,) + (0,) * (a.ndim - 1))
    return pl.pallas_call(
        _prompt_attn_kernel,
        grid=(b, t // rows),
        in_specs=[qblk(ATT_WIDTH), qblk(LANE), qblk(POOL_WIDTH),
                  pl.BlockSpec((1, POOL_HALO, POOL_WIDTH), lambda i, j: (i, jnp.maximum(j * halo_per_step - 1, 0), 0)),
                  per_b(pool_prev), seq(0), seq(1), per_b(vst), per_b(vwt), per_b(ck), per_b(cvt),
                  full(delta), full(mimp), full(eblk), full(wpool), full(pscale)],
        out_specs=pl.BlockSpec((1, rows, D_MODEL), lambda i, j: (i, j, 0)),
        out_shape=jax.ShapeDtypeStruct((b, t, D_MODEL), BF16),
        scratch_shapes=[pltpu.VMEM((nq, 16 + n_cmp + 8, cols), F32),
                        pltpu.VMEM((nq, cols, 2 * LANE), BF16), pltpu.VMEM((nq, cols, 2 * LANE), BF16),
                        pltpu.VMEM((nq, VT_ROWS, cols), F32), pltpu.VMEM((nq, 1, cols), F32),
                        pltpu.VMEM((nq, POOL_HALO + TQ, POOL_WIDTH), F32)],
        compiler_params=pltpu.CompilerParams(dimension_semantics=("parallel", "arbitrary"),
                                             vmem_limit_bytes=VMEM_LIMIT),
    )(q, gates, u, u, pool_prev, kb, kb, vst, vwt, ck, cvt, delta, mimp, eblk, wpool, pscale)


SAMPLE_SEQS = 2
SD_LAST = 0
SD_NEW = LANE
SD_EDGE = 2 * LANE
SD_CMP = 3 * LANE


def _sample_attn_kernel(pt_ref, q_ref, gate_ref, u_ref, kvs_ref, kvw_ref, swin_ref, spool_ref,
                        ccache_ref, scache_ref, *refs, layer):
    (wcmp_ref, pe_ref, delta_ref, mimp_ref, expand_ref, wpool_ref, pscale_ref,
     mix_ref, nwin_ref, npool_ref, cbuf, sbuf, xk_s, xv_s, sem, ext_s) = refs[-16:]
    b = pl.program_id(0)
    nb = pl.num_programs(0)
    n_pages = pt_ref.shape[1]
    past = n_pages * PAGE_SIZE
    n_sub = past // CMP_STRIDE
    n_seq, tq = q_ref.shape[0], q_ref.shape[1]
    wb = swin_ref.shape[3]
    n_sel = past // SEL_BLOCK + 1
    sub_per_page = PAGE_SIZE // CMP_STRIDE
    slot = lax.rem(b, 2)

    def page_copies(step, slt):
        cps = []
        for s in range(n_seq):
            for p in range(n_pages):
                pg = pt_ref[step * n_seq + s, p]
                dst = (slice(None), pl.ds(p * PAGE_SIZE, PAGE_SIZE))
                cps.append(pltpu.make_async_copy(ccache_ref.at[layer, pg], cbuf.at[slt, s].at[dst], sem.at[0, slt]))
                cps.append(pltpu.make_async_copy(scache_ref.at[layer, pg], sbuf.at[slt, s].at[dst], sem.at[1, slt]))
        return cps

    @pl.when(b == 0)
    def _():
        for cp in page_copies(0, 0):
            cp.start()

    @pl.when(b + 1 < nb)
    def _():
        for cp in page_copies(b + 1, 1 - slot):
            cp.start()

    for cp in page_copies(b, slot):
        cp.wait()

    val = [dict() for _ in range(n_seq)]

    def scores(qbd, k_rows):
        return lax.dot_general(qbd, k_rows, NT_DIMS, preferred_element_type=F32)

    def scores_t(qbd, k_t):
        return jnp.dot(qbd, k_t.astype(BF16), preferred_element_type=F32)

    def pad_rows(x):
        return jnp.concatenate([x, jnp.zeros((LANE - tq, KV_LANES), F32)], axis=0)

    pad_new = lambda x: pad_rows(x).astype(BF16)
    d_last = delta_ref[:, SD_LAST:SD_LAST + LANE]
    d_new = delta_ref[:, SD_NEW:SD_NEW + LANE]

    def softmax_pv(s_old, vt_old, s_new, v_new):
        m = jnp.maximum(s_old.max(axis=1, keepdims=True), s_new.max(axis=1, keepdims=True))
        p_old = jnp.exp2(s_old - m)
        p_new = jnp.exp2(s_new - m)
        l = jnp.sum(p_old, axis=1, keepdims=True) + jnp.sum(p_new, axis=1, keepdims=True)
        o = (lax.dot_general(p_old.astype(BF16), vt_old.astype(BF16), NT_DIMS, preferred_element_type=F32)
             + jnp.dot(p_new.astype(BF16), v_new, preferred_element_type=F32))
        return o, l

    def stage_scores(s):
        v = val[s]
        q = q_ref[s].astype(F32)
        zeros = jnp.zeros((tq, HEAD_DIM), F32)
        blocks = []
        for h in range(N_HEADS):
            halves = [zeros] * N_KV
            halves[h // GROUP] = q[:, h * HEAD_DIM:(h + 1) * HEAD_DIM]
            blocks.append(jnp.concatenate(halves, axis=1))
        qbd = jnp.concatenate(blocks, axis=0).astype(BF16)
        v["qbd"] = qbd
        kvw_new = kvw_ref[s]
        s_first = scores_t(qbd, swin_ref[0, s, 0:KV_LANES, 0:LANE]) + delta_ref[:, SD_EDGE:SD_EDGE + LANE]
        s_mid = scores_t(qbd, swin_ref[0, s, 0:KV_LANES, LANE:wb - LANE])
        s_lastw = scores_t(qbd, swin_ref[0, s, 0:KV_LANES, wb - LANE:wb]) + d_last
        v["s_wold"] = jnp.concatenate([s_first, s_mid, s_lastw], axis=1)
        v["s_wnew"] = scores(qbd, pad_new(kvw_new[:, 0:KV_LANES])) + d_new
        kvs_new = kvs_ref[s]
        s_far = scores_t(qbd, sbuf[slot, s, 0:KV_LANES, 0:past - LANE])
        s_last = scores_t(qbd, sbuf[slot, s, 0:KV_LANES, past - LANE:past]) + d_last
        v["s_past"] = jnp.concatenate([s_far, s_last], axis=1)
        v["s_new"] = scores(qbd, pad_new(kvs_new[:, 0:KV_LANES])) + d_new

    def stage_regroup(s):
        for p in range(n_pages):
            tok = slice(p * PAGE_SIZE, (p + 1) * PAGE_SIZE)
            _regroup_tokens(cbuf[slot, s, 0:KV_LANES, tok], xk_s.at[s], p * sub_per_page)
            _regroup_tokens(cbuf[slot, s, KV_LANES:KV_COLS, tok], xv_s.at[s], p * sub_per_page)

    def stage_window(s):
        v = val[s]
        kvw_new = kvw_ref[s]
        v["o_win"], v["l_win"] = softmax_pv(v["s_wold"], swin_ref[0, s, KV_LANES:KV_COLS, :],
                                            v["s_wnew"], pad_new(kvw_new[:, KV_LANES:KV_COLS]))
        shifted = pltpu.roll(swin_ref[0, s], wb - tq, axis=1)
        new_t = jnp.concatenate([pad_rows(kvw_new[:, 0:KV_LANES]).T, pad_rows(kvw_new[:, KV_LANES:KV_COLS]).T], axis=0)
        new_t = pltpu.roll(new_t, LANE - tq, axis=1)
        lane = lax.broadcasted_iota(jnp.int32, (KV_COLS, LANE), 1)
        nwin_ref[0, s, :, 0:wb - LANE] = shifted[:, 0:wb - LANE]
        nwin_ref[0, s, :, wb - LANE:wb] = jnp.where(lane >= LANE - tq, new_t, shifted[:, wb - LANE:wb])

    def stage_compress(s):
        v = val[s]
        ck, cv = _compress_tokens(_regrouped_rows((xk_s.at[s], xv_s.at[s]), n_sub), wcmp_ref, pe_ref, n_sub)
        v["cv"] = cv.astype(BF16)
        v["sc"] = scores(v["qbd"], ck.astype(BF16)) + delta_ref[:, SD_CMP:SD_CMP + n_sub]

    def stage_compressed(s):
        v = val[s]
        sc = v["sc"]
        m_c = jnp.max(sc, axis=1, keepdims=True)
        p = jnp.exp2(sc - m_c)
        l_c = jnp.sum(p, axis=1, keepdims=True)
        pn = p * jnp.where(m_c > VALID_MIN, 1.0 / l_c, 0.0)
        v["o_cmp"] = jnp.dot(pn.astype(BF16), v["cv"], preferred_element_type=F32)
        v["imp"] = []
        for g in range(N_KV):
            ps = pn[g * GROUP * tq:(g * GROUP + 1) * tq, :]
            for r in range(1, GROUP):
                ps = ps + pn[(g * GROUP + r) * tq:(g * GROUP + r + 1) * tq, :]
            hi = ps.astype(BF16)
            lo = (ps - hi.astype(F32)).astype(BF16)
            v["imp"].append(jnp.dot(hi, mimp_ref[...], preferred_element_type=F32)
                            + jnp.dot(lo, mimp_ref[...], preferred_element_type=F32))

    jidx = lax.broadcasted_iota(jnp.int32, (tq, LANE), 1)
    cur = (past + lax.broadcasted_iota(jnp.int32, (tq, LANE), 0)) // SEL_BLOCK
    forced = (jidx == 0) | (jidx == cur) | (jidx == cur - 1)

    def stage_select(s):
        v = val[s]
        sel_g = []
        for imp in v["imp"]:
            score = jnp.where(forced, jnp.inf, imp)
            score = jnp.where((jidx > cur) | (jidx >= n_sel), -jnp.inf, score)
            cnt = jnp.zeros((tq, LANE), jnp.int32)
            for jp in range(n_sel):
                col = score[:, jp:jp + 1]
                cnt = cnt + jnp.where(jidx > jp, jnp.where(col >= score, 1, 0), jnp.where(col > score, 1, 0))
            sel = (cnt < min(N_SELECT, n_sel)) & (jidx <= cur) & (jidx < n_sel)
            sel_g.append(jnp.where(sel, 1.0, 0.0))
        sel_rows = jnp.concatenate([sel_g[h // GROUP] for h in range(N_HEADS)], axis=0).astype(BF16)
        v["mask"] = jnp.dot(sel_rows, expand_ref[...], preferred_element_type=F32)

    def stage_selected(s):
        v = val[s]
        s_past = jnp.where(v["mask"] > 0.5, v["s_past"], NEG)
        v["o_sel"], v["l_sel"] = softmax_pv(s_past, sbuf[slot, s, KV_LANES:KV_COLS, :], v["s_new"],
                                            pad_new(kvs_ref[s][:, KV_LANES:KV_COLS]))

    def stage_output(s):
        v = val[s]
        gates = gate_ref[s]
        gate_col = lambda x: jnp.concatenate([gates[:, N_BRANCH * h + x:N_BRANCH * h + x + 1] for h in range(N_HEADS)],
                                             axis=0)
        o = (v["o_cmp"] * gate_col(0) + v["o_sel"] * (gate_col(1) / v["l_sel"])
             + v["o_win"] * (gate_col(2) / v["l_win"]))
        pieces = []
        for h in range(N_HEADS):
            g = h // GROUP
            pieces.append(o[h * tq:(h + 1) * tq, g * HEAD_DIM:(g + 1) * HEAD_DIM])
        mix_ref[s, :, 0:ATT_WIDTH] = jnp.concatenate(pieces, axis=1).astype(BF16)
        u = u_ref[s]
        ext = ext_s.at[s]
        ext[0:1, :] = jnp.zeros((1, POOL_WIDTH), F32)
        ext[1:POOL_HALO, :] = spool_ref[0, s]
        ext[POOL_HALO:POOL_HALO + tq, :] = u
        pos = past + lax.broadcasted_iota(jnp.int32, (tq, 1), 0)
        for k, o_k in enumerate(_pool_mix(ext, u, pos, wpool_ref, pscale_ref, tq)):
            mix_ref[s, :, ATT_WIDTH + k * POOL_GROUP_DIM:ATT_WIDTH + (k + 1) * POOL_GROUP_DIM] = o_k
        npool_ref[s] = ext[POOL_HALO + tq - POOL_STATE:POOL_HALO + tq, :]

    for stage in (stage_scores, stage_regroup, stage_window, stage_compress, stage_compressed, stage_select,
                  stage_selected, stage_output):
        for s in range(n_seq):
            stage(s)


def _sample_attention(layer, page_table, q, gates, u, kvs, kvw, swin_t, state_pool, ccache_t, scache_t, nwin_all,
                      wcmp, pe_rows, delta, mimp, expand, wpool, pscale):
    nb, tq, _ = q.shape
    carried = [nwin_all]
    n_pages = page_table.shape[1]
    past = n_pages * PAGE_SIZE
    n_sub = past // CMP_STRIDE
    wb = swin_t.shape[3]
    assert wb == WINDOW and wb >= 3 * LANE and tq <= CMP_STRIDE and POOL_HALO + tq - POOL_STATE >= 0
    assert past // SEL_BLOCK + 1 <= LANE and past % LANE == 0 and n_sub <= LANE
    ns = SAMPLE_SEQS if nb % SAMPLE_SEQS == 0 else 1
    full = lambda a: pl.BlockSpec(a.shape, lambda i, pt: (0,) * a.ndim)
    per_b = lambda c: pl.BlockSpec((ns, tq, c), lambda i, pt: (i, 0, 0))
    layer_b = lambda a: pl.BlockSpec((1, ns) + a.shape[2:], lambda i, pt: (layer, i) + (0,) * (a.ndim - 2))
    hbm = pl.BlockSpec(memory_space=pl.ANY)
    grid_spec = pltpu.PrefetchScalarGridSpec(
        num_scalar_prefetch=1,
        grid=(nb // ns,),
        in_specs=[per_b(ATT_WIDTH), per_b(LANE), per_b(POOL_WIDTH), per_b(KV_COLS), per_b(KV_COLS),
                  layer_b(swin_t), layer_b(state_pool), hbm, hbm] + [hbm] * len(carried)
        + [full(wcmp), full(pe_rows), full(delta), full(mimp), full(expand), full(wpool), full(pscale)],
        out_specs=[pl.BlockSpec((ns, tq, D_MODEL), lambda i, pt: (i, 0, 0)),
                   layer_b(swin_t),
                   pl.BlockSpec((ns, POOL_STATE, POOL_WIDTH), lambda i, pt: (i, 0, 0))],
        scratch_shapes=[pltpu.VMEM((2, ns, KV_COLS, past), F32), pltpu.VMEM((2, ns, KV_COLS, past), F32),
                        pltpu.VMEM((ns, CMP_STRIDE * (n_sub + 8), KV_LANES), F32),
                        pltpu.VMEM((ns, CMP_STRIDE * (n_sub + 8), KV_LANES), F32),
                        pltpu.SemaphoreType.DMA((2, 2)), pltpu.VMEM((ns, POOL_HALO + tq, POOL_WIDTH), F32)],
    )
    return pl.pallas_call(
        functools.partial(_sample_attn_kernel, layer=layer),
        grid_spec=grid_spec,
        out_shape=[jax.ShapeDtypeStruct((nb, tq, D_MODEL), BF16),
                   jax.ShapeDtypeStruct(swin_t.shape, F32),
                   jax.ShapeDtypeStruct((nb, POOL_STATE, POOL_WIDTH), F32)],
        input_output_aliases={10: 1},
        compiler_params=pltpu.CompilerParams(dimension_semantics=("arbitrary",), vmem_limit_bytes=VMEM_LIMIT),
    )(page_table, q, gates, u, kvs, kvw, swin_t, state_pool, ccache_t, scache_t, *carried,
      wcmp, pe_rows, delta, mimp, expand, wpool, pscale)


def _ffn_kernel(x_ref, mix_ref, wout_ref, gn_ref, win_ref, wd_ref, gf_ref, o_ref, *, final_norm):
    xm = x_ref[...] + jnp.dot(mix_ref[...], wout_ref[...], preferred_element_type=F32)
    h = _rms(xm, gn_ref[...]).astype(BF16)
    d_ff = wd_ref.shape[0]
    cuts = list(range(0, d_ff, FFN_SLICE)) + [d_ff]
    slices = list(zip(cuts[:-1], cuts[1:]))

    def gate_up(a, b):
        return (jnp.dot(h, win_ref[:, a:b], preferred_element_type=F32),
                jnp.dot(h, win_ref[:, d_ff + a:d_ff + b], preferred_element_type=F32))

    y = xm
    pending = gate_up(*slices[0])
    for j, (a, b) in enumerate(slices):
        gate, up = pending
        if j + 1 < len(slices):
            pending = gate_up(*slices[j + 1])
        act = (gate * _sigmoid(gate) * up).astype(BF16)
        y = y + jnp.dot(act, wd_ref[a:b, :], preferred_element_type=F32)
    o_ref[...] = _rms(y, gf_ref[...]) if final_norm else y


def _ffn(x, mix, wout, gn, wffn_in, wffn_out, gf, final_norm):
    n = x.shape[0]
    tm = min(ROW_TILE, n)
    assert n % tm == 0 and wffn_out.shape[0] % LANE == 0
    once = lambda a: pl.BlockSpec(a.shape, lambda i: (0,) * a.ndim, pipeline_mode=pl.Buffered(1))
    row = lambda cdim: pl.BlockSpec((tm, cdim), lambda i: (i, 0))
    return pl.pallas_call(
        functools.partial(_ffn_kernel, final_norm=final_norm),
        grid=(n // tm,),
        in_specs=[row(D_MODEL), row(D_MODEL), once(wout), once(gn), once(wffn_in), once(wffn_out), once(gf)],
        out_specs=row(D_MODEL),
        out_shape=jax.ShapeDtypeStruct((n, D_MODEL), F32),
        compiler_params=pltpu.CompilerParams(dimension_semantics=("parallel",), vmem_limit_bytes=VMEM_LIMIT),
    )(x, mix, wout, gn, wffn_in, wffn_out, gf)


def _importance_matrix(n_sel, n_cmp_rows, n_cmp):
    spb = SEL_BLOCK // CMP_STRIDE
    j = np.arange(n_sel)[:, None]
    c = np.arange(n_cmp_rows)[None, :]
    return ((c >= spb * j - (CMP_RATIO - 1)) & (c <= spb * j + spb - 1) & (c < n_cmp)).astype(np.float32)


def kernel(x_prompt, x_sample, cache_cmp, cache_sel, state_win, state_pool, page_table, rel_bias, norm_mix, norm_ffn,
           norm_final, w_in, w_out, cmp_pos, w_cmp, w_pool, pool_scale, w_ffn_in, w_ffn_out):
    bp, t, _ = x_prompt.shape
    bs, tq, _ = x_sample.shape
    depth = w_in.shape[0]
    n_pages = page_table.shape[1]
    past = n_pages * PAGE_SIZE
    wb = state_win.shape[2]
    _check_far_bucket(max(t, past + tq) + WINDOW)

    rb_prompt = jnp.repeat(rel_bias.astype(F32), TQ, axis=1)
    delta_p = _bias_tiles(_prompt_bucket_table(), rb_prompt)
    rb_sample = jnp.pad(jnp.repeat(rel_bias.astype(F32), tq, axis=1), ((0, 0), (0, LANE - N_HEADS * tq)))
    delta_s = _bias_tiles(_sample_bucket_table(past, tq, wb), rb_sample).T[:N_HEADS * tq]

    n_sub_p = t // CMP_STRIDE
    mimp_p = jnp.asarray(_importance_matrix(t // SEL_BLOCK, n_sub_p, n_sub_p - CMP_RATIO + 1), BF16)
    eblk_p = jnp.asarray((np.arange(t)[:, None] // SEL_BLOCK == np.arange(LANE)[None, :]).astype(np.float32), BF16)
    n_sub_s = past // CMP_STRIDE
    mimp_s = jnp.asarray(_importance_matrix(LANE, n_sub_s, n_sub_s - CMP_RATIO + 1).T, BF16)
    expand = jnp.asarray((np.arange(LANE)[:, None] == np.arange(past)[None, :] // SEL_BLOCK).astype(np.float32), BF16)

    pool_zero = jnp.zeros((bp, POOL_HALO, POOL_WIDTH), F32)
    feature_major = lambda a: jnp.moveaxis(a, 2, -1).reshape(a.shape[0], a.shape[1], KV_COLS, a.shape[2])
    token_major = lambda a: jnp.moveaxis(a.reshape(a.shape[:2] + (2, N_KV, HEAD_DIM, a.shape[3])), -1, 2)
    ccache_t = feature_major(cache_cmp)
    scache_t = feature_major(cache_sel)
    swin_t = feature_major(state_win)
    kvct_all = jnp.zeros((depth, bp, KV_COLS, t), F32)
    kvst_all = jnp.zeros((depth, bp, KV_COLS, t), F32)
    nwin_all = jnp.zeros((depth, bs, KV_COLS, wb), F32)

    xp = x_prompt.reshape(bp * t, D_MODEL)
    xs = x_sample.reshape(bs * tq, D_MODEL)
    outs = {k: [] for k in ("p_win", "p_pool", "s_cmp", "s_sel", "s_pool")}
    kv_shape = lambda b, n: (b, n, 2, N_KV, HEAD_DIM)
    off_g = ATT_WIDTH + N_BRANCH * KV_COLS
    for l in range(depth):
        wl = w_in[l]
        w_all = jnp.concatenate([wl[:, :ATT_WIDTH] * (HEAD_DIM ** -0.5 * LOG2E), wl[:, ATT_WIDTH:off_g],
                                 wl[:, off_g + GATE_COLS:], wl[:, off_g:off_g + GATE_COLS],
                                 jnp.zeros((D_MODEL, LANE - GATE_COLS), F32)], axis=1).astype(BF16)
        g_mix = norm_mix[l].reshape(1, D_MODEL)
        g_ffn = norm_ffn[l].reshape(1, D_MODEL)
        g_fin = norm_final.reshape(1, D_MODEL)
        wc = w_cmp[l].reshape(2, CMP_RATIO, CMP_STRIDE, HEAD_DIM, HEAD_DIM)
        zero = jnp.zeros_like(wc)
        w_bd = jnp.concatenate([jnp.concatenate([wc, zero], axis=-1), jnp.concatenate([zero, wc], axis=-1)],
                               axis=-2).astype(BF16)
        pe = cmp_pos[l].reshape(CMP_RATIO, CMP_STRIDE, 2, HEAD_DIM).transpose(2, 0, 1, 3)
        pe_rows = jnp.tile(pe.reshape(2 * CMP_RATIO * CMP_STRIDE, HEAD_DIM), (1, N_KV)).astype(F32)
        wpool = w_pool[l].astype(BF16)
        pscale = pool_scale[l].reshape(1, POOL_WIDTH)
        wout = w_out[l].astype(BF16)
        wffn_in = w_ffn_in[l].astype(BF16)
        wffn_out = w_ffn_out[l].astype(BF16)
        last = l == depth - 1

        q, u, gates, kb, vst, vwt, kvct_all, kvst_all, kvwt = _inproj_seq(l, xp, g_mix, w_all, kvct_all, kvst_all, wb)
        ck, cvt = _compress_prompt(l, kvct_all, w_bd, pe_rows)
        mix = _prompt_attention(q.reshape(bp, t, ATT_WIDTH), gates.reshape(bp, t, LANE), u.reshape(bp, t, POOL_WIDTH),
                                pool_zero, kb.reshape(bp, t, KV_COLS), vst, vwt, ck, cvt, delta_p, mimp_p, eblk_p,
                                wpool, pscale)
        xp = _ffn(xp, mix.reshape(bp * t, D_MODEL), wout, g_ffn, wffn_in, wffn_out, g_fin, last)
        outs["p_win"].append(kvwt)
        outs["p_pool"].append(u.reshape(bp, t, POOL_WIDTH)[:, t - POOL_STATE:])

        q, u, gates, kvc, kvs, kvw = _inproj_rows(xs, g_mix, w_all)
        mix, nwin_all, npool = _sample_attention(
            l, page_table, q.reshape(bs, tq, ATT_WIDTH), gates.reshape(bs, tq, LANE), u.reshape(bs, tq, POOL_WIDTH),
            kvs.reshape(bs, tq, KV_COLS), kvw.reshape(bs, tq, KV_COLS), swin_t, state_pool, ccache_t, scache_t,
            nwin_all, w_bd, pe_rows, delta_s, mimp_s, expand, wpool, pscale)
        xs = _ffn(xs, mix.reshape(bs * tq, D_MODEL), wout, g_ffn, wffn_in, wffn_out, g_fin, last)
        outs["s_cmp"].append(kvc.reshape(kv_shape(bs, tq)))
        outs["s_sel"].append(kvs.reshape(kv_shape(bs, tq)))
        outs["s_pool"].append(npool)

    return (xp.reshape(bp, t, D_MODEL), xs.reshape(bs, tq, D_MODEL),
            token_major(kvct_all), token_major(kvst_all), token_major(jnp.stack(outs["p_win"])),
            jnp.stack(outs["p_pool"]), jnp.stack(outs["s_cmp"]), jnp.stack(outs["s_sel"]), token_major(nwin_all),
            jnp.stack(outs["s_pool"]))
```

```python
import functools
import math

import numpy as np
import jax
import jax.numpy as jnp
from jax import lax
from jax.experimental import pallas as pl
from jax.experimental.pallas import tpu as pltpu

D_MODEL = 1024
HEAD_DIM = 64
N_HEADS = 8
N_KV = 2
GROUP = N_HEADS // N_KV
ATT_WIDTH = N_HEADS * HEAD_DIM
POOL_WIDTH = D_MODEL - ATT_WIDTH
KV_COLS = 2 * N_KV * HEAD_DIM
KV_LANES = N_KV * HEAD_DIM
VT_ROWS = KV_LANES + 16
CMP_LEN = 32
CMP_STRIDE = 16
CMP_RATIO = CMP_LEN // CMP_STRIDE
SEL_BLOCK = 64
N_SELECT = 16
WINDOW = 512
N_BRANCH = 3
GATE_COLS = N_HEADS * N_BRANCH
POOL_WINDOWS = (2, 4, 8, 16)
POOL_GROUP_DIM = POOL_WIDTH // len(POOL_WINDOWS)
POOL_STATE = max(POOL_WINDOWS) - 1
POOL_HALO = 16
N_BUCKETS = 32
MAX_DISTANCE = 128
PAGE_SIZE = 128
EPS = 1e-6
NEG = -1e30
VALID_MIN = -1e29

LANE = 128
TQ = 128
PROMPT_QBLOCKS = 2
FAR_TILE = 512
CHAIN_COLS = 256
LOG2E = 1.4426950408889634
ROW_TILE = 512
FFN_SLICE = 512
VMEM_LIMIT = 52 * 1024 * 1024

F32 = jnp.float32
BF16 = jnp.bfloat16
NT_DIMS = (((1,), (1,)), ((), ()))


def _bucket_np(dist):
    n = np.maximum(dist, 0)
    max_exact = N_BUCKETS // 2
    nf = np.maximum(n, 1).astype(np.float32)
    large = max_exact + (np.log(nf / max_exact) / math.log(MAX_DISTANCE / max_exact)
                         * (N_BUCKETS - max_exact)).astype(np.int32)
    return np.where(n < max_exact, n, np.minimum(large, N_BUCKETS - 1)).astype(np.int32)


def _bucket_or_masked(dist, valid):
    return np.where(valid, _bucket_np(dist), -1).astype(np.int32)


FAR_DIST = TQ + 1


def _check_far_bucket(max_dist):
    assert (_bucket_np(np.arange(FAR_DIST, max_dist + 1)) == N_BUCKETS - 1).all()


def _bias_tile_kernel(bucket_ref, rb_ref, out_ref):
    bucket = bucket_ref[...]
    acc = jnp.zeros(bucket.shape, F32)
    for b in range(N_BUCKETS):
        acc = acc + jnp.where(bucket == b, rb_ref[b:b + 1, :], 0.0)
    acc = (acc - rb_ref[N_BUCKETS - 1:N_BUCKETS, :]) * LOG2E
    out_ref[...] = jnp.where(bucket < 0, NEG, acc)


def _bias_tiles(bucket_np, rb_cols):
    rows, cols = bucket_np.shape
    return pl.pallas_call(
        _bias_tile_kernel,
        out_shape=jax.ShapeDtypeStruct((rows, cols), F32),
    )(jnp.asarray(bucket_np), rb_cols)


DD0 = 0
DC0 = 2 * TQ
DC_ROWS = 24
DW0 = DC0 + 32


def _prompt_bucket_table():
    i = np.arange(TQ)[None, :]
    kk = np.arange(2 * TQ)[:, None]
    d = i + TQ - kk
    dd = _bucket_or_masked(d, d >= 0)
    cc = np.arange(32)[:, None] - 16
    d = i - CMP_STRIDE * cc - (CMP_LEN - 1)
    dc = _bucket_or_masked(d, (d >= 0) & (cc < 8))
    j = np.arange(TQ)[:, None]
    d = i + WINDOW - j
    dw = _bucket_or_masked(d, (d >= 0) & (d < WINDOW))
    tab = np.concatenate([dd, dc, dw], axis=0)
    return np.tile(tab, (1, N_HEADS))


def _sample_bucket_table(past, tq, wb):
    col = np.arange(LANE)
    t = (col % tq)[None, :]
    colok = (col < N_HEADS * tq)[None, :]
    kk = np.arange(LANE)[:, None]
    d = LANE + t - kk
    dlast = _bucket_or_masked(d, colok & (d >= 0))
    d = t - kk
    dnew = _bucket_or_masked(d, colok & (d >= 0) & (kk < tq))
    d = wb + t - kk
    dedge = _bucket_or_masked(d, colok & (d >= 0) & (d < WINDOW))
    nsub = past // CMP_STRIDE
    c = np.arange(nsub)[:, None]
    d = past + t - CMP_STRIDE * c - (CMP_LEN - 1)
    dcmp = _bucket_or_masked(d, colok & (d >= 0) & (c < nsub - CMP_RATIO + 1))
    return np.concatenate([dlast, dnew, dedge, dcmp], axis=0)


IN_Q = 0
IN_KV = ATT_WIDTH
IN_U = IN_KV + N_BRANCH * KV_COLS
IN_G = IN_U + POOL_WIDTH
IN_COLS_PAD = IN_G + LANE


def _rms(x, g):
    return x * lax.rsqrt(jnp.mean(x * x, axis=-1, keepdims=True) + EPS) * g


def _sigmoid(x):
    return 1.0 / (1.0 + jnp.exp(-x))


def _inproj_common(x_ref, g_ref, w_ref, q_ref, u_ref, gate_ref):
    h = _rms(x_ref[...], g_ref[...]).astype(BF16)
    proj = lambda a, b: jnp.dot(h, w_ref[:, a:b], preferred_element_type=F32)
    q_ref[...] = proj(IN_Q, IN_KV).astype(BF16)
    kv = proj(IN_KV, IN_U)
    u_ref[...] = proj(IN_U, IN_G)
    gate_ref[...] = _sigmoid(proj(IN_G, IN_COLS_PAD))
    return [kv[:, i * KV_COLS:(i + 1) * KV_COLS] for i in range(N_BRANCH)]


def _inproj_rows_kernel(x_ref, g_ref, w_ref, q_ref, u_ref, gate_ref, kvc_ref, kvs_ref, kvw_ref):
    kvc, kvs, kvw = _inproj_common(x_ref, g_ref, w_ref, q_ref, u_ref, gate_ref)
    kvc_ref[...] = kvc
    kvs_ref[...] = kvs
    kvw_ref[...] = kvw


def _inproj_seq_kernel(x_ref, g_ref, w_ref, *refs):
    q_ref, u_ref, gate_ref, kb_ref, vst_ref, vwt_ref, kvct_ref, kvst_ref, kvwt_ref = refs[-9:]
    kvc, kvs, kvw = _inproj_common(x_ref, g_ref, w_ref, q_ref, u_ref, gate_ref)
    kb_ref[...] = jnp.concatenate([kvs[:, 0:KV_LANES], kvw[:, 0:KV_LANES]], axis=1).astype(BF16)
    kvst = kvs.T
    kvwt = kvw.T
    kvct_ref[0, 0] = kvc.T
    kvst_ref[0, 0] = kvst
    for later in range(1, kvct_ref.shape[0]):
        kvct_ref[later, 0] = jnp.zeros(kvst.shape, F32)
        kvst_ref[later, 0] = jnp.zeros(kvst.shape, F32)
    kvwt_ref[0] = kvwt
    ones = jnp.ones((VT_ROWS - KV_LANES, kvst.shape[1]), F32)
    vst_ref[0] = jnp.concatenate([kvst[KV_LANES:KV_COLS], ones], axis=0).astype(BF16)
    vwt_ref[0] = jnp.concatenate([kvwt[KV_LANES:KV_COLS], ones], axis=0).astype(BF16)


def _inproj_rows(x, g, w):
    n = x.shape[0]
    tm = min(ROW_TILE, n)
    row = lambda c: pl.BlockSpec((tm, c), lambda i: (i, 0))
    full = lambda a: pl.BlockSpec(a.shape, lambda i: (0,) * a.ndim)
    outs = [(ATT_WIDTH, BF16), (POOL_WIDTH, F32), (LANE, F32), (KV_COLS, F32), (KV_COLS, F32), (KV_COLS, F32)]
    return pl.pallas_call(
        _inproj_rows_kernel,
        grid=(n // tm,),
        in_specs=[row(D_MODEL), full(g), full(w)],
        out_specs=[row(c) for c, _ in outs],
        out_shape=[jax.ShapeDtypeStruct((n, c), dt) for c, dt in outs],
        compiler_params=pltpu.CompilerParams(dimension_semantics=("parallel",), vmem_limit_bytes=VMEM_LIMIT),
    )(x, g, w)


def _inproj_seq(layer, x, g, w, all_shape, kvct_all, kvst_all, wb):
    n = x.shape[0]
    depth, b, _, t = all_shape
    assert (kvct_all is None) == (layer == 0)
    tm = min(ROW_TILE, t)
    tiles = t // tm
    assert wb % tm == 0 and wb <= t
    dropped = tiles - wb // tm
    row = lambda c: pl.BlockSpec((tm, c), lambda i: (i, 0))
    full = lambda a: pl.BlockSpec(a.shape, lambda i: (0,) * a.ndim)
    hbm = pl.BlockSpec(memory_space=pl.ANY)
    seq_t = lambda r: pl.BlockSpec((1, r, tm), lambda i: (i // tiles, 0, i % tiles))
    all_t = pl.BlockSpec((depth if layer == 0 else 1, 1, KV_COLS, tm), lambda i: (layer, i // tiles, 0, i % tiles))
    tail_t = pl.BlockSpec((1, KV_COLS, tm), lambda i: (i // tiles, 0, jnp.maximum(i % tiles - dropped, 0)))
    rows = [(ATT_WIDTH, BF16), (POOL_WIDTH, F32), (LANE, F32), (KV_COLS, BF16)]
    carried = [] if layer == 0 else [kvct_all, kvst_all]
    first_all = len(rows) + 2
    return pl.pallas_call(
        _inproj_seq_kernel,
        grid=(n // tm,),
        in_specs=[row(D_MODEL), full(g), full(w)] + [hbm] * len(carried),
        out_specs=[row(c) for c, _ in rows] + [seq_t(VT_ROWS), seq_t(VT_ROWS), all_t, all_t, tail_t],
        out_shape=[jax.ShapeDtypeStruct((n, c), dt) for c, dt in rows]
        + [jax.ShapeDtypeStruct((b, VT_ROWS, t), BF16), jax.ShapeDtypeStruct((b, VT_ROWS, t), BF16),
           jax.ShapeDtypeStruct(all_shape, F32), jax.ShapeDtypeStruct(all_shape, F32),
           jax.ShapeDtypeStruct((b, KV_COLS, wb), F32)],
        input_output_aliases={3 + k: first_all + k for k in range(len(carried))},
        compiler_params=pltpu.CompilerParams(dimension_semantics=("arbitrary",), vmem_limit_bytes=VMEM_LIMIT),
    )(x, g, w, *carried)


def _compress_tokens(x_of, w_ref, pe_ref, n_sub):
    outs = []
    for j in range(2):
        parts = [jnp.zeros((n_sub, KV_LANES), F32) for _ in range(CMP_RATIO)]
        for l in range(CMP_STRIDE):
            x = x_of(l, j)
            for r in range(CMP_RATIO):
                row = (j * CMP_RATIO + r) * CMP_STRIDE + l
                a = (x + pe_ref[row:row + 1, :]).astype(BF16)
                parts[r] = parts[r] + jnp.dot(a, w_ref[j, r, l], preferred_element_type=F32)
        comp = parts[0]
        for r in range(1, CMP_RATIO):
            comp = comp + pltpu.roll(parts[r], n_sub - r, axis=0)
        outs.append(comp)
    return outs


def _regroup_tokens(feat_tok, x_s, first_sub):
    pitch = x_s.shape[0] // CMP_STRIDE
    xt = feat_tok.T
    for r in range(xt.shape[0] // 8):
        l0 = (8 * r) % CMP_STRIDE
        sub = first_sub + (8 * r) // CMP_STRIDE
        x_s[pl.ds(l0 * pitch + sub, 8, stride=pitch), :] = xt[8 * r:8 * r + 8, :]


def _regrouped_rows(x_refs, n_sub):
    def x_of(l, j):
        pitch = x_refs[j].shape[0] // CMP_STRIDE
        return x_refs[j][l * pitch:l * pitch + n_sub, :]
    return x_of


def _compress_kernel(xt_ref, w_ref, pe_ref, ck_ref, cvt_ref, xk_s, xv_s):
    t = xt_ref.shape[3]
    n_sub = t // CMP_STRIDE
    for p in range(t // LANE):
        tok = slice(p * LANE, (p + 1) * LANE)
        _regroup_tokens(xt_ref[0, 0, 0:KV_LANES, tok], xk_s, p * (LANE // CMP_STRIDE))
        _regroup_tokens(xt_ref[0, 0, KV_LANES:KV_COLS, tok], xv_s, p * (LANE // CMP_STRIDE))
    ck, cv = _compress_tokens(_regrouped_rows((xk_s, xv_s), n_sub), w_ref, pe_ref, n_sub)
    ck_ref[0] = ck.astype(BF16)
    cvt_ref[0] = cv.T.astype(BF16)


def _compress_prompt(layer, kvct_all, w_bd, pe_rows):
    _, b, _, t = kvct_all.shape
    n_sub = t // CMP_STRIDE
    full = lambda a: pl.BlockSpec(a.shape, lambda i: (0,) * a.ndim)
    slabs = pltpu.VMEM((CMP_STRIDE * (n_sub + 8), KV_LANES), F32)
    return pl.pallas_call(
        _compress_kernel,
        grid=(b,),
        in_specs=[pl.BlockSpec((1, 1, KV_COLS, t), lambda i: (layer, i, 0, 0)), full(w_bd), full(pe_rows)],
        out_specs=[pl.BlockSpec((1, n_sub, KV_LANES), lambda i: (i, 0, 0)),
                   pl.BlockSpec((1, KV_LANES, n_sub), lambda i: (i, 0, 0))],
        out_shape=[jax.ShapeDtypeStruct((b, n_sub, KV_LANES), BF16), jax.ShapeDtypeStruct((b, KV_LANES, n_sub), BF16)],
        scratch_shapes=[slabs, slabs],
        compiler_params=pltpu.CompilerParams(dimension_semantics=("parallel",), vmem_limit_bytes=VMEM_LIMIT),
    )(kvct_all, w_bd, pe_rows)


def _split_dot(m, p):
    hi = p.astype(BF16)
    lo = (p - hi.astype(F32)).astype(BF16)
    return jnp.dot(m, hi, preferred_element_type=F32) + jnp.dot(m, lo, preferred_element_type=F32)


def _col_max(s):
    rows = s.shape[0]
    parts = [s[r:r + 64] for r in range(0, rows, 64)] if rows % 64 == 0 and rows > 64 else [s]
    while len(parts) > 1:
        parts = [jnp.maximum(a, b) for a, b in zip(parts[0::2], parts[1::2])] + ([parts[-1]] if len(parts) % 2 else [])
    return jnp.max(parts[0], axis=0, keepdims=True)


def _pool_mix(ext_ref, u, pos, wpool_ref, pscale_ref, rows):
    outs = []
    for k, w in enumerate(POOL_WINDOWS):
        ln = slice(k * POOL_GROUP_DIM, (k + 1) * POOL_GROUP_DIM)
        ws = ext_ref[POOL_HALO:POOL_HALO + rows, ln]
        for s in range(1, w):
            ws = ws + ext_ref[POOL_HALO - s:POOL_HALO - s + rows, ln]
        cnt = jnp.minimum(pos + 1, w).astype(F32)
        pooled = (ws / cnt - u[:, ln]).astype(BF16)
        o = jnp.dot(pooled, wpool_ref[k], preferred_element_type=F32) * pscale_ref[:, ln]
        outs.append(o.astype(BF16))
    return outs


def _prompt_attn_kernel(q_ref, gate_ref, u_ref, uh_ref, pst_ref, ksel_ref, kwin_ref, vst_ref, vwt_ref,
                        ck_ref, cvt_ref, delta_ref, mimp_ref, eblk_ref, wpool_ref, pscale_ref,
                        mix_ref,
                        sc_s, qa_s, qf_s, acc_s, m_s, ext_s):
    step = pl.program_id(1)
    n_blk = q_ref.shape[1] // TQ
    t_len = ksel_ref.shape[1]
    n_cmp = ck_ref.shape[1]
    n_sel = t_len // SEL_BLOCK
    cols = N_HEADS * TQ
    blk_per_q = TQ // SEL_BLOCK
    n_chain = cols // CHAIN_COLS
    chains = [slice(c * CHAIN_COLS, (c + 1) * CHAIN_COLS) for c in range(n_chain)]
    n_wc = WINDOW // TQ + 1
    w_delta = {0: DW0, n_wc - 2: DD0, n_wc - 1: DD0 + TQ}

    val = [dict() for _ in range(n_blk)]
    qb_of = lambda a: n_blk * step + a
    tok = lambda a: slice(a * TQ, (a + 1) * TQ)

    def stage_compressed(a):
        v, qb = val[a], qb_of(a)
        q = q_ref[0, tok(a), :].astype(F32)
        zeros = jnp.zeros((TQ, HEAD_DIM), F32)
        blocks = []
        for h in range(N_HEADS):
            halves = [zeros] * N_KV
            halves[h // GROUP] = q[:, h * HEAD_DIM:(h + 1) * HEAD_DIM]
            blocks.append(jnp.concatenate(halves, axis=1))
        qbd = jnp.concatenate(blocks, axis=0).astype(BF16)
        v["qbd"] = qbd
        sc = lax.dot_general(ck_ref[0], qbd, NT_DIMS, preferred_element_type=F32)
        crow = lax.broadcasted_iota(jnp.int32, (n_cmp, cols), 0)
        sc = jnp.where(crow >= 8 * qb + 8, NEG, sc)
        sc_s[a, 0:16, :] = jnp.zeros((16, cols), F32)
        sc_s[a, 16 + n_cmp:16 + n_cmp + 8, :] = jnp.zeros((8, cols), F32)
        sc_s[a, 16:16 + n_cmp, :] = sc
        near = pl.ds(pl.multiple_of(8 * qb, 8), DC_ROWS)
        sc_s[a, near, :] = sc_s[a, near, :] + delta_ref[DC0:DC0 + DC_ROWS, :]
        sc = sc_s[a, 16:16 + n_cmp, :]
        m_c = jnp.max(sc, axis=0, keepdims=True)
        p = jnp.exp2(sc - m_c)
        l_c = jnp.sum(p, axis=0, keepdims=True)
        pn = p * jnp.where(m_c > VALID_MIN, 1.0 / l_c, 0.0)
        v["o_cmp"] = jnp.dot(cvt_ref[0], pn.astype(BF16), preferred_element_type=F32)
        v["imp"] = []
        for g in range(N_KV):
            ps = pn[:, (g * GROUP) * TQ:(g * GROUP + 1) * TQ]
            for r in range(1, GROUP):
                ps = ps + pn[:, (g * GROUP + r) * TQ:(g * GROUP + r + 1) * TQ]
            v["imp"].append(_split_dot(mimp_ref[...], ps))

    def stage_window_scores(a):
        v, qb = val[a], qb_of(a)
        w_pos0 = [(qb - (n_wc - 1) + c) * TQ for c in range(n_wc)]
        w_k0 = [pl.multiple_of(jnp.maximum(p0, 0), TQ) for p0 in w_pos0]
        w_keys = [kwin_ref[0, pl.ds(k0, TQ), :] for k0 in w_k0]
        v["w_vt"] = [vwt_ref[0, :, pl.ds(k0, TQ)] for k0 in w_k0]
        v["s_win"] = []
        for cs in chains:
            s_w = []
            for c in range(n_wc):
                s = lax.dot_general(w_keys[c], v["qbd"][cs], NT_DIMS, preferred_element_type=F32)
                if c in w_delta:
                    s = s + delta_ref[w_delta[c]:w_delta[c] + TQ, cs]
                if c < n_wc - 1:
                    s = s + jnp.where(w_pos0[c] < 0, NEG, 0.0)
                s_w.append(s)
            v["s_win"].append(s_w)

    jidx = lax.broadcasted_iota(jnp.int32, (n_sel, TQ), 0)
    qi = lax.broadcasted_iota(jnp.int32, (n_sel, TQ), 1)
    pad = jnp.zeros((LANE - n_sel, TQ), F32)

    def stage_select(a):
        v, qb = val[a], qb_of(a)
        cur = blk_per_q * qb + qi // SEL_BLOCK
        forced = (jidx == 0) | (jidx == cur) | (jidx == cur - 1)
        prev_blk = blk_per_q * jnp.maximum(qb - 1, 0)
        ns_t = []
        for imp in v["imp"]:
            score = jnp.where(forced, jnp.inf, imp)
            score = jnp.where(jidx > cur, -jnp.inf, score)
            cnt = jnp.zeros((n_sel, TQ), jnp.int32)
            for jp in range(n_sel):
                row = score[jp:jp + 1, :]
                cnt = cnt + jnp.where(jidx > jp, jnp.where(row >= score, 1, 0), jnp.where(row > score, 1, 0))
            sel = (cnt < min(N_SELECT, n_sel)) & (jidx <= cur)
            ns_all = jnp.where(sel, 0.0, NEG)
            ns_far = jnp.where(jidx < prev_blk, ns_all, NEG)
            ns_t.append([jnp.concatenate([x, pad], axis=0).T.astype(BF16) for x in (ns_all, ns_far)])
        for h in range(N_HEADS):
            rows_h = slice(h * TQ, (h + 1) * TQ)
            qa_s[a, rows_h, 0:LANE] = v["qbd"][rows_h]
            qf_s[a, rows_h, 0:LANE] = v["qbd"][rows_h]
            qa_s[a, rows_h, LANE:2 * LANE] = ns_t[h // GROUP][0]
            qf_s[a, rows_h, LANE:2 * LANE] = ns_t[h // GROUP][1]
        m_s[a] = jnp.full((1, cols), NEG, F32)
        acc_s[a] = jnp.zeros((VT_ROWS, cols), F32)

    def online_update(a, cs, s, vt):
        m_old = m_s[a, :, cs]
        m_new = jnp.maximum(m_old, _col_max(s))
        alpha = jnp.exp2(m_old - m_new)
        p = jnp.exp2(s - m_new)
        acc_s[a, :, cs] = alpha * acc_s[a, :, cs] + jnp.dot(vt, p.astype(BF16), preferred_element_type=F32)
        m_s[a, :, cs] = m_new

    def keys_with_block(k0, n):
        return jnp.concatenate([ksel_ref[0, pl.ds(k0, n), :], eblk_ref[pl.ds(k0, n), :]], axis=1)

    def masked_scores(kcat, q_s, a, cs):
        return lax.dot_general(kcat, q_s[a, cs, :], NT_DIMS, preferred_element_type=F32)

    n_far = (TQ * jnp.maximum(qb_of(n_blk - 1) - 1, 0) + FAR_TILE - 1) // FAR_TILE

    def far_tiles(t0, n):
        k0 = [pl.multiple_of((t0 + i) * FAR_TILE, FAR_TILE) for i in range(n)]
        s_tiles = []
        for i in range(n):
            kcat = keys_with_block(k0[i], FAR_TILE)
            s_tiles.append([[masked_scores(kcat, qf_s, a, cs) for cs in chains] for a in range(n_blk)])
        for i in range(n):
            vt = vst_ref[0, :, pl.ds(k0[i], FAR_TILE)]
            for a in range(n_blk):
                for cs, s in zip(chains, s_tiles[i][a]):
                    online_update(a, cs, s, vt)

    def stage_near_scores(a):
        v, qb = val[a], qb_of(a)
        k_prev = pl.multiple_of(TQ * jnp.maximum(qb - 1, 0), TQ)
        k_diag = pl.multiple_of(qb * TQ, TQ)
        no_prev = jnp.where(qb == 0, NEG, 0.0)
        kcat_prev = keys_with_block(k_prev, TQ)
        kcat_diag = keys_with_block(k_diag, TQ)
        v["vt_near"] = jnp.concatenate([vst_ref[0, :, pl.ds(k_prev, TQ)], vst_ref[0, :, pl.ds(k_diag, TQ)]], axis=1)
        v["s_near"] = []
        for cs in chains:
            s_prev = masked_scores(kcat_prev, qa_s, a, cs) + delta_ref[DD0:DD0 + TQ, cs] + no_prev
            s_diag = masked_scores(kcat_diag, qa_s, a, cs) + delta_ref[DD0 + TQ:DD0 + 2 * TQ, cs]
            v["s_near"].append(jnp.concatenate([s_prev, s_diag], axis=0))

    def stage_near_update(a):
        v = val[a]
        for cs, s in zip(chains, v["s_near"]):
            online_update(a, cs, s, v["vt_near"])

    def stage_window(a):
        v = val[a]
        o_w = []
        for s_w in v["s_win"]:
            m_w = s_w[0].max(axis=0, keepdims=True)
            for s in s_w[1:]:
                m_w = jnp.maximum(m_w, s.max(axis=0, keepdims=True))
            o_c = jnp.zeros((VT_ROWS, CHAIN_COLS), F32)
            for s, vt in zip(s_w, v["w_vt"]):
                o_c = o_c + jnp.dot(vt, jnp.exp2(s - m_w).astype(BF16), preferred_element_type=F32)
            o_w.append(o_c)
        v["o_win"] = jnp.concatenate(o_w, axis=1)

    def stage_output(a):
        v, qb = val[a], qb_of(a)
        l_sel = acc_s[a, KV_LANES:KV_LANES + 1, :]
        o_sel = acc_s[a, 0:KV_LANES, :]
        l_win = v["o_win"][KV_LANES:KV_LANES + 1, :]
        o_win = v["o_win"][0:KV_LANES, :]
        gt = gate_ref[0, tok(a), :].T
        gate_row = lambda x: jnp.concatenate([gt[N_BRANCH * h + x:N_BRANCH * h + x + 1, :] for h in range(N_HEADS)],
                                             axis=1)
        o_t = v["o_cmp"] * gate_row(0) + o_sel * (gate_row(1) / l_sel) + o_win * (gate_row(2) / l_win)
        pieces = []
        for h in range(N_HEADS):
            g = h // GROUP
            pieces.append(o_t[g * HEAD_DIM:(g + 1) * HEAD_DIM, h * TQ:(h + 1) * TQ])
        mix_ref[0, tok(a), 0:ATT_WIDTH] = jnp.concatenate(pieces, axis=0).T.astype(BF16)
        u = u_ref[0, tok(a), :]
        ext = ext_s.at[a]
        if a == 0:
            ext[0:POOL_HALO, :] = jnp.where(qb == 0, pst_ref[0], uh_ref[0])
        else:
            ext[0:POOL_HALO, :] = u_ref[0, a * TQ - POOL_HALO:a * TQ, :]
        ext[POOL_HALO:POOL_HALO + TQ, :] = u
        pos = qb * TQ + lax.broadcasted_iota(jnp.int32, (TQ, 1), 0)
        for k, o in enumerate(_pool_mix(ext, u, pos, wpool_ref, pscale_ref, TQ)):
            mix_ref[0, tok(a), ATT_WIDTH + k * POOL_GROUP_DIM:ATT_WIDTH + (k + 1) * POOL_GROUP_DIM] = o

    def run(stage):
        for a in range(n_blk):
            stage(a)

    run(stage_compressed)
    run(stage_window_scores)
    run(stage_select)

    odd = lax.rem(n_far, 2)

    @pl.when(odd == 1)
    def _():
        far_tiles(0, 1)

    def pair_body(i, carry):
        far_tiles(odd + 2 * i, 2)
        return carry

    lax.fori_loop(0, lax.div(n_far, 2), pair_body, 0)

    run(stage_near_scores)
    run(stage_near_update)
    run(stage_window)
    run(stage_output)


def _prompt_attention(q, gates, u, pool_prev, kb, vst, vwt, ck, cvt, delta, mimp, eblk, wpool, pscale):
    b, t, _ = q.shape
    assert t % FAR_TILE == 0 and t >= WINDOW + TQ and t // SEL_BLOCK <= LANE
    n_cmp = ck.shape[1]
    cols = N_HEADS * TQ
    nq = PROMPT_QBLOCKS if (t // TQ) % PROMPT_QBLOCKS == 0 else 1
    rows = nq * TQ
    halo_per_step = rows // POOL_HALO
    full = lambda a: pl.BlockSpec(a.shape, lambda i, j: (0,) * a.ndim)
    qblk = lambda c: pl.BlockSpec((1, rows, c), lambda i, j: (i, j, 0))
    seq = lambda c: pl.BlockSpec((1, t, KV_LANES), lambda i, j: (i, 0, c))
    per_b = lambda a: pl.BlockSpec((1,) + a.shape[1:], lambda i, j: (i,) + (0,) * (a.ndim - 1))
    return pl.pallas_call(
        _prompt_attn_kernel,
        grid=(b, t // rows),
        in_specs=[qblk(ATT_WIDTH), qblk(LANE), qblk(POOL_WIDTH),
                  pl.BlockSpec((1, POOL_HALO, POOL_WIDTH), lambda i, j: (i, jnp.maximum(j * halo_per_step - 1, 0), 0)),
                  per_b(pool_prev), seq(0), seq(1), per_b(vst), per_b(vwt), per_b(ck), per_b(cvt),
                  full(delta), full(mimp), full(eblk), full(wpool), full(pscale)],
        out_specs=pl.BlockSpec((1, rows, D_MODEL), lambda i, j: (i, j, 0)),
        out_shape=jax.ShapeDtypeStruct((b, t, D_MODEL), BF16),
        scratch_shapes=[pltpu.VMEM((nq, 16 + n_cmp + 8, cols), F32),
                        pltpu.VMEM((nq, cols, 2 * LANE), BF16), pltpu.VMEM((nq, cols, 2 * LANE), BF16),
                        pltpu.VMEM((nq, VT_ROWS, cols), F32), pltpu.VMEM((nq, 1, cols), F32),
                        pltpu.VMEM((nq, POOL_HALO + TQ, POOL_WIDTH), F32)],
        compiler_params=pltpu.CompilerParams(dimension_semantics=("parallel", "arbitrary"),
                                             vmem_limit_bytes=VMEM_LIMIT),
    )(q, gates, u, u, pool_prev, kb, kb, vst, vwt, ck, cvt, delta, mimp, eblk, wpool, pscale)


SAMPLE_SEQS = 2
SD_LAST = 0
SD_NEW = LANE
SD_EDGE = 2 * LANE
SD_CMP = 3 * LANE


def _sample_attn_kernel(pt_ref, q_ref, gate_ref, u_ref, kvs_ref, kvw_ref, swin_ref, spool_ref,
                        ccache_ref, scache_ref, *refs, layer):
    (wcmp_ref, pe_ref, delta_ref, mimp_ref, expand_ref, wpool_ref, pscale_ref,
     mix_ref, nwin_ref, npool_ref, cbuf, sbuf, xk_s, xv_s, sem, ext_s) = refs[-16:]
    b = pl.program_id(0)
    nb = pl.num_programs(0)
    n_pages = pt_ref.shape[1]
    past = n_pages * PAGE_SIZE
    n_sub = past // CMP_STRIDE
    n_seq, tq = q_ref.shape[0], q_ref.shape[1]
    wb = swin_ref.shape[3]
    n_sel = past // SEL_BLOCK + 1
    sub_per_page = PAGE_SIZE // CMP_STRIDE
    slot = lax.rem(b, 2)

    def page_copies(step, slt):
        cps = []
        for s in range(n_seq):
            for p in range(n_pages):
                pg = pt_ref[step * n_seq + s, p]
                dst = (slice(None), pl.ds(p * PAGE_SIZE, PAGE_SIZE))
                cps.append(pltpu.make_async_copy(ccache_ref.at[layer, pg], cbuf.at[slt, s].at[dst], sem.at[0, slt]))
                cps.append(pltpu.make_async_copy(scache_ref.at[layer, pg], sbuf.at[slt, s].at[dst], sem.at[1, slt]))
        return cps

    @pl.when(b == 0)
    def _():
        for cp in page_copies(0, 0):
            cp.start()

    @pl.when(b + 1 < nb)
    def _():
        for cp in page_copies(b + 1, 1 - slot):
            cp.start()

    for cp in page_copies(b, slot):
        cp.wait()

    val = [dict() for _ in range(n_seq)]

    def scores(qbd, k_rows):
        return lax.dot_general(qbd, k_rows, NT_DIMS, preferred_element_type=F32)

    def scores_t(qbd, k_t):
        return jnp.dot(qbd, k_t.astype(BF16), preferred_element_type=F32)

    def pad_rows(x):
        return jnp.concatenate([x, jnp.zeros((LANE - tq, KV_LANES), F32)], axis=0)

    pad_new = lambda x: pad_rows(x).astype(BF16)
    d_last = delta_ref[:, SD_LAST:SD_LAST + LANE]
    d_new = delta_ref[:, SD_NEW:SD_NEW + LANE]

    def softmax_pv(s_old, vt_old, s_new, v_new):
        m = jnp.maximum(s_old.max(axis=1, keepdims=True), s_new.max(axis=1, keepdims=True))
        p_old = jnp.exp2(s_old - m)
        p_new = jnp.exp2(s_new - m)
        l = jnp.sum(p_old, axis=1, keepdims=True) + jnp.sum(p_new, axis=1, keepdims=True)
        o = (lax.dot_general(p_old.astype(BF16), vt_old.astype(BF16), NT_DIMS, preferred_element_type=F32)
             + jnp.dot(p_new.astype(BF16), v_new, preferred_element_type=F32))
        return o, l

    def stage_scores(s):
        v = val[s]
        q = q_ref[s].astype(F32)
        zeros = jnp.zeros((tq, HEAD_DIM), F32)
        blocks = []
        for h in range(N_HEADS):
            halves = [zeros] * N_KV
            halves[h // GROUP] = q[:, h * HEAD_DIM:(h + 1) * HEAD_DIM]
            blocks.append(jnp.concatenate(halves, axis=1))
        qbd = jnp.concatenate(blocks, axis=0).astype(BF16)
        v["qbd"] = qbd
        kvw_new = kvw_ref[s]
        s_first = scores_t(qbd, swin_ref[0, s, 0:KV_LANES, 0:LANE]) + delta_ref[:, SD_EDGE:SD_EDGE + LANE]
        s_mid = scores_t(qbd, swin_ref[0, s, 0:KV_LANES, LANE:wb - LANE])
        s_lastw = scores_t(qbd, swin_ref[0, s, 0:KV_LANES, wb - LANE:wb]) + d_last
        v["s_wold"] = jnp.concatenate([s_first, s_mid, s_lastw], axis=1)
        v["s_wnew"] = scores(qbd, pad_new(kvw_new[:, 0:KV_LANES])) + d_new
        kvs_new = kvs_ref[s]
        s_far = scores_t(qbd, sbuf[slot, s, 0:KV_LANES, 0:past - LANE])
        s_last = scores_t(qbd, sbuf[slot, s, 0:KV_LANES, past - LANE:past]) + d_last
        v["s_past"] = jnp.concatenate([s_far, s_last], axis=1)
        v["s_new"] = scores(qbd, pad_new(kvs_new[:, 0:KV_LANES])) + d_new

    def stage_regroup(s):
        for p in range(n_pages):
            tok = slice(p * PAGE_SIZE, (p + 1) * PAGE_SIZE)
            _regroup_tokens(cbuf[slot, s, 0:KV_LANES, tok], xk_s.at[s], p * sub_per_page)
            _regroup_tokens(cbuf[slot, s, KV_LANES:KV_COLS, tok], xv_s.at[s], p * sub_per_page)

    def stage_window(s):
        v = val[s]
        kvw_new = kvw_ref[s]
        v["o_win"], v["l_win"] = softmax_pv(v["s_wold"], swin_ref[0, s, KV_LANES:KV_COLS, :],
                                            v["s_wnew"], pad_new(kvw_new[:, KV_LANES:KV_COLS]))
        shifted = pltpu.roll(swin_ref[0, s], wb - tq, axis=1)
        new_t = jnp.concatenate([pad_rows(kvw_new[:, 0:KV_LANES]).T, pad_rows(kvw_new[:, KV_LANES:KV_COLS]).T], axis=0)
        new_t = pltpu.roll(new_t, LANE - tq, axis=1)
        lane = lax.broadcasted_iota(jnp.int32, (KV_COLS, LANE), 1)
        nwin_ref[0, s, :, 0:wb - LANE] = shifted[:, 0:wb - LANE]
        nwin_ref[0, s, :, wb - LANE:wb] = jnp.where(lane >= LANE - tq, new_t, shifted[:, wb - LANE:wb])
        for later in range(1, nwin_ref.shape[0]):
            nwin_ref[later, s] = jnp.zeros((KV_COLS, wb), F32)

    def stage_compress(s):
        v = val[s]
        ck, cv = _compress_tokens(_regrouped_rows((xk_s.at[s], xv_s.at[s]), n_sub), wcmp_ref, pe_ref, n_sub)
        v["cv"] = cv.astype(BF16)
        v["sc"] = scores(v["qbd"], ck.astype(BF16)) + delta_ref[:, SD_CMP:SD_CMP + n_sub]

    def stage_compressed(s):
        v = val[s]
        sc = v["sc"]
        m_c = jnp.max(sc, axis=1, keepdims=True)
        p = jnp.exp2(sc - m_c)
        l_c = jnp.sum(p, axis=1, keepdims=True)
        pn = p * jnp.where(m_c > VALID_MIN, 1.0 / l_c, 0.0)
        v["o_cmp"] = jnp.dot(pn.astype(BF16), v["cv"], preferred_element_type=F32)
        v["imp"] = []
        for g in range(N_KV):
            ps = pn[g * GROUP * tq:(g * GROUP + 1) * tq, :]
            for r in range(1, GROUP):
                ps = ps + pn[(g * GROUP + r) * tq:(g * GROUP + r + 1) * tq, :]
            hi = ps.astype(BF16)
            lo = (ps - hi.astype(F32)).astype(BF16)
            v["imp"].append(jnp.dot(hi, mimp_ref[...], preferred_element_type=F32)
                            + jnp.dot(lo, mimp_ref[...], preferred_element_type=F32))

    jidx = lax.broadcasted_iota(jnp.int32, (tq, LANE), 1)
    cur = (past + lax.broadcasted_iota(jnp.int32, (tq, LANE), 0)) // SEL_BLOCK
    forced = (jidx == 0) | (jidx == cur) | (jidx == cur - 1)

    def stage_select(s):
        v = val[s]
        sel_g = []
        for imp in v["imp"]:
            score = jnp.where(forced, jnp.inf, imp)
            score = jnp.where((jidx > cur) | (jidx >= n_sel), -jnp.inf, score)
            cnt = jnp.zeros((tq, LANE), jnp.int32)
            for jp in range(n_sel):
                col = score[:, jp:jp + 1]
                cnt = cnt + jnp.where(jidx > jp, jnp.where(col >= score, 1, 0), jnp.where(col > score, 1, 0))
            sel = (cnt < min(N_SELECT, n_sel)) & (jidx <= cur) & (jidx < n_sel)
            sel_g.append(jnp.where(sel, 1.0, 0.0))
        sel_rows = jnp.concatenate([sel_g[h // GROUP] for h in range(N_HEADS)], axis=0).astype(BF16)
        v["mask"] = jnp.dot(sel_rows, expand_ref[...], preferred_element_type=F32)

    def stage_selected(s):
        v = val[s]
        s_past = jnp.where(v["mask"] > 0.5, v["s_past"], NEG)
        v["o_sel"], v["l_sel"] = softmax_pv(s_past, sbuf[slot, s, KV_LANES:KV_COLS, :], v["s_new"],
                                            pad_new(kvs_ref[s][:, KV_LANES:KV_COLS]))

    def stage_output(s):
        v = val[s]
        gates = gate_ref[s]
        gate_col = lambda x: jnp.concatenate([gates[:, N_BRANCH * h + x:N_BRANCH * h + x + 1] for h in range(N_HEADS)],
                                             axis=0)
        o = (v["o_cmp"] * gate_col(0) + v["o_sel"] * (gate_col(1) / v["l_sel"])
             + v["o_win"] * (gate_col(2) / v["l_win"]))
        pieces = []
        for h in range(N_HEADS):
            g = h // GROUP
            pieces.append(o[h * tq:(h + 1) * tq, g * HEAD_DIM:(g + 1) * HEAD_DIM])
        mix_ref[s, :, 0:ATT_WIDTH] = jnp.concatenate(pieces, axis=1).astype(BF16)
        u = u_ref[s]
        ext = ext_s.at[s]
        ext[0:1, :] = jnp.zeros((1, POOL_WIDTH), F32)
        ext[1:POOL_HALO, :] = spool_ref[0, s]
        ext[POOL_HALO:POOL_HALO + tq, :] = u
        pos = past + lax.broadcasted_iota(jnp.int32, (tq, 1), 0)
        for k, o_k in enumerate(_pool_mix(ext, u, pos, wpool_ref, pscale_ref, tq)):
            mix_ref[s, :, ATT_WIDTH + k * POOL_GROUP_DIM:ATT_WIDTH + (k + 1) * POOL_GROUP_DIM] = o_k
        npool_ref[s] = ext[POOL_HALO + tq - POOL_STATE:POOL_HALO + tq, :]

    for stage in (stage_scores, stage_regroup, stage_window, stage_compress, stage_compressed, stage_select,
                  stage_selected, stage_output):
        for s in range(n_seq):
            stage(s)


def _sample_attention(layer, page_table, q, gates, u, kvs, kvw, swin_t, state_pool, ccache_t, scache_t, nwin_all,
                      wcmp, pe_rows, delta, mimp, expand, wpool, pscale):
    nb, tq, _ = q.shape
    assert (nwin_all is None) == (layer == 0)
    carried = [] if layer == 0 else [nwin_all]
    depth = swin_t.shape[0]
    n_pages = page_table.shape[1]
    past = n_pages * PAGE_SIZE
    n_sub = past // CMP_STRIDE
    wb = swin_t.shape[3]
    assert wb == WINDOW and wb >= 3 * LANE and tq <= CMP_STRIDE and POOL_HALO + tq - POOL_STATE >= 0
    assert past // SEL_BLOCK + 1 <= LANE and past % LANE == 0 and n_sub <= LANE
    ns = SAMPLE_SEQS if nb % SAMPLE_SEQS == 0 else 1
    full = lambda a: pl.BlockSpec(a.shape, lambda i, pt: (0,) * a.ndim)
    per_b = lambda c: pl.BlockSpec((ns, tq, c), lambda i, pt: (i, 0, 0))
    layer_b = lambda a: pl.BlockSpec((1, ns) + a.shape[2:], lambda i, pt: (layer, i) + (0,) * (a.ndim - 2))
    hbm = pl.BlockSpec(memory_space=pl.ANY)
    grid_spec = pltpu.PrefetchScalarGridSpec(
        num_scalar_prefetch=1,
        grid=(nb // ns,),
        in_specs=[per_b(ATT_WIDTH), per_b(LANE), per_b(POOL_WIDTH), per_b(KV_COLS), per_b(KV_COLS),
                  layer_b(swin_t), layer_b(state_pool), hbm, hbm] + [hbm] * len(carried)
        + [full(wcmp), full(pe_rows), full(delta), full(mimp), full(expand), full(wpool), full(pscale)],
        out_specs=[pl.BlockSpec((ns, tq, D_MODEL), lambda i, pt: (i, 0, 0)),
                   pl.BlockSpec((depth if layer == 0 else 1, ns, KV_COLS, wb), lambda i, pt: (layer, i, 0, 0)),
                   pl.BlockSpec((ns, POOL_STATE, POOL_WIDTH), lambda i, pt: (i, 0, 0))],
        scratch_shapes=[pltpu.VMEM((2, ns, KV_COLS, past), F32), pltpu.VMEM((2, ns, KV_COLS, past), F32),
                        pltpu.VMEM((ns, CMP_STRIDE * (n_sub + 8), KV_LANES), F32),
                        pltpu.VMEM((ns, CMP_STRIDE * (n_sub + 8), KV_LANES), F32),
                        pltpu.SemaphoreType.DMA((2, 2)), pltpu.VMEM((ns, POOL_HALO + tq, POOL_WIDTH), F32)],
    )
    return pl.pallas_call(
        functools.partial(_sample_attn_kernel, layer=layer),
        grid_spec=grid_spec,
        out_shape=[jax.ShapeDtypeStruct((nb, tq, D_MODEL), BF16),
                   jax.ShapeDtypeStruct(swin_t.shape, F32),
                   jax.ShapeDtypeStruct((nb, POOL_STATE, POOL_WIDTH), F32)],
        input_output_aliases={10: 1} if carried else {},
        compiler_params=pltpu.CompilerParams(dimension_semantics=("arbitrary",), vmem_limit_bytes=VMEM_LIMIT),
    )(page_table, q, gates, u, kvs, kvw, swin_t, state_pool, ccache_t, scache_t, *carried,
      wcmp, pe_rows, delta, mimp, expand, wpool, pscale)


def _ffn_kernel(x_ref, mix_ref, wout_ref, gn_ref, win_ref, wd_ref, gf_ref, o_ref, *, final_norm):
    xm = x_ref[...] + jnp.dot(mix_ref[...], wout_ref[...], preferred_element_type=F32)
    h = _rms(xm, gn_ref[...]).astype(BF16)
    d_ff = wd_ref.shape[0]
    cuts = list(range(0, d_ff, FFN_SLICE)) + [d_ff]
    slices = list(zip(cuts[:-1], cuts[1:]))

    def gate_up(a, b):
        return (jnp.dot(h, win_ref[:, a:b], preferred_element_type=F32),
                jnp.dot(h, win_ref[:, d_ff + a:d_ff + b], preferred_element_type=F32))

    y = xm
    pending = gate_up(*slices[0])
    for j, (a, b) in enumerate(slices):
        gate, up = pending
        if j + 1 < len(slices):
            pending = gate_up(*slices[j + 1])
        act = (gate * _sigmoid(gate) * up).astype(BF16)
        y = y + jnp.dot(act, wd_ref[a:b, :], preferred_element_type=F32)
    o_ref[...] = _rms(y, gf_ref[...]) if final_norm else y


def _ffn(x, mix, wout, gn, wffn_in, wffn_out, gf, final_norm):
    n = x.shape[0]
    tm = min(ROW_TILE, n)
    assert n % tm == 0 and wffn_out.shape[0] % LANE == 0
    once = lambda a: pl.BlockSpec(a.shape, lambda i: (0,) * a.ndim, pipeline_mode=pl.Buffered(1))
    row = lambda cdim: pl.BlockSpec((tm, cdim), lambda i: (i, 0))
    return pl.pallas_call(
        functools.partial(_ffn_kernel, final_norm=final_norm),
        grid=(n // tm,),
        in_specs=[row(D_MODEL), row(D_MODEL), once(wout), once(gn), once(wffn_in), once(wffn_out), once(gf)],
        out_specs=row(D_MODEL),
        out_shape=jax.ShapeDtypeStruct((n, D_MODEL), F32),
        compiler_params=pltpu.CompilerParams(dimension_semantics=("parallel",), vmem_limit_bytes=VMEM_LIMIT),
    )(x, mix, wout, gn, wffn_in, wffn_out, gf)


def _importance_matrix(n_sel, n_cmp_rows, n_cmp):
    spb = SEL_BLOCK // CMP_STRIDE
    j = np.arange(n_sel)[:, None]
    c = np.arange(n_cmp_rows)[None, :]
    return ((c >= spb * j - (CMP_RATIO - 1)) & (c <= spb * j + spb - 1) & (c < n_cmp)).astype(np.float32)


def kernel(x_prompt, x_sample, cache_cmp, cache_sel, state_win, state_pool, page_table, rel_bias, norm_mix, norm_ffn,
           norm_final, w_in, w_out, cmp_pos, w_cmp, w_pool, pool_scale, w_ffn_in, w_ffn_out):
    bp, t, _ = x_prompt.shape
    bs, tq, _ = x_sample.shape
    depth = w_in.shape[0]
    n_pages = page_table.shape[1]
    past = n_pages * PAGE_SIZE
    wb = state_win.shape[2]
    _check_far_bucket(max(t, past + tq) + WINDOW)

    rb_prompt = jnp.repeat(rel_bias.astype(F32), TQ, axis=1)
    delta_p = _bias_tiles(_prompt_bucket_table(), rb_prompt)
    rb_sample = jnp.pad(jnp.repeat(rel_bias.astype(F32), tq, axis=1), ((0, 0), (0, LANE - N_HEADS * tq)))
    delta_s = _bias_tiles(_sample_bucket_table(past, tq, wb), rb_sample).T[:N_HEADS * tq]

    n_sub_p = t // CMP_STRIDE
    mimp_p = jnp.asarray(_importance_matrix(t // SEL_BLOCK, n_sub_p, n_sub_p - CMP_RATIO + 1), BF16)
    eblk_p = jnp.asarray((np.arange(t)[:, None] // SEL_BLOCK == np.arange(LANE)[None, :]).astype(np.float32), BF16)
    n_sub_s = past // CMP_STRIDE
    mimp_s = jnp.asarray(_importance_matrix(LANE, n_sub_s, n_sub_s - CMP_RATIO + 1).T, BF16)
    expand = jnp.asarray((np.arange(LANE)[:, None] == np.arange(past)[None, :] // SEL_BLOCK).astype(np.float32), BF16)

    pool_zero = jnp.zeros((bp, POOL_HALO, POOL_WIDTH), F32)
    feature_major = lambda a: jnp.moveaxis(a, 2, -1).reshape(a.shape[0], a.shape[1], KV_COLS, a.shape[2])
    token_major = lambda a: jnp.moveaxis(a.reshape(a.shape[:2] + (2, N_KV, HEAD_DIM, a.shape[3])), -1, 2)
    ccache_t = feature_major(cache_cmp)
    scache_t = feature_major(cache_sel)
    swin_t = feature_major(state_win)
    kvct_all = kvst_all = nwin_all = None

    xp = x_prompt.reshape(bp * t, D_MODEL)
    xs = x_sample.reshape(bs * tq, D_MODEL)
    outs = {k: [] for k in ("p_win", "p_pool", "s_cmp", "s_sel", "s_pool")}
    kv_shape = lambda b, n: (b, n, 2, N_KV, HEAD_DIM)
    off_g = ATT_WIDTH + N_BRANCH * KV_COLS
    for l in range(depth):
        wl = w_in[l]
        w_all = jnp.concatenate([wl[:, :ATT_WIDTH] * (HEAD_DIM ** -0.5 * LOG2E), wl[:, ATT_WIDTH:off_g],
                                 wl[:, off_g + GATE_COLS:], wl[:, off_g:off_g + GATE_COLS],
                                 jnp.zeros((D_MODEL, LANE - GATE_COLS), F32)], axis=1).astype(BF16)
        g_mix = norm_mix[l].reshape(1, D_MODEL)
        g_ffn = norm_ffn[l].reshape(1, D_MODEL)
        g_fin = norm_final.reshape(1, D_MODEL)
        wc = w_cmp[l].reshape(2, CMP_RATIO, CMP_STRIDE, HEAD_DIM, HEAD_DIM)
        zero = jnp.zeros_like(wc)
        w_bd = jnp.concatenate([jnp.concatenate([wc, zero], axis=-1), jnp.concatenate([zero, wc], axis=-1)],
                               axis=-2).astype(BF16)
        pe = cmp_pos[l].reshape(CMP_RATIO, CMP_STRIDE, 2, HEAD_DIM).transpose(2, 0, 1, 3)
        pe_rows = jnp.tile(pe.reshape(2 * CMP_RATIO * CMP_STRIDE, HEAD_DIM), (1, N_KV)).astype(F32)
        wpool = w_pool[l].astype(BF16)
        pscale = pool_scale[l].reshape(1, POOL_WIDTH)
        wout = w_out[l].astype(BF16)
        wffn_in = w_ffn_in[l].astype(BF16)
        wffn_out = w_ffn_out[l].astype(BF16)
        last = l == depth - 1

        q, u, gates, kb, vst, vwt, kvct_all, kvst_all, kvwt = _inproj_seq(
            l, xp, g_mix, w_all, (depth, bp, KV_COLS, t), kvct_all, kvst_all, wb)
        ck, cvt = _compress_prompt(l, kvct_all, w_bd, pe_rows)
        mix = _prompt_attention(q.reshape(bp, t, ATT_WIDTH), gates.reshape(bp, t, LANE), u.reshape(bp, t, POOL_WIDTH),
                                pool_zero, kb.reshape(bp, t, KV_COLS), vst, vwt, ck, cvt, delta_p, mimp_p, eblk_p,
                                wpool, pscale)
        xp = _ffn(xp, mix.reshape(bp * t, D_MODEL), wout, g_ffn, wffn_in, wffn_out, g_fin, last)
        outs["p_win"].append(kvwt)
        outs["p_pool"].append(u.reshape(bp, t, POOL_WIDTH)[:, t - POOL_STATE:])

        q, u, gates, kvc, kvs, kvw = _inproj_rows(xs, g_mix, w_all)
        mix, nwin_all, npool = _sample_attention(
            l, page_table, q.reshape(bs, tq, ATT_WIDTH), gates.reshape(bs, tq, LANE), u.reshape(bs, tq, POOL_WIDTH),
            kvs.reshape(bs, tq, KV_COLS), kvw.reshape(bs, tq, KV_COLS), swin_t, state_pool, ccache_t, scache_t,
            nwin_all, w_bd, pe_rows, delta_s, mimp_s, expand, wpool, pscale)
        xs = _ffn(xs, mix.reshape(bs * tq, D_MODEL), wout, g_ffn, wffn_in, wffn_out, g_fin, last)
        outs["s_cmp"].append(kvc.reshape(kv_shape(bs, tq)))
        outs["s_sel"].append(kvs.reshape(kv_shape(bs, tq)))
        outs["s_pool"].append(npool)

    return (xp.reshape(bp, t, D_MODEL), xs.reshape(bs, tq, D_MODEL),
            token_major(kvct_all), token_major(kvst_all), token_major(jnp.stack(outs["p_win"])),
            jnp.stack(outs["p_pool"]), jnp.stack(outs["s_cmp"]), jnp.stack(outs["s_sel"]), token_major(nwin_all),
            jnp.stack(outs["s_pool"]))
```

```python
import functools
import math

import numpy as np
import jax
import jax.numpy as jnp
from jax import lax
from jax.experimental import pallas as pl
from jax.experimental.pallas import tpu as pltpu

D_MODEL = 1024
HEAD_DIM = 64
N_HEADS = 8
N_KV = 2
GROUP = N_HEADS // N_KV
ATT_WIDTH = N_HEADS * HEAD_DIM
POOL_WIDTH = D_MODEL - ATT_WIDTH
KV_COLS = 2 * N_KV * HEAD_DIM
KV_LANES = N_KV * HEAD_DIM
VT_ROWS = KV_LANES + 16
CMP_LEN = 32
CMP_STRIDE = 16
CMP_RATIO = CMP_LEN // CMP_STRIDE
SEL_BLOCK = 64
N_SELECT = 16
WINDOW = 512
N_BRANCH = 3
GATE_COLS = N_HEADS * N_BRANCH
POOL_WINDOWS = (2, 4, 8, 16)
POOL_GROUP_DIM = POOL_WIDTH // len(POOL_WINDOWS)
POOL_STATE = max(POOL_WINDOWS) - 1
POOL_HALO = 16
N_BUCKETS = 32
MAX_DISTANCE = 128
PAGE_SIZE = 128
EPS = 1e-6
NEG = -1e30
VALID_MIN = -1e29

LANE = 128
TQ = 128
PROMPT_QBLOCKS = 2
FAR_TILE = 512
CHAIN_COLS = 256
LOG2E = 1.4426950408889634
ROW_TILE = 512
FFN_SLICE = 512
VMEM_LIMIT = 52 * 1024 * 1024

F32 = jnp.float32
BF16 = jnp.bfloat16
NT_DIMS = (((1,), (1,)), ((), ()))


def _bucket_np(dist):
    n = np.maximum(dist, 0)
    max_exact = N_BUCKETS // 2
    nf = np.maximum(n, 1).astype(np.float32)
    large = max_exact + (np.log(nf / max_exact) / math.log(MAX_DISTANCE / max_exact)
                         * (N_BUCKETS - max_exact)).astype(np.int32)
    return np.where(n < max_exact, n, np.minimum(large, N_BUCKETS - 1)).astype(np.int32)


def _bucket_or_masked(dist, valid):
    return np.where(valid, _bucket_np(dist), -1).astype(np.int32)


FAR_DIST = TQ + 1


def _check_far_bucket(max_dist):
    assert (_bucket_np(np.arange(FAR_DIST, max_dist + 1)) == N_BUCKETS - 1).all()


def _bias_tile_kernel(bucket_ref, rb_ref, out_ref):
    bucket = bucket_ref[...]
    acc = jnp.zeros(bucket.shape, F32)
    for b in range(N_BUCKETS):
        acc = acc + jnp.where(bucket == b, rb_ref[b:b + 1, :], 0.0)
    acc = (acc - rb_ref[N_BUCKETS - 1:N_BUCKETS, :]) * LOG2E
    out_ref[...] = jnp.where(bucket < 0, NEG, acc)


def _bias_tiles(bucket_np, rb_cols):
    rows, cols = bucket_np.shape
    return pl.pallas_call(
        _bias_tile_kernel,
        out_shape=jax.ShapeDtypeStruct((rows, cols), F32),
    )(jnp.asarray(bucket_np), rb_cols)


DD0 = 0
DC0 = 2 * TQ
DC_ROWS = 24
DW0 = DC0 + 32


def _prompt_bucket_table():
    i = np.arange(TQ)[None, :]
    kk = np.arange(2 * TQ)[:, None]
    d = i + TQ - kk
    dd = _bucket_or_masked(d, d >= 0)
    cc = np.arange(32)[:, None] - 16
    d = i - CMP_STRIDE * cc - (CMP_LEN - 1)
    dc = _bucket_or_masked(d, (d >= 0) & (cc < 8))
    j = np.arange(TQ)[:, None]
    d = i + WINDOW - j
    dw = _bucket_or_masked(d, (d >= 0) & (d < WINDOW))
    tab = np.concatenate([dd, dc, dw], axis=0)
    return np.tile(tab, (1, N_HEADS))


def _sample_bucket_table(past, tq, wb):
    col = np.arange(LANE)
    t = (col % tq)[None, :]
    colok = (col < N_HEADS * tq)[None, :]
    kk = np.arange(LANE)[:, None]
    d = LANE + t - kk
    dlast = _bucket_or_masked(d, colok & (d >= 0))
    d = t - kk
    dnew = _bucket_or_masked(d, colok & (d >= 0) & (kk < tq))
    d = wb + t - kk
    dedge = _bucket_or_masked(d, colok & (d >= 0) & (d < WINDOW))
    nsub = past // CMP_STRIDE
    c = np.arange(nsub)[:, None]
    d = past + t - CMP_STRIDE * c - (CMP_LEN - 1)
    dcmp = _bucket_or_masked(d, colok & (d >= 0) & (c < nsub - CMP_RATIO + 1))
    return np.concatenate([dlast, dnew, dedge, dcmp], axis=0)


IN_Q = 0
IN_KV = ATT_WIDTH
IN_U = IN_KV + N_BRANCH * KV_COLS
IN_G = IN_U + POOL_WIDTH
IN_COLS_PAD = IN_G + LANE


def _rms(x, g):
    return x * lax.rsqrt(jnp.mean(x * x, axis=-1, keepdims=True) + EPS) * g


def _sigmoid(x):
    return 1.0 / (1.0 + jnp.exp(-x))


def _inproj_common(x_ref, g_ref, w_ref, q_ref, u_ref, gate_ref):
    h = _rms(x_ref[...], g_ref[...]).astype(BF16)
    proj = lambda a, b: jnp.dot(h, w_ref[:, a:b], preferred_element_type=F32)
    q_ref[...] = proj(IN_Q, IN_KV).astype(BF16)
    kv = proj(IN_KV, IN_U)
    u_ref[...] = proj(IN_U, IN_G)
    gate_ref[...] = _sigmoid(proj(IN_G, IN_COLS_PAD))
    return [kv[:, i * KV_COLS:(i + 1) * KV_COLS] for i in range(N_BRANCH)]


def _inproj_rows_kernel(x_ref, g_ref, w_ref, q_ref, u_ref, gate_ref, kvc_ref, kvs_ref, kvw_ref):
    kvc, kvs, kvw = _inproj_common(x_ref, g_ref, w_ref, q_ref, u_ref, gate_ref)
    kvc_ref[...] = kvc
    kvs_ref[...] = kvs
    kvw_ref[...] = kvw


def _inproj_seq_kernel(x_ref, g_ref, w_ref, *refs):
    q_ref, u_ref, gate_ref, kb_ref, vst_ref, vwt_ref, kvct_ref, kvst_ref, kvwt_ref = refs[-9:]
    kvc, kvs, kvw = _inproj_common(x_ref, g_ref, w_ref, q_ref, u_ref, gate_ref)
    kb_ref[...] = jnp.concatenate([kvs[:, 0:KV_LANES], kvw[:, 0:KV_LANES]], axis=1).astype(BF16)
    kvst = kvs.T
    kvwt = kvw.T
    kvct_ref[0, 0] = kvc.T
    kvst_ref[0, 0] = kvst
    for later in range(1, kvct_ref.shape[0]):
        kvct_ref[later, 0] = jnp.zeros(kvst.shape, F32)
        kvst_ref[later, 0] = jnp.zeros(kvst.shape, F32)
    kvwt_ref[0] = kvwt
    ones = jnp.ones((VT_ROWS - KV_LANES, kvst.shape[1]), F32)
    vst_ref[0] = jnp.concatenate([kvst[KV_LANES:KV_COLS], ones], axis=0).astype(BF16)
    vwt_ref[0] = jnp.concatenate([kvwt[KV_LANES:KV_COLS], ones], axis=0).astype(BF16)


def _inproj_rows(x, g, w):
    n = x.shape[0]
    tm = min(ROW_TILE, n)
    row = lambda c: pl.BlockSpec((tm, c), lambda i: (i, 0))
    full = lambda a: pl.BlockSpec(a.shape, lambda i: (0,) * a.ndim)
    outs = [(ATT_WIDTH, BF16), (POOL_WIDTH, F32), (LANE, F32), (KV_COLS, F32), (KV_COLS, F32), (KV_COLS, F32)]
    return pl.pallas_call(
        _inproj_rows_kernel,
        grid=(n // tm,),
        in_specs=[row(D_MODEL), full(g), full(w)],
        out_specs=[row(c) for c, _ in outs],
        out_shape=[jax.ShapeDtypeStruct((n, c), dt) for c, dt in outs],
        compiler_params=pltpu.CompilerParams(dimension_semantics=("parallel",), vmem_limit_bytes=VMEM_LIMIT),
    )(x, g, w)


def _inproj_seq(layer, x, g, w, all_shape, kvct_all, kvst_all, wb):
    n = x.shape[0]
    depth, b, _, t = all_shape
    assert (kvct_all is None) == (layer == 0)
    tm = min(ROW_TILE, t)
    tiles = t // tm
    assert wb % tm == 0 and wb <= t
    dropped = tiles - wb // tm
    row = lambda c: pl.BlockSpec((tm, c), lambda i: (i, 0))
    full = lambda a: pl.BlockSpec(a.shape, lambda i: (0,) * a.ndim)
    hbm = pl.BlockSpec(memory_space=pl.ANY)
    seq_t = lambda r: pl.BlockSpec((1, r, tm), lambda i: (i // tiles, 0, i % tiles))
    all_t = pl.BlockSpec((depth if layer == 0 else 1, 1, KV_COLS, tm), lambda i: (layer, i // tiles, 0, i % tiles))
    tail_t = pl.BlockSpec((1, KV_COLS, tm), lambda i: (i // tiles, 0, jnp.maximum(i % tiles - dropped, 0)))
    rows = [(ATT_WIDTH, BF16), (POOL_WIDTH, F32), (LANE, F32), (KV_COLS, BF16)]
    carried = [] if layer == 0 else [kvct_all, kvst_all]
    first_all = len(rows) + 2
    return pl.pallas_call(
        _inproj_seq_kernel,
        grid=(n // tm,),
        in_specs=[row(D_MODEL), full(g), full(w)] + [hbm] * len(carried),
        out_specs=[row(c) for c, _ in rows] + [seq_t(VT_ROWS), seq_t(VT_ROWS), all_t, all_t, tail_t],
        out_shape=[jax.ShapeDtypeStruct((n, c), dt) for c, dt in rows]
        + [jax.ShapeDtypeStruct((b, VT_ROWS, t), BF16), jax.ShapeDtypeStruct((b, VT_ROWS, t), BF16),
           jax.ShapeDtypeStruct(all_shape, F32), jax.ShapeDtypeStruct(all_shape, F32),
           jax.ShapeDtypeStruct((b, KV_COLS, wb), F32)],
        input_output_aliases={3 + k: first_all + k for k in range(len(carried))},
        compiler_params=pltpu.CompilerParams(dimension_semantics=("arbitrary",), vmem_limit_bytes=VMEM_LIMIT),
    )(x, g, w, *carried)


def _compress_tokens(x_of, w_ref, pe_ref, n_sub):
    outs = []
    for j in range(2):
        parts = [jnp.zeros((n_sub, KV_LANES), F32) for _ in range(CMP_RATIO)]
        for l in range(CMP_STRIDE):
            x = x_of(l, j)
            for r in range(CMP_RATIO):
                row = (j * CMP_RATIO + r) * CMP_STRIDE + l
                a = (x + pe_ref[row:row + 1, :]).astype(BF16)
                parts[r] = parts[r] + jnp.dot(a, w_ref[j, r, l], preferred_element_type=F32)
        comp = parts[0]
        for r in range(1, CMP_RATIO):
            comp = comp + pltpu.roll(parts[r], n_sub - r, axis=0)
        outs.append(comp)
    return outs


def _regroup_tokens(feat_tok, x_s, first_sub):
    pitch = x_s.shape[0] // CMP_STRIDE
    xt = feat_tok.T
    for r in range(xt.shape[0] // 8):
        l0 = (8 * r) % CMP_STRIDE
        sub = first_sub + (8 * r) // CMP_STRIDE
        x_s[pl.ds(l0 * pitch + sub, 8, stride=pitch), :] = xt[8 * r:8 * r + 8, :]


def _regrouped_rows(x_refs, n_sub):
    def x_of(l, j):
        pitch = x_refs[j].shape[0] // CMP_STRIDE
        return x_refs[j][l * pitch:l * pitch + n_sub, :]
    return x_of


def _compress_kernel(xt_ref, w_ref, pe_ref, ck_ref, cvt_ref, xk_s, xv_s):
    t = xt_ref.shape[3]
    n_sub = t // CMP_STRIDE
    for p in range(t // LANE):
        tok = slice(p * LANE, (p + 1) * LANE)
        _regroup_tokens(xt_ref[0, 0, 0:KV_LANES, tok], xk_s, p * (LANE // CMP_STRIDE))
        _regroup_tokens(xt_ref[0, 0, KV_LANES:KV_COLS, tok], xv_s, p * (LANE // CMP_STRIDE))
    ck, cv = _compress_tokens(_regrouped_rows((xk_s, xv_s), n_sub), w_ref, pe_ref, n_sub)
    ck_ref[0] = ck.astype(BF16)
    cvt_ref[0] = cv.T.astype(BF16)


def _compress_prompt(layer, kvct_all, w_bd, pe_rows):
    _, b, _, t = kvct_all.shape
    n_sub = t // CMP_STRIDE
    full = lambda a: pl.BlockSpec(a.shape, lambda i: (0,) * a.ndim)
    slabs = pltpu.VMEM((CMP_STRIDE * (n_sub + 8), KV_LANES), F32)
    return pl.pallas_call(
        _compress_kernel,
        grid=(b,),
        in_specs=[pl.BlockSpec((1, 1, KV_COLS, t), lambda i: (layer, i, 0, 0)), full(w_bd), full(pe_rows)],
        out_specs=[pl.BlockSpec((1, n_sub, KV_LANES), lambda i: (i, 0, 0)),
                   pl.BlockSpec((1, KV_LANES, n_sub), lambda i: (i, 0, 0))],
        out_shape=[jax.ShapeDtypeStruct((b, n_sub, KV_LANES), BF16), jax.ShapeDtypeStruct((b, KV_LANES, n_sub), BF16)],
        scratch_shapes=[slabs, slabs],
        compiler_params=pltpu.CompilerParams(dimension_semantics=("parallel",), vmem_limit_bytes=VMEM_LIMIT),
    )(kvct_all, w_bd, pe_rows)


def _split_dot(m, p):
    hi = p.astype(BF16)
    lo = (p - hi.astype(F32)).astype(BF16)
    return jnp.dot(m, hi, preferred_element_type=F32) + jnp.dot(m, lo, preferred_element_type=F32)


def _col_max(s):
    rows = s.shape[0]
    parts = [s[r:r + 64] for r in range(0, rows, 64)] if rows % 64 == 0 and rows > 64 else [s]
    while len(parts) > 1:
        parts = [jnp.maximum(a, b) for a, b in zip(parts[0::2], parts[1::2])] + ([parts[-1]] if len(parts) % 2 else [])
    return jnp.max(parts[0], axis=0, keepdims=True)


def _pool_mix(ext_ref, u, pos, wpool_ref, pscale_ref, rows):
    outs = []
    for k, w in enumerate(POOL_WINDOWS):
        ln = slice(k * POOL_GROUP_DIM, (k + 1) * POOL_GROUP_DIM)
        ws = ext_ref[POOL_HALO:POOL_HALO + rows, ln]
        for s in range(1, w):
            ws = ws + ext_ref[POOL_HALO - s:POOL_HALO - s + rows, ln]
        cnt = jnp.minimum(pos + 1, w).astype(F32)
        pooled = (ws / cnt - u[:, ln]).astype(BF16)
        o = jnp.dot(pooled, wpool_ref[k], preferred_element_type=F32) * pscale_ref[:, ln]
        outs.append(o.astype(BF16))
    return outs


def _prompt_attn_kernel(q_ref, gate_ref, u_ref, uh_ref, pst_ref, ksel_ref, kwin_ref, vst_ref, vwt_ref,
                        ck_ref, cvt_ref, delta_ref, mimp_ref, eblk_ref, wpool_ref, pscale_ref,
                        mix_ref,
                        sc_s, qa_s, qf_s, acc_s, m_s, ext_s):
    step = pl.program_id(1)
    n_blk = q_ref.shape[1] // TQ
    t_len = ksel_ref.shape[1]
    n_cmp = ck_ref.shape[1]
    n_sel = t_len // SEL_BLOCK
    cols = N_HEADS * TQ
    blk_per_q = TQ // SEL_BLOCK
    n_chain = cols // CHAIN_COLS
    chains = [slice(c * CHAIN_COLS, (c + 1) * CHAIN_COLS) for c in range(n_chain)]
    n_wc = WINDOW // TQ + 1
    w_delta = {0: DW0, n_wc - 2: DD0, n_wc - 1: DD0 + TQ}

    val = [dict() for _ in range(n_blk)]
    qb_of = lambda a: n_blk * step + a
    tok = lambda a: slice(a * TQ, (a + 1) * TQ)

    def stage_compressed(a):
        v, qb = val[a], qb_of(a)
        q = q_ref[0, tok(a), :].astype(F32)
        zeros = jnp.zeros((TQ, HEAD_DIM), F32)
        blocks = []
        for h in range(N_HEADS):
            halves = [zeros] * N_KV
            halves[h // GROUP] = q[:, h * HEAD_DIM:(h + 1) * HEAD_DIM]
            blocks.append(jnp.concatenate(halves, axis=1))
        qbd = jnp.concatenate(blocks, axis=0).astype(BF16)
        v["qbd"] = qbd
        sc = lax.dot_general(ck_ref[0], qbd, NT_DIMS, preferred_element_type=F32)
        crow = lax.broadcasted_iota(jnp.int32, (n_cmp, cols), 0)
        sc = jnp.where(crow >= 8 * qb + 8, NEG, sc)
        sc_s[a, 0:16, :] = jnp.zeros((16, cols), F32)
        sc_s[a, 16 + n_cmp:16 + n_cmp + 8, :] = jnp.zeros((8, cols), F32)
        sc_s[a, 16:16 + n_cmp, :] = sc
        near = pl.ds(pl.multiple_of(8 * qb, 8), DC_ROWS)
        sc_s[a, near, :] = sc_s[a, near, :] + delta_ref[DC0:DC0 + DC_ROWS, :]
        sc = sc_s[a, 16:16 + n_cmp, :]
        m_c = jnp.max(sc, axis=0, keepdims=True)
        p = jnp.exp2(sc - m_c)
        l_c = jnp.sum(p, axis=0, keepdims=True)
        pn = p * jnp.where(m_c > VALID_MIN, 1.0 / l_c, 0.0)
        v["o_cmp"] = jnp.dot(cvt_ref[0], pn.astype(BF16), preferred_element_type=F32)
        v["imp"] = []
        for g in range(N_KV):
            ps = pn[:, (g * GROUP) * TQ:(g * GROUP + 1) * TQ]
            for r in range(1, GROUP):
                ps = ps + pn[:, (g * GROUP + r) * TQ:(g * GROUP + r + 1) * TQ]
            v["imp"].append(_split_dot(mimp_ref[...], ps))

    def stage_window_scores(a):
        v, qb = val[a], qb_of(a)
        w_pos0 = [(qb - (n_wc - 1) + c) * TQ for c in range(n_wc)]
        w_k0 = [pl.multiple_of(jnp.maximum(p0, 0), TQ) for p0 in w_pos0]
        w_keys = [kwin_ref[0, pl.ds(k0, TQ), :] for k0 in w_k0]
        v["w_vt"] = [vwt_ref[0, :, pl.ds(k0, TQ)] for k0 in w_k0]
        v["s_win"] = []
        for cs in chains:
            s_w = []
            for c in range(n_wc):
                s = lax.dot_general(w_keys[c], v["qbd"][cs], NT_DIMS, preferred_element_type=F32)
                if c in w_delta:
                    s = s + delta_ref[w_delta[c]:w_delta[c] + TQ, cs]
                if c < n_wc - 1:
                    s = s + jnp.where(w_pos0[c] < 0, NEG, 0.0)
                s_w.append(s)
            v["s_win"].append(s_w)

    jidx = lax.broadcasted_iota(jnp.int32, (n_sel, TQ), 0)
    qi = lax.broadcasted_iota(jnp.int32, (n_sel, TQ), 1)
    pad = jnp.zeros((LANE - n_sel, TQ), F32)

    def stage_select(a):
        v, qb = val[a], qb_of(a)
        cur = blk_per_q * qb + qi // SEL_BLOCK
        forced = (jidx == 0) | (jidx == cur) | (jidx == cur - 1)
        prev_blk = blk_per_q * jnp.maximum(qb - 1, 0)
        ns_t = []
        for imp in v["imp"]:
            score = jnp.where(forced, jnp.inf, imp)
            score = jnp.where(jidx > cur, -jnp.inf, score)
            cnt = jnp.zeros((n_sel, TQ), jnp.int32)
            for jp in range(n_sel):
                row = score[jp:jp + 1, :]
                cnt = cnt + jnp.where(jidx > jp, jnp.where(row >= score, 1, 0), jnp.where(row > score, 1, 0))
            sel = (cnt < min(N_SELECT, n_sel)) & (jidx <= cur)
            ns_all = jnp.where(sel, 0.0, NEG)
            ns_far = jnp.where(jidx < prev_blk, ns_all, NEG)
            ns_t.append([jnp.concatenate([x, pad], axis=0).T.astype(BF16) for x in (ns_all, ns_far)])
        for h in range(N_HEADS):
            rows_h = slice(h * TQ, (h + 1) * TQ)
            qa_s[a, rows_h, 0:LANE] = v["qbd"][rows_h]
            qf_s[a, rows_h, 0:LANE] = v["qbd"][rows_h]
            qa_s[a, rows_h, LANE:2 * LANE] = ns_t[h // GROUP][0]
            qf_s[a, rows_h, LANE:2 * LANE] = ns_t[h // GROUP][1]
        m_s[a] = jnp.full((1, cols), NEG, F32)
        acc_s[a] = jnp.zeros((VT_ROWS, cols), F32)

    def online_update(a, cs, s, vt):
        m_old = m_s[a, :, cs]
        m_new = jnp.maximum(m_old, _col_max(s))
        alpha = jnp.exp2(m_old - m_new)
        p = jnp.exp2(s - m_new)
        acc_s[a, :, cs] = alpha * acc_s[a, :, cs] + jnp.dot(vt, p.astype(BF16), preferred_element_type=F32)
        m_s[a, :, cs] = m_new

    def keys_with_block(k0, n):
        return jnp.concatenate([ksel_ref[0, pl.ds(k0, n), :], eblk_ref[pl.ds(k0, n), :]], axis=1)

    def masked_scores(kcat, q_s, a, cs):
        return lax.dot_general(kcat, q_s[a, cs, :], NT_DIMS, preferred_element_type=F32)

    n_far = (TQ * jnp.maximum(qb_of(n_blk - 1) - 1, 0) + FAR_TILE - 1) // FAR_TILE

    def far_tiles(t0, n):
        k0 = [pl.multiple_of((t0 + i) * FAR_TILE, FAR_TILE) for i in range(n)]
        s_tiles = []
        for i in range(n):
            kcat = keys_with_block(k0[i], FAR_TILE)
            s_tiles.append([[masked_scores(kcat, qf_s, a, cs) for cs in chains] for a in range(n_blk)])
        for i in range(n):
            vt = vst_ref[0, :, pl.ds(k0[i], FAR_TILE)]
            for a in range(n_blk):
                for cs, s in zip(chains, s_tiles[i][a]):
                    online_update(a, cs, s, vt)

    def stage_near_scores(a):
        v, qb = val[a], qb_of(a)
        k_prev = pl.multiple_of(TQ * jnp.maximum(qb - 1, 0), TQ)
        k_diag = pl.multiple_of(qb * TQ, TQ)
        no_prev = jnp.where(qb == 0, NEG, 0.0)
        kcat_prev = keys_with_block(k_prev, TQ)
        kcat_diag = keys_with_block(k_diag, TQ)
        v["vt_near"] = jnp.concatenate([vst_ref[0, :, pl.ds(k_prev, TQ)], vst_ref[0, :, pl.ds(k_diag, TQ)]], axis=1)
        v["s_near"] = []
        for cs in chains:
            s_prev = masked_scores(kcat_prev, qa_s, a, cs) + delta_ref[DD0:DD0 + TQ, cs] + no_prev
            s_diag = masked_scores(kcat_diag, qa_s, a, cs) + delta_ref[DD0 + TQ:DD0 + 2 * TQ, cs]
            v["s_near"].append(jnp.concatenate([s_prev, s_diag], axis=0))

    def stage_near_update(a):
        v = val[a]
        for cs, s in zip(chains, v["s_near"]):
            online_update(a, cs, s, v["vt_near"])

    def stage_window(a):
        v = val[a]
        o_w = []
        for s_w in v["s_win"]:
            m_w = s_w[0].max(axis=0, keepdims=True)
            for s in s_w[1:]:
                m_w = jnp.maximum(m_w, s.max(axis=0, keepdims=True))
            o_c = jnp.zeros((VT_ROWS, CHAIN_COLS), F32)
            for s, vt in zip(s_w, v["w_vt"]):
                o_c = o_c + jnp.dot(vt, jnp.exp2(s - m_w).astype(BF16), preferred_element_type=F32)
            o_w.append(o_c)
        v["o_win"] = jnp.concatenate(o_w, axis=1)

    def stage_output(a):
        v, qb = val[a], qb_of(a)
        l_sel = acc_s[a, KV_LANES:KV_LANES + 1, :]
        o_sel = acc_s[a, 0:KV_LANES, :]
        l_win = v["o_win"][KV_LANES:KV_LANES + 1, :]
        o_win = v["o_win"][0:KV_LANES, :]
        gt = gate_ref[0, tok(a), :].T
        gate_row = lambda x: jnp.concatenate([gt[N_BRANCH * h + x:N_BRANCH * h + x + 1, :] for h in range(N_HEADS)],
                                             axis=1)
        o_t = v["o_cmp"] * gate_row(0) + o_sel * (gate_row(1) / l_sel) + o_win * (gate_row(2) / l_win)
        pieces = []
        for h in range(N_HEADS):
            g = h // GROUP
            pieces.append(o_t[g * HEAD_DIM:(g + 1) * HEAD_DIM, h * TQ:(h + 1) * TQ])
        mix_ref[0, tok(a), 0:ATT_WIDTH] = jnp.concatenate(pieces, axis=0).T.astype(BF16)
        u = u_ref[0, tok(a), :]
        ext = ext_s.at[a]
        if a == 0:
            ext[0:POOL_HALO, :] = jnp.where(qb == 0, pst_ref[0], uh_ref[0])
        else:
            ext[0:POOL_HALO, :] = u_ref[0, a * TQ - POOL_HALO:a * TQ, :]
        ext[POOL_HALO:POOL_HALO + TQ, :] = u
        pos = qb * TQ + lax.broadcasted_iota(jnp.int32, (TQ, 1), 0)
        for k, o in enumerate(_pool_mix(ext, u, pos, wpool_ref, pscale_ref, TQ)):
            mix_ref[0, tok(a), ATT_WIDTH + k * POOL_GROUP_DIM:ATT_WIDTH + (k + 1) * POOL_GROUP_DIM] = o

    def run(stage):
        for a in range(n_blk):
            stage(a)

    run(stage_compressed)
    run(stage_window_scores)
    run(stage_select)

    odd = lax.rem(n_far, 2)

    @pl.when(odd == 1)
    def _():
        far_tiles(0, 1)

    def pair_body(i, carry):
        far_tiles(odd + 2 * i, 2)
        return carry

    lax.fori_loop(0, lax.div(n_far, 2), pair_body, 0)

    run(stage_near_scores)
    run(stage_near_update)
    run(stage_window)
    run(stage_output)


def _prompt_attention(q, gates, u, pool_prev, kb, vst, vwt, ck, cvt, delta, mimp, eblk, wpool, pscale):
    b, t, _ = q.shape
    assert t % FAR_TILE == 0 and t >= WINDOW + TQ and t // SEL_BLOCK <= LANE
    n_cmp = ck.shape[1]
    cols = N_HEADS * TQ
    nq = PROMPT_QBLOCKS if (t // TQ) % PROMPT_QBLOCKS == 0 else 1
    rows = nq * TQ
    halo_per_step = rows // POOL_HALO
    full = lambda a: pl.BlockSpec(a.shape, lambda i, j: (0,) * a.ndim)
    qblk = lambda c: pl.BlockSpec((1, rows, c), lambda i, j: (i, j, 0))
    seq = lambda c: pl.BlockSpec((1, t, KV_LANES), lambda i, j: (i, 0, c))
    per_b = lambda a: pl.BlockSpec((1,) + a.shape[1:], lambda i, j: (i,) + (0,) * (a.ndim - 1))
    return pl.pallas_call(
        _prompt_attn_kernel,
        grid=(b, t // rows),
        in_specs=[qblk(ATT_WIDTH), qblk(LANE), qblk(POOL_WIDTH),
                  pl.BlockSpec((1, POOL_HALO, POOL_WIDTH), lambda i, j: (i, jnp.maximum(j * halo_per_step - 1, 0), 0)),
                  per_b(pool_prev), seq(0), seq(1), per_b(vst), per_b(vwt), per_b(ck), per_b(cvt),
                  full(delta), full(mimp), full(eblk), full(wpool), full(pscale)],
        out_specs=pl.BlockSpec((1, rows, D_MODEL), lambda i, j: (i, j, 0)),
        out_shape=jax.ShapeDtypeStruct((b, t, D_MODEL), BF16),
        scratch_shapes=[pltpu.VMEM((nq, 16 + n_cmp + 8, cols), F32),
                        pltpu.VMEM((nq, cols, 2 * LANE), BF16), pltpu.VMEM((nq, cols, 2 * LANE), BF16),
                        pltpu.VMEM((nq, VT_ROWS, cols), F32), pltpu.VMEM((nq, 1, cols), F32),
                        pltpu.VMEM((nq, POOL_HALO + TQ, POOL_WIDTH), F32)],
        compiler_params=pltpu.CompilerParams(dimension_semantics=("parallel", "arbitrary"),
                                             vmem_limit_bytes=VMEM_LIMIT),
    )(q, gates, u, u, pool_prev, kb, kb, vst, vwt, ck, cvt, delta, mimp, eblk, wpool, pscale)


SAMPLE_SEQS = 2
SD_LAST = 0
SD_NEW = LANE
SD_EDGE = 2 * LANE
SD_CMP = 3 * LANE


def _sample_attn_kernel(pt_ref, q_ref, gate_ref, u_ref, kvs_ref, kvw_ref, swin_ref, spool_ref,
                        ccache_ref, scache_ref, *refs, layer):
    (wcmp_ref, pe_ref, delta_ref, mimp_ref, expand_ref, wpool_ref, pscale_ref,
     mix_ref, nwin_ref, npool_ref, cbuf, sbuf, xk_s, xv_s, sem, ext_s) = refs[-16:]
    b = pl.program_id(0)
    nb = pl.num_programs(0)
    n_pages = pt_ref.shape[1]
    past = n_pages * PAGE_SIZE
    n_sub = past // CMP_STRIDE
    n_seq, tq = q_ref.shape[0], q_ref.shape[1]
    wb = swin_ref.shape[3]
    n_sel = past // SEL_BLOCK + 1
    sub_per_page = PAGE_SIZE // CMP_STRIDE
    slot = lax.rem(b, 2)

    def page_copies(step, slt):
        cps = []
        for s in range(n_seq):
            for p in range(n_pages):
                pg = pt_ref[step * n_seq + s, p]
                dst = (slice(None), pl.ds(p * PAGE_SIZE, PAGE_SIZE))
                cps.append(pltpu.make_async_copy(ccache_ref.at[layer, pg], cbuf.at[slt, s].at[dst], sem.at[0, slt]))
                cps.append(pltpu.make_async_copy(scache_ref.at[layer, pg], sbuf.at[slt, s].at[dst], sem.at[1, slt]))
        return cps

    def start_all(cps):
        for k, cp in enumerate(cps):
            cp.start(priority=k % 2)

    @pl.when(b == 0)
    def _():
        start_all(page_copies(0, 0))

    @pl.when(b + 1 < nb)
    def _():
        start_all(page_copies(b + 1, 1 - slot))

    for cp in page_copies(b, slot):
        cp.wait()

    val = [dict() for _ in range(n_seq)]

    def scores(qbd, k_rows):
        return lax.dot_general(qbd, k_rows, NT_DIMS, preferred_element_type=F32)

    def scores_t(qbd, k_t):
        return jnp.dot(qbd, k_t.astype(BF16), preferred_element_type=F32)

    def pad_rows(x):
        return jnp.concatenate([x, jnp.zeros((LANE - tq, KV_LANES), F32)], axis=0)

    pad_new = lambda x: pad_rows(x).astype(BF16)
    d_last = delta_ref[:, SD_LAST:SD_LAST + LANE]
    d_new = delta_ref[:, SD_NEW:SD_NEW + LANE]

    def softmax_pv(s_old, vt_old, s_new, v_new):
        m = jnp.maximum(s_old.max(axis=1, keepdims=True), s_new.max(axis=1, keepdims=True))
        p_old = jnp.exp2(s_old - m)
        p_new = jnp.exp2(s_new - m)
        l = jnp.sum(p_old, axis=1, keepdims=True) + jnp.sum(p_new, axis=1, keepdims=True)
        o = (lax.dot_general(p_old.astype(BF16), vt_old.astype(BF16), NT_DIMS, preferred_element_type=F32)
             + jnp.dot(p_new.astype(BF16), v_new, preferred_element_type=F32))
        return o, l

    def stage_scores(s):
        v = val[s]
        q = q_ref[s].astype(F32)
        zeros = jnp.zeros((tq, HEAD_DIM), F32)
        blocks = []
        for h in range(N_HEADS):
            halves = [zeros] * N_KV
            halves[h // GROUP] = q[:, h * HEAD_DIM:(h + 1) * HEAD_DIM]
            blocks.append(jnp.concatenate(halves, axis=1))
        qbd = jnp.concatenate(blocks, axis=0).astype(BF16)
        v["qbd"] = qbd
        kvw_new = kvw_ref[s]
        s_first = scores_t(qbd, swin_ref[0, s, 0:KV_LANES, 0:LANE]) + delta_ref[:, SD_EDGE:SD_EDGE + LANE]
        s_mid = scores_t(qbd, swin_ref[0, s, 0:KV_LANES, LANE:wb - LANE])
        s_lastw = scores_t(qbd, swin_ref[0, s, 0:KV_LANES, wb - LANE:wb]) + d_last
        v["s_wold"] = jnp.concatenate([s_first, s_mid, s_lastw], axis=1)
        v["s_wnew"] = scores(qbd, pad_new(kvw_new[:, 0:KV_LANES])) + d_new
        kvs_new = kvs_ref[s]
        s_far = scores_t(qbd, sbuf[slot, s, 0:KV_LANES, 0:past - LANE])
        s_last = scores_t(qbd, sbuf[slot, s, 0:KV_LANES, past - LANE:past]) + d_last
        v["s_past"] = jnp.concatenate([s_far, s_last], axis=1)
        v["s_new"] = scores(qbd, pad_new(kvs_new[:, 0:KV_LANES])) + d_new

    def stage_regroup(s):
        for p in range(n_pages):
            tok = slice(p * PAGE_SIZE, (p + 1) * PAGE_SIZE)
            _regroup_tokens(cbuf[slot, s, 0:KV_LANES, tok], xk_s.at[s], p * sub_per_page)
            _regroup_tokens(cbuf[slot, s, KV_LANES:KV_COLS, tok], xv_s.at[s], p * sub_per_page)

    def stage_window(s):
        v = val[s]
        kvw_new = kvw_ref[s]
        v["o_win"], v["l_win"] = softmax_pv(v["s_wold"], swin_ref[0, s, KV_LANES:KV_COLS, :],
                                            v["s_wnew"], pad_new(kvw_new[:, KV_LANES:KV_COLS]))
        shifted = pltpu.roll(swin_ref[0, s], wb - tq, axis=1)
        new_t = jnp.concatenate([pad_rows(kvw_new[:, 0:KV_LANES]).T, pad_rows(kvw_new[:, KV_LANES:KV_COLS]).T], axis=0)
        new_t = pltpu.roll(new_t, LANE - tq, axis=1)
        lane = lax.broadcasted_iota(jnp.int32, (KV_COLS, LANE), 1)
        nwin_ref[0, s, :, 0:wb - LANE] = shifted[:, 0:wb - LANE]
        nwin_ref[0, s, :, wb - LANE:wb] = jnp.where(lane >= LANE - tq, new_t, shifted[:, wb - LANE:wb])
        for later in range(1, nwin_ref.shape[0]):
            nwin_ref[later, s] = jnp.zeros((KV_COLS, wb), F32)

    def stage_compress(s):
        v = val[s]
        ck, cv = _compress_tokens(_regrouped_rows((xk_s.at[s], xv_s.at[s]), n_sub), wcmp_ref, pe_ref, n_sub)
        v["cv"] = cv.astype(BF16)
        v["sc"] = scores(v["qbd"], ck.astype(BF16)) + delta_ref[:, SD_CMP:SD_CMP + n_sub]

    def stage_compressed(s):
        v = val[s]
        sc = v["sc"]
        m_c = jnp.max(sc, axis=1, keepdims=True)
        p = jnp.exp2(sc - m_c)
        l_c = jnp.sum(p, axis=1, keepdims=True)
        pn = p * jnp.where(m_c > VALID_MIN, 1.0 / l_c, 0.0)
        v["o_cmp"] = jnp.dot(pn.astype(BF16), v["cv"], preferred_element_type=F32)
        v["imp"] = []
        for g in range(N_KV):
            ps = pn[g * GROUP * tq:(g * GROUP + 1) * tq, :]
            for r in range(1, GROUP):
                ps = ps + pn[(g * GROUP + r) * tq:(g * GROUP + r + 1) * tq, :]
            hi = ps.astype(BF16)
            lo = (ps - hi.astype(F32)).astype(BF16)
            v["imp"].append(jnp.dot(hi, mimp_ref[...], preferred_element_type=F32)
                            + jnp.dot(lo, mimp_ref[...], preferred_element_type=F32))

    jidx = lax.broadcasted_iota(jnp.int32, (tq, LANE), 1)
    cur = (past + lax.broadcasted_iota(jnp.int32, (tq, LANE), 0)) // SEL_BLOCK
    forced = (jidx == 0) | (jidx == cur) | (jidx == cur - 1)

    def stage_select(s):
        v = val[s]
        sel_g = []
        for imp in v["imp"]:
            score = jnp.where(forced, jnp.inf, imp)
            score = jnp.where((jidx > cur) | (jidx >= n_sel), -jnp.inf, score)
            cnt = jnp.zeros((tq, LANE), jnp.int32)
            for jp in range(n_sel):
                col = score[:, jp:jp + 1]
                cnt = cnt + jnp.where(jidx > jp, jnp.where(col >= score, 1, 0), jnp.where(col > score, 1, 0))
            sel = (cnt < min(N_SELECT, n_sel)) & (jidx <= cur) & (jidx < n_sel)
            sel_g.append(jnp.where(sel, 1.0, 0.0))
        sel_rows = jnp.concatenate([sel_g[h // GROUP] for h in range(N_HEADS)], axis=0).astype(BF16)
        v["mask"] = jnp.dot(sel_rows, expand_ref[...], preferred_element_type=F32)

    def stage_selected(s):
        v = val[s]
        s_past = jnp.where(v["mask"] > 0.5, v["s_past"], NEG)
        v["o_sel"], v["l_sel"] = softmax_pv(s_past, sbuf[slot, s, KV_LANES:KV_COLS, :], v["s_new"],
                                            pad_new(kvs_ref[s][:, KV_LANES:KV_COLS]))

    def stage_output(s):
        v = val[s]
        gates = gate_ref[s]
        gate_col = lambda x: jnp.concatenate([gates[:, N_BRANCH * h + x:N_BRANCH * h + x + 1] for h in range(N_HEADS)],
                                             axis=0)
        o = (v["o_cmp"] * gate_col(0) + v["o_sel"] * (gate_col(1) / v["l_sel"])
             + v["o_win"] * (gate_col(2) / v["l_win"]))
        pieces = []
        for h in range(N_HEADS):
            g = h // GROUP
            pieces.append(o[h * tq:(h + 1) * tq, g * HEAD_DIM:(g + 1) * HEAD_DIM])
        mix_ref[s, :, 0:ATT_WIDTH] = jnp.concatenate(pieces, axis=1).astype(BF16)
        u = u_ref[s]
        ext = ext_s.at[s]
        ext[0:1, :] = jnp.zeros((1, POOL_WIDTH), F32)
        ext[1:POOL_HALO, :] = spool_ref[0, s]
        ext[POOL_HALO:POOL_HALO + tq, :] = u
        pos = past + lax.broadcasted_iota(jnp.int32, (tq, 1), 0)
        for k, o_k in enumerate(_pool_mix(ext, u, pos, wpool_ref, pscale_ref, tq)):
            mix_ref[s, :, ATT_WIDTH + k * POOL_GROUP_DIM:ATT_WIDTH + (k + 1) * POOL_GROUP_DIM] = o_k
        npool_ref[s] = ext[POOL_HALO + tq - POOL_STATE:POOL_HALO + tq, :]

    for stage in (stage_scores, stage_regroup, stage_window, stage_compress, stage_compressed, stage_select,
                  stage_selected, stage_output):
        for s in range(n_seq):
            stage(s)


def _sample_attention(layer, page_table, q, gates, u, kvs, kvw, swin_t, state_pool, ccache_t, scache_t, nwin_all,
                      wcmp, pe_rows, delta, mimp, expand, wpool, pscale):
    nb, tq, _ = q.shape
    assert (nwin_all is None) == (layer == 0)
    carried = [] if layer == 0 else [nwin_all]
    depth = swin_t.shape[0]
    n_pages = page_table.shape[1]
    past = n_pages * PAGE_SIZE
    n_sub = past // CMP_STRIDE
    wb = swin_t.shape[3]
    assert wb == WINDOW and wb >= 3 * LANE and tq <= CMP_STRIDE and POOL_HALO + tq - POOL_STATE >= 0
    assert past // SEL_BLOCK + 1 <= LANE and past % LANE == 0 and n_sub <= LANE
    ns = SAMPLE_SEQS if nb % SAMPLE_SEQS == 0 else 1
    full = lambda a: pl.BlockSpec(a.shape, lambda i, pt: (0,) * a.ndim)
    per_b = lambda c: pl.BlockSpec((ns, tq, c), lambda i, pt: (i, 0, 0))
    layer_b = lambda a: pl.BlockSpec((1, ns) + a.shape[2:], lambda i, pt: (layer, i) + (0,) * (a.ndim - 2))
    hbm = pl.BlockSpec(memory_space=pl.ANY)
    grid_spec = pltpu.PrefetchScalarGridSpec(
        num_scalar_prefetch=1,
        grid=(nb // ns,),
        in_specs=[per_b(ATT_WIDTH), per_b(LANE), per_b(POOL_WIDTH), per_b(KV_COLS), per_b(KV_COLS),
                  layer_b(swin_t), layer_b(state_pool), hbm, hbm] + [hbm] * len(carried)
        + [full(wcmp), full(pe_rows), full(delta), full(mimp), full(expand), full(wpool), full(pscale)],
        out_specs=[pl.BlockSpec((ns, tq, D_MODEL), lambda i, pt: (i, 0, 0)),
                   pl.BlockSpec((depth if layer == 0 else 1, ns, KV_COLS, wb), lambda i, pt: (layer, i, 0, 0)),
                   pl.BlockSpec((ns, POOL_STATE, POOL_WIDTH), lambda i, pt: (i, 0, 0))],
        scratch_shapes=[pltpu.VMEM((2, ns, KV_COLS, past), F32), pltpu.VMEM((2, ns, KV_COLS, past), F32),
                        pltpu.VMEM((ns, CMP_STRIDE * (n_sub + 8), KV_LANES), F32),
                        pltpu.VMEM((ns, CMP_STRIDE * (n_sub + 8), KV_LANES), F32),
                        pltpu.SemaphoreType.DMA((2, 2)), pltpu.VMEM((ns, POOL_HALO + tq, POOL_WIDTH), F32)],
    )
    return pl.pallas_call(
        functools.partial(_sample_attn_kernel, layer=layer),
        grid_spec=grid_spec,
        out_shape=[jax.ShapeDtypeStruct((nb, tq, D_MODEL), BF16),
                   jax.ShapeDtypeStruct(swin_t.shape, F32),
                   jax.ShapeDtypeStruct((nb, POOL_STATE, POOL_WIDTH), F32)],
        input_output_aliases={10: 1} if carried else {},
        compiler_params=pltpu.CompilerParams(dimension_semantics=("arbitrary",), vmem_limit_bytes=VMEM_LIMIT),
    )(page_table, q, gates, u, kvs, kvw, swin_t, state_pool, ccache_t, scache_t, *carried,
      wcmp, pe_rows, delta, mimp, expand, wpool, pscale)


def _ffn_kernel(x_ref, mix_ref, wout_ref, gn_ref, win_ref, wd_ref, gf_ref, o_ref, *, final_norm):
    xm = x_ref[...] + jnp.dot(mix_ref[...], wout_ref[...], preferred_element_type=F32)
    h = _rms(xm, gn_ref[...]).astype(BF16)
    d_ff = wd_ref.shape[0]
    cuts = list(range(0, d_ff, FFN_SLICE)) + [d_ff]
    slices = list(zip(cuts[:-1], cuts[1:]))

    def gate_up(a, b):
        return (jnp.dot(h, win_ref[:, a:b], preferred_element_type=F32),
                jnp.dot(h, win_ref[:, d_ff + a:d_ff + b], preferred_element_type=F32))

    y = xm
    pending = gate_up(*slices[0])
    for j, (a, b) in enumerate(slices):
        gate, up = pending
        if j + 1 < len(slices):
            pending = gate_up(*slices[j + 1])
        act = (gate * _sigmoid(gate) * up).astype(BF16)
        y = y + jnp.dot(act, wd_ref[a:b, :], preferred_element_type=F32)
    o_ref[...] = _rms(y, gf_ref[...]) if final_norm else y


def _ffn(x, mix, wout, gn, wffn_in, wffn_out, gf, final_norm):
    n = x.shape[0]
    tm = min(ROW_TILE, n)
    assert n % tm == 0 and wffn_out.shape[0] % LANE == 0
    once = lambda a: pl.BlockSpec(a.shape, lambda i: (0,) * a.ndim, pipeline_mode=pl.Buffered(1))
    row = lambda cdim: pl.BlockSpec((tm, cdim), lambda i: (i, 0))
    return pl.pallas_call(
        functools.partial(_ffn_kernel, final_norm=final_norm),
        grid=(n // tm,),
        in_specs=[row(D_MODEL), row(D_MODEL), once(wout), once(gn), once(wffn_in), once(wffn_out), once(gf)],
        out_specs=row(D_MODEL),
        out_shape=jax.ShapeDtypeStruct((n, D_MODEL), F32),
        compiler_params=pltpu.CompilerParams(dimension_semantics=("parallel",), vmem_limit_bytes=VMEM_LIMIT),
    )(x, mix, wout, gn, wffn_in, wffn_out, gf)


def _importance_matrix(n_sel, n_cmp_rows, n_cmp):
    spb = SEL_BLOCK // CMP_STRIDE
    j = np.arange(n_sel)[:, None]
    c = np.arange(n_cmp_rows)[None, :]
    return ((c >= spb * j - (CMP_RATIO - 1)) & (c <= spb * j + spb - 1) & (c < n_cmp)).astype(np.float32)


def kernel(x_prompt, x_sample, cache_cmp, cache_sel, state_win, state_pool, page_table, rel_bias, norm_mix, norm_ffn,
           norm_final, w_in, w_out, cmp_pos, w_cmp, w_pool, pool_scale, w_ffn_in, w_ffn_out):
    bp, t, _ = x_prompt.shape
    bs, tq, _ = x_sample.shape
    depth = w_in.shape[0]
    n_pages = page_table.shape[1]
    past = n_pages * PAGE_SIZE
    wb = state_win.shape[2]
    _check_far_bucket(max(t, past + tq) + WINDOW)

    rb_prompt = jnp.repeat(rel_bias.astype(F32), TQ, axis=1)
    delta_p = _bias_tiles(_prompt_bucket_table(), rb_prompt)
    rb_sample = jnp.pad(jnp.repeat(rel_bias.astype(F32), tq, axis=1), ((0, 0), (0, LANE - N_HEADS * tq)))
    delta_s = _bias_tiles(_sample_bucket_table(past, tq, wb), rb_sample).T[:N_HEADS * tq]

    n_sub_p = t // CMP_STRIDE
    mimp_p = jnp.asarray(_importance_matrix(t // SEL_BLOCK, n_sub_p, n_sub_p - CMP_RATIO + 1), BF16)
    eblk_p = jnp.asarray((np.arange(t)[:, None] // SEL_BLOCK == np.arange(LANE)[None, :]).astype(np.float32), BF16)
    n_sub_s = past // CMP_STRIDE
    mimp_s = jnp.asarray(_importance_matrix(LANE, n_sub_s, n_sub_s - CMP_RATIO + 1).T, BF16)
    expand = jnp.asarray((np.arange(LANE)[:, None] == np.arange(past)[None, :] // SEL_BLOCK).astype(np.float32), BF16)

    pool_zero = jnp.zeros((bp, POOL_HALO, POOL_WIDTH), F32)
    feature_major = lambda a: jnp.moveaxis(a, 2, -1).reshape(a.shape[0], a.shape[1], KV_COLS, a.shape[2])
    token_major = lambda a: jnp.moveaxis(a.reshape(a.shape[:2] + (2, N_KV, HEAD_DIM, a.shape[3])), -1, 2)
    ccache_t = feature_major(cache_cmp)
    scache_t = feature_major(cache_sel)
    swin_t = feature_major(state_win)
    kvct_all = kvst_all = nwin_all = None

    xp = x_prompt.reshape(bp * t, D_MODEL)
    xs = x_sample.reshape(bs * tq, D_MODEL)
    outs = {k: [] for k in ("p_win", "p_pool", "s_cmp", "s_sel", "s_pool")}
    kv_shape = lambda b, n: (b, n, 2, N_KV, HEAD_DIM)
    off_g = ATT_WIDTH + N_BRANCH * KV_COLS
    for l in range(depth):
        wl = w_in[l]
        w_all = jnp.concatenate([wl[:, :ATT_WIDTH] * (HEAD_DIM ** -0.5 * LOG2E), wl[:, ATT_WIDTH:off_g],
                                 wl[:, off_g + GATE_COLS:], wl[:, off_g:off_g + GATE_COLS],
                                 jnp.zeros((D_MODEL, LANE - GATE_COLS), F32)], axis=1).astype(BF16)
        g_mix = norm_mix[l].reshape(1, D_MODEL)
        g_ffn = norm_ffn[l].reshape(1, D_MODEL)
        g_fin = norm_final.reshape(1, D_MODEL)
        wc = w_cmp[l].reshape(2, CMP_RATIO, CMP_STRIDE, HEAD_DIM, HEAD_DIM)
        zero = jnp.zeros_like(wc)
        w_bd = jnp.concatenate([jnp.concatenate([wc, zero], axis=-1), jnp.concatenate([zero, wc], axis=-1)],
                               axis=-2).astype(BF16)
        pe = cmp_pos[l].reshape(CMP_RATIO, CMP_STRIDE, 2, HEAD_DIM).transpose(2, 0, 1, 3)
        pe_rows = jnp.tile(pe.reshape(2 * CMP_RATIO * CMP_STRIDE, HEAD_DIM), (1, N_KV)).astype(F32)
        wpool = w_pool[l].astype(BF16)
        pscale = pool_scale[l].reshape(1, POOL_WIDTH)
        wout = w_out[l].astype(BF16)
        wffn_in = w_ffn_in[l].astype(BF16)
        wffn_out = w_ffn_out[l].astype(BF16)
        last = l == depth - 1

        q, u, gates, kb, vst, vwt, kvct_all, kvst_all, kvwt = _inproj_seq(
            l, xp, g_mix, w_all, (depth, bp, KV_COLS, t), kvct_all, kvst_all, wb)
        ck, cvt = _compress_prompt(l, kvct_all, w_bd, pe_rows)
        mix = _prompt_attention(q.reshape(bp, t, ATT_WIDTH), gates.reshape(bp, t, LANE), u.reshape(bp, t, POOL_WIDTH),
                                pool_zero, kb.reshape(bp, t, KV_COLS), vst, vwt, ck, cvt, delta_p, mimp_p, eblk_p,
                                wpool, pscale)
        xp = _ffn(xp, mix.reshape(bp * t, D_MODEL), wout, g_ffn, wffn_in, wffn_out, g_fin, last)
        outs["p_win"].append(kvwt)
        outs["p_pool"].append(u.reshape(bp, t, POOL_WIDTH)[:, t - POOL_STATE:])

        q, u, gates, kvc, kvs, kvw = _inproj_rows(xs, g_mix, w_all)
        mix, nwin_all, npool = _sample_attention(
            l, page_table, q.reshape(bs, tq, ATT_WIDTH), gates.reshape(bs, tq, LANE), u.reshape(bs, tq, POOL_WIDTH),
            kvs.reshape(bs, tq, KV_COLS), kvw.reshape(bs, tq, KV_COLS), swin_t, state_pool, ccache_t, scache_t,
            nwin_all, w_bd, pe_rows, delta_s, mimp_s, expand, wpool, pscale)
        xs = _ffn(xs, mix.reshape(bs * tq, D_MODEL), wout, g_ffn, wffn_in, wffn_out, g_fin, last)
        outs["s_cmp"].append(kvc.reshape(kv_shape(bs, tq)))
        outs["s_sel"].append(kvs.reshape(kv_shape(bs, tq)))
        outs["s_pool"].append(npool)

    return (xp.reshape(bp, t, D_MODEL), xs.reshape(bs, tq, D_MODEL),
            token_major(kvct_all), token_major(kvst_all), token_major(jnp.stack(outs["p_win"])),
            jnp.stack(outs["p_pool"]), jnp.stack(outs["s_cmp"]), jnp.stack(outs["s_sel"]), token_major(nwin_all),
            jnp.stack(outs["s_pool"]))
```

```python
import functools
import math

import numpy as np
import jax
import jax.numpy as jnp
from jax import lax
from jax.experimental import pallas as pl
from jax.experimental.pallas import tpu as pltpu

D_MODEL = 1024
HEAD_DIM = 64
N_HEADS = 8
N_KV = 2
GROUP = N_HEADS // N_KV
ATT_WIDTH = N_HEADS * HEAD_DIM
POOL_WIDTH = D_MODEL - ATT_WIDTH
KV_COLS = 2 * N_KV * HEAD_DIM
KV_LANES = N_KV * HEAD_DIM
VT_ROWS = KV_LANES + 16
CMP_LEN = 32
CMP_STRIDE = 16
CMP_RATIO = CMP_LEN // CMP_STRIDE
SEL_BLOCK = 64
N_SELECT = 16
WINDOW = 512
N_BRANCH = 3
GATE_COLS = N_HEADS * N_BRANCH
POOL_WINDOWS = (2, 4, 8, 16)
POOL_GROUP_DIM = POOL_WIDTH // len(POOL_WINDOWS)
POOL_STATE = max(POOL_WINDOWS) - 1
POOL_HALO = 16
N_BUCKETS = 32
MAX_DISTANCE = 128
PAGE_SIZE = 128
EPS = 1e-6
NEG = -1e30
VALID_MIN = -1e29

LANE = 128
TQ = 128
PROMPT_QBLOCKS = 2
FAR_TILE = 512
CHAIN_COLS = 256
LOG2E = 1.4426950408889634
ROW_TILE = 512
FFN_SLICE = 512
VMEM_LIMIT = 52 * 1024 * 1024

F32 = jnp.float32
BF16 = jnp.bfloat16
NT_DIMS = (((1,), (1,)), ((), ()))


def _bucket_np(dist):
    n = np.maximum(dist, 0)
    max_exact = N_BUCKETS // 2
    nf = np.maximum(n, 1).astype(np.float32)
    large = max_exact + (np.log(nf / max_exact) / math.log(MAX_DISTANCE / max_exact)
                         * (N_BUCKETS - max_exact)).astype(np.int32)
    return np.where(n < max_exact, n, np.minimum(large, N_BUCKETS - 1)).astype(np.int32)


def _bucket_or_masked(dist, valid):
    return np.where(valid, _bucket_np(dist), -1).astype(np.int32)


FAR_DIST = TQ + 1


def _check_far_bucket(max_dist):
    assert (_bucket_np(np.arange(FAR_DIST, max_dist + 1)) == N_BUCKETS - 1).all()


def _bias_tile_kernel(bucket_ref, rb_ref, out_ref):
    bucket = bucket_ref[...]
    acc = jnp.zeros(bucket.shape, F32)
    for b in range(N_BUCKETS):
        acc = acc + jnp.where(bucket == b, rb_ref[b:b + 1, :], 0.0)
    acc = (acc - rb_ref[N_BUCKETS - 1:N_BUCKETS, :]) * LOG2E
    out_ref[...] = jnp.where(bucket < 0, NEG, acc)


def _bias_tiles(bucket_np, rb_cols):
    rows, cols = bucket_np.shape
    return pl.pallas_call(
        _bias_tile_kernel,
        out_shape=jax.ShapeDtypeStruct((rows, cols), F32),
    )(jnp.asarray(bucket_np), rb_cols)


DD0 = 0
DC0 = 2 * TQ
DC_ROWS = 24
DW0 = DC0 + 32


def _prompt_bucket_table():
    i = np.arange(TQ)[None, :]
    kk = np.arange(2 * TQ)[:, None]
    d = i + TQ - kk
    dd = _bucket_or_masked(d, d >= 0)
    cc = np.arange(32)[:, None] - 16
    d = i - CMP_STRIDE * cc - (CMP_LEN - 1)
    dc = _bucket_or_masked(d, (d >= 0) & (cc < 8))
    j = np.arange(TQ)[:, None]
    d = i + WINDOW - j
    dw = _bucket_or_masked(d, (d >= 0) & (d < WINDOW))
    tab = np.concatenate([dd, dc, dw], axis=0)
    return np.tile(tab, (1, N_HEADS))


def _sample_bucket_table(past, tq, wb):
    col = np.arange(LANE)
    t = (col % tq)[None, :]
    colok = (col < N_HEADS * tq)[None, :]
    kk = np.arange(LANE)[:, None]
    d = LANE + t - kk
    dlast = _bucket_or_masked(d, colok & (d >= 0))
    d = t - kk
    dnew = _bucket_or_masked(d, colok & (d >= 0) & (kk < tq))
    d = wb + t - kk
    dedge = _bucket_or_masked(d, colok & (d >= 0) & (d < WINDOW))
    nsub = past // CMP_STRIDE
    c = np.arange(nsub)[:, None]
    d = past + t - CMP_STRIDE * c - (CMP_LEN - 1)
    dcmp = _bucket_or_masked(d, colok & (d >= 0) & (c < nsub - CMP_RATIO + 1))
    return np.concatenate([dlast, dnew, dedge, dcmp], axis=0)


IN_Q = 0
IN_KV = ATT_WIDTH
IN_U = IN_KV + N_BRANCH * KV_COLS
IN_G = IN_U + POOL_WIDTH
IN_COLS_PAD = IN_G + LANE


def _rms(x, g):
    return x * lax.rsqrt(jnp.mean(x * x, axis=-1, keepdims=True) + EPS) * g


def _sigmoid(x):
    return 1.0 / (1.0 + jnp.exp(-x))


def _inproj_common(x_ref, g_ref, w_ref, q_ref, u_ref, gate_ref):
    h = _rms(x_ref[...], g_ref[...]).astype(BF16)
    proj = lambda a, b: jnp.dot(h, w_ref[:, a:b], preferred_element_type=F32)
    q_ref[...] = proj(IN_Q, IN_KV).astype(BF16)
    kv = proj(IN_KV, IN_U)
    u_ref[...] = proj(IN_U, IN_G)
    gate_ref[...] = _sigmoid(proj(IN_G, IN_COLS_PAD))
    return [kv[:, i * KV_COLS:(i + 1) * KV_COLS] for i in range(N_BRANCH)]


def _inproj_rows_kernel(x_ref, g_ref, w_ref, q_ref, u_ref, gate_ref, kvc_ref, kvs_ref, kvw_ref):
    kvc, kvs, kvw = _inproj_common(x_ref, g_ref, w_ref, q_ref, u_ref, gate_ref)
    kvc_ref[...] = kvc
    kvs_ref[...] = kvs
    kvw_ref[...] = kvw


def _inproj_seq_kernel(x_ref, g_ref, w_ref, *refs):
    q_ref, u_ref, gate_ref, kb_ref, vst_ref, vwt_ref, kvct_ref, kvst_ref, kvwt_ref = refs[-9:]
    kvc, kvs, kvw = _inproj_common(x_ref, g_ref, w_ref, q_ref, u_ref, gate_ref)
    kb_ref[...] = jnp.concatenate([kvs[:, 0:KV_LANES], kvw[:, 0:KV_LANES]], axis=1).astype(BF16)
    kvst = kvs.T
    kvwt = kvw.T
    kvct_ref[0, 0] = kvc.T
    kvst_ref[0, 0] = kvst
    for later in range(1, kvct_ref.shape[0]):
        kvct_ref[later, 0] = jnp.zeros(kvst.shape, F32)
        kvst_ref[later, 0] = jnp.zeros(kvst.shape, F32)
    kvwt_ref[0] = kvwt
    ones = jnp.ones((VT_ROWS - KV_LANES, kvst.shape[1]), F32)
    vst_ref[0] = jnp.concatenate([kvst[KV_LANES:KV_COLS], ones], axis=0).astype(BF16)
    vwt_ref[0] = jnp.concatenate([kvwt[KV_LANES:KV_COLS], ones], axis=0).astype(BF16)


def _inproj_rows(x, g, w):
    n = x.shape[0]
    tm = min(ROW_TILE, n)
    row = lambda c: pl.BlockSpec((tm, c), lambda i: (i, 0))
    full = lambda a: pl.BlockSpec(a.shape, lambda i: (0,) * a.ndim)
    outs = [(ATT_WIDTH, BF16), (POOL_WIDTH, F32), (LANE, F32), (KV_COLS, F32), (KV_COLS, F32), (KV_COLS, F32)]
    return pl.pallas_call(
        _inproj_rows_kernel,
        grid=(n // tm,),
        in_specs=[row(D_MODEL), full(g), full(w)],
        out_specs=[row(c) for c, _ in outs],
        out_shape=[jax.ShapeDtypeStruct((n, c), dt) for c, dt in outs],
        compiler_params=pltpu.CompilerParams(dimension_semantics=("parallel",), vmem_limit_bytes=VMEM_LIMIT),
    )(x, g, w)


def _inproj_seq(layer, x, g, w, all_shape, kvct_all, kvst_all, wb):
    n = x.shape[0]
    depth, b, _, t = all_shape
    assert (kvct_all is None) == (layer == 0)
    tm = min(ROW_TILE, t)
    tiles = t // tm
    assert wb % tm == 0 and wb <= t
    dropped = tiles - wb // tm
    row = lambda c: pl.BlockSpec((tm, c), lambda i: (i, 0))
    full = lambda a: pl.BlockSpec(a.shape, lambda i: (0,) * a.ndim)
    hbm = pl.BlockSpec(memory_space=pl.ANY)
    seq_t = lambda r: pl.BlockSpec((1, r, tm), lambda i: (i // tiles, 0, i % tiles))
    all_t = pl.BlockSpec((depth if layer == 0 else 1, 1, KV_COLS, tm), lambda i: (layer, i // tiles, 0, i % tiles))
    tail_t = pl.BlockSpec((1, KV_COLS, tm), lambda i: (i // tiles, 0, jnp.maximum(i % tiles - dropped, 0)))
    rows = [(ATT_WIDTH, BF16), (POOL_WIDTH, F32), (LANE, F32), (KV_COLS, BF16)]
    carried = [] if layer == 0 else [kvct_all, kvst_all]
    first_all = len(rows) + 2
    return pl.pallas_call(
        _inproj_seq_kernel,
        grid=(n // tm,),
        in_specs=[row(D_MODEL), full(g), full(w)] + [hbm] * len(carried),
        out_specs=[row(c) for c, _ in rows] + [seq_t(VT_ROWS), seq_t(VT_ROWS), all_t, all_t, tail_t],
        out_shape=[jax.ShapeDtypeStruct((n, c), dt) for c, dt in rows]
        + [jax.ShapeDtypeStruct((b, VT_ROWS, t), BF16), jax.ShapeDtypeStruct((b, VT_ROWS, t), BF16),
           jax.ShapeDtypeStruct(all_shape, F32), jax.ShapeDtypeStruct(all_shape, F32),
           jax.ShapeDtypeStruct((b, KV_COLS, wb), F32)],
        input_output_aliases={3 + k: first_all + k for k in range(len(carried))},
        compiler_params=pltpu.CompilerParams(dimension_semantics=("arbitrary",), vmem_limit_bytes=VMEM_LIMIT),
    )(x, g, w, *carried)


def _compress_tokens(x_of, w_ref, pe_ref, n_sub):
    outs = []
    for j in range(2):
        parts = [jnp.zeros((n_sub, KV_LANES), F32) for _ in range(CMP_RATIO)]
        for l in range(CMP_STRIDE):
            x = x_of(l, j)
            for r in range(CMP_RATIO):
                row = (j * CMP_RATIO + r) * CMP_STRIDE + l
                a = (x + pe_ref[row:row + 1, :]).astype(BF16)
                parts[r] = parts[r] + jnp.dot(a, w_ref[j, r, l], preferred_element_type=F32)
        comp = parts[0]
        for r in range(1, CMP_RATIO):
            comp = comp + pltpu.roll(parts[r], n_sub - r, axis=0)
        outs.append(comp)
    return outs


def _regroup_tokens(feat_tok, x_s, first_sub):
    pitch = x_s.shape[0] // CMP_STRIDE
    xt = feat_tok.T
    for r in range(xt.shape[0] // 8):
        l0 = (8 * r) % CMP_STRIDE
        sub = first_sub + (8 * r) // CMP_STRIDE
        x_s[pl.ds(l0 * pitch + sub, 8, stride=pitch), :] = xt[8 * r:8 * r + 8, :]


def _regrouped_rows(x_refs, n_sub):
    def x_of(l, j):
        pitch = x_refs[j].shape[0] // CMP_STRIDE
        return x_refs[j][l * pitch:l * pitch + n_sub, :]
    return x_of


def _compress_kernel(xt_ref, w_ref, pe_ref, ck_ref, cvt_ref, xk_s, xv_s):
    t = xt_ref.shape[3]
    n_sub = t // CMP_STRIDE
    for p in range(t // LANE):
        tok = slice(p * LANE, (p + 1) * LANE)
        _regroup_tokens(xt_ref[0, 0, 0:KV_LANES, tok], xk_s, p * (LANE // CMP_STRIDE))
        _regroup_tokens(xt_ref[0, 0, KV_LANES:KV_COLS, tok], xv_s, p * (LANE // CMP_STRIDE))
    ck, cv = _compress_tokens(_regrouped_rows((xk_s, xv_s), n_sub), w_ref, pe_ref, n_sub)
    ck_ref[0] = ck.astype(BF16)
    cvt_ref[0] = cv.T.astype(BF16)


def _compress_prompt(layer, kvct_all, w_bd, pe_rows):
    _, b, _, t = kvct_all.shape
    n_sub = t // CMP_STRIDE
    full = lambda a: pl.BlockSpec(a.shape, lambda i: (0,) * a.ndim)
    slabs = pltpu.VMEM((CMP_STRIDE * (n_sub + 8), KV_LANES), F32)
    return pl.pallas_call(
        _compress_kernel,
        grid=(b,),
        in_specs=[pl.BlockSpec((1, 1, KV_COLS, t), lambda i: (layer, i, 0, 0)), full(w_bd), full(pe_rows)],
        out_specs=[pl.BlockSpec((1, n_sub, KV_LANES), lambda i: (i, 0, 0)),
                   pl.BlockSpec((1, KV_LANES, n_sub), lambda i: (i, 0, 0))],
        out_shape=[jax.ShapeDtypeStruct((b, n_sub, KV_LANES), BF16), jax.ShapeDtypeStruct((b, KV_LANES, n_sub), BF16)],
        scratch_shapes=[slabs, slabs],
        compiler_params=pltpu.CompilerParams(dimension_semantics=("parallel",), vmem_limit_bytes=VMEM_LIMIT),
    )(kvct_all, w_bd, pe_rows)


def _split_dot(m, p):
    hi = p.astype(BF16)
    lo = (p - hi.astype(F32)).astype(BF16)
    return jnp.dot(m, hi, preferred_element_type=F32) + jnp.dot(m, lo, preferred_element_type=F32)


def _col_max(s):
    rows = s.shape[0]
    parts = [s[r:r + 64] for r in range(0, rows, 64)] if rows % 64 == 0 and rows > 64 else [s]
    while len(parts) > 1:
        parts = [jnp.maximum(a, b) for a, b in zip(parts[0::2], parts[1::2])] + ([parts[-1]] if len(parts) % 2 else [])
    return jnp.max(parts[0], axis=0, keepdims=True)


def _pool_mix(ext_ref, u, pos, wpool_ref, pscale_ref, rows):
    outs = []
    for k, w in enumerate(POOL_WINDOWS):
        ln = slice(k * POOL_GROUP_DIM, (k + 1) * POOL_GROUP_DIM)
        ws = ext_ref[POOL_HALO:POOL_HALO + rows, ln]
        for s in range(1, w):
            ws = ws + ext_ref[POOL_HALO - s:POOL_HALO - s + rows, ln]
        cnt = jnp.minimum(pos + 1, w).astype(F32)
        pooled = (ws / cnt - u[:, ln]).astype(BF16)
        o = jnp.dot(pooled, wpool_ref[k], preferred_element_type=F32) * pscale_ref[:, ln]
        outs.append(o.astype(BF16))
    return outs


def _prompt_attn_kernel(q_ref, gate_ref, u_ref, uh_ref, pst_ref, ksel_ref, kwin_ref, vst_ref, vwt_ref,
                        ck_ref, cvt_ref, delta_ref, mimp_ref, eblk_ref, wpool_ref, pscale_ref,
                        mix_ref,
                        sc_s, qa_s, qf_s, acc_s, m_s, ext_s):
    step = pl.program_id(1)
    n_blk = q_ref.shape[1] // TQ
    t_len = ksel_ref.shape[1]
    n_cmp = ck_ref.shape[1]
    n_sel = t_len // SEL_BLOCK
    cols = N_HEADS * TQ
    blk_per_q = TQ // SEL_BLOCK
    n_chain = cols // CHAIN_COLS
    chains = [slice(c * CHAIN_COLS, (c + 1) * CHAIN_COLS) for c in range(n_chain)]
    n_wc = WINDOW // TQ + 1
    w_delta = {0: DW0, n_wc - 2: DD0, n_wc - 1: DD0 + TQ}

    val = [dict() for _ in range(n_blk)]
    qb_of = lambda a: n_blk * step + a
    tok = lambda a: slice(a * TQ, (a + 1) * TQ)

    def stage_compressed(a):
        v, qb = val[a], qb_of(a)
        q = q_ref[0, tok(a), :].astype(F32)
        zeros = jnp.zeros((TQ, HEAD_DIM), F32)
        blocks = []
        for h in range(N_HEADS):
            halves = [zeros] * N_KV
            halves[h // GROUP] = q[:, h * HEAD_DIM:(h + 1) * HEAD_DIM]
            blocks.append(jnp.concatenate(halves, axis=1))
        qbd = jnp.concatenate(blocks, axis=0).astype(BF16)
        v["qbd"] = qbd
        sc_s[a, 0:16, :] = jnp.zeros((16, cols), F32)
        sc_s[a, 16 + n_cmp:16 + n_cmp + 8, :] = jnp.zeros((8, cols), F32)
        near = pl.ds(pl.multiple_of(8 * qb, 8), DC_ROWS)
        crow = lax.broadcasted_iota(jnp.int32, (n_cmp, CHAIN_COLS), 0)
        pn_c, o_c = [], []
        for cs in chains:
            sc = lax.dot_general(ck_ref[0], qbd[cs], NT_DIMS, preferred_element_type=F32)
            sc_s[a, 16:16 + n_cmp, cs] = jnp.where(crow >= 8 * qb + 8, NEG, sc)
            sc_s[a, near, cs] = sc_s[a, near, cs] + delta_ref[DC0:DC0 + DC_ROWS, cs]
            sc = sc_s[a, 16:16 + n_cmp, cs]
            m_c = jnp.max(sc, axis=0, keepdims=True)
            p = jnp.exp2(sc - m_c)
            l_c = jnp.sum(p, axis=0, keepdims=True)
            pn_c.append(p * jnp.where(m_c > VALID_MIN, 1.0 / l_c, 0.0))
            o_c.append(jnp.dot(cvt_ref[0], pn_c[-1].astype(BF16), preferred_element_type=F32))
        pn = jnp.concatenate(pn_c, axis=1)
        v["o_cmp"] = jnp.concatenate(o_c, axis=1)
        v["imp"] = []
        for g in range(N_KV):
            ps = pn[:, (g * GROUP) * TQ:(g * GROUP + 1) * TQ]
            for r in range(1, GROUP):
                ps = ps + pn[:, (g * GROUP + r) * TQ:(g * GROUP + r + 1) * TQ]
            v["imp"].append(_split_dot(mimp_ref[...], ps))

    def stage_window_scores(a):
        v, qb = val[a], qb_of(a)
        w_pos0 = [(qb - (n_wc - 1) + c) * TQ for c in range(n_wc)]
        w_k0 = [pl.multiple_of(jnp.maximum(p0, 0), TQ) for p0 in w_pos0]
        w_keys = [kwin_ref[0, pl.ds(k0, TQ), :] for k0 in w_k0]
        v["w_vt"] = [vwt_ref[0, :, pl.ds(k0, TQ)] for k0 in w_k0]
        v["s_win"] = []
        for cs in chains:
            s_w = []
            for c in range(n_wc):
                s = lax.dot_general(w_keys[c], v["qbd"][cs], NT_DIMS, preferred_element_type=F32)
                if c in w_delta:
                    s = s + delta_ref[w_delta[c]:w_delta[c] + TQ, cs]
                if c < n_wc - 1:
                    s = s + jnp.where(w_pos0[c] < 0, NEG, 0.0)
                s_w.append(s)
            v["s_win"].append(s_w)

    jidx = lax.broadcasted_iota(jnp.int32, (n_sel, TQ), 0)
    qi = lax.broadcasted_iota(jnp.int32, (n_sel, TQ), 1)
    pad = jnp.zeros((LANE - n_sel, TQ), F32)

    def stage_select(a):
        v, qb = val[a], qb_of(a)
        cur = blk_per_q * qb + qi // SEL_BLOCK
        forced = (jidx == 0) | (jidx == cur) | (jidx == cur - 1)
        prev_blk = blk_per_q * jnp.maximum(qb - 1, 0)
        ns_t = []
        for imp in v["imp"]:
            score = jnp.where(forced, jnp.inf, imp)
            score = jnp.where(jidx > cur, -jnp.inf, score)
            cnt = jnp.zeros((n_sel, TQ), jnp.int32)
            for jp in range(n_sel):
                row = score[jp:jp + 1, :]
                cnt = cnt + jnp.where(jidx > jp, jnp.where(row >= score, 1, 0), jnp.where(row > score, 1, 0))
            sel = (cnt < min(N_SELECT, n_sel)) & (jidx <= cur)
            ns_all = jnp.where(sel, 0.0, NEG)
            ns_far = jnp.where(jidx < prev_blk, ns_all, NEG)
            ns_t.append([jnp.concatenate([x, pad], axis=0).T.astype(BF16) for x in (ns_all, ns_far)])
        for h in range(N_HEADS):
            rows_h = slice(h * TQ, (h + 1) * TQ)
            qa_s[a, rows_h, 0:LANE] = v["qbd"][rows_h]
            qf_s[a, rows_h, 0:LANE] = v["qbd"][rows_h]
            qa_s[a, rows_h, LANE:2 * LANE] = ns_t[h // GROUP][0]
            qf_s[a, rows_h, LANE:2 * LANE] = ns_t[h // GROUP][1]
        m_s[a] = jnp.full((1, cols), NEG, F32)
        acc_s[a] = jnp.zeros((VT_ROWS, cols), F32)

    def online_update(a, cs, s, vt):
        m_old = m_s[a, :, cs]
        m_new = jnp.maximum(m_old, _col_max(s))
        alpha = jnp.exp2(m_old - m_new)
        p = jnp.exp2(s - m_new)
        acc_s[a, :, cs] = alpha * acc_s[a, :, cs] + jnp.dot(vt, p.astype(BF16), preferred_element_type=F32)
        m_s[a, :, cs] = m_new

    def keys_with_block(k0, n):
        return jnp.concatenate([ksel_ref[0, pl.ds(k0, n), :], eblk_ref[pl.ds(k0, n), :]], axis=1)

    def masked_scores(kcat, q_s, a, cs):
        return lax.dot_general(kcat, q_s[a, cs, :], NT_DIMS, preferred_element_type=F32)

    n_far = (TQ * jnp.maximum(qb_of(n_blk - 1) - 1, 0) + FAR_TILE - 1) // FAR_TILE

    def far_tiles(t0, n):
        k0 = [pl.multiple_of((t0 + i) * FAR_TILE, FAR_TILE) for i in range(n)]
        s_tiles = []
        for i in range(n):
            kcat = keys_with_block(k0[i], FAR_TILE)
            s_tiles.append([[masked_scores(kcat, qf_s, a, cs) for cs in chains] for a in range(n_blk)])
        for i in range(n):
            vt = vst_ref[0, :, pl.ds(k0[i], FAR_TILE)]
            for a in range(n_blk):
                for cs, s in zip(chains, s_tiles[i][a]):
                    online_update(a, cs, s, vt)

    def stage_near_scores(a):
        v, qb = val[a], qb_of(a)
        k_prev = pl.multiple_of(TQ * jnp.maximum(qb - 1, 0), TQ)
        k_diag = pl.multiple_of(qb * TQ, TQ)
        no_prev = jnp.where(qb == 0, NEG, 0.0)
        kcat_prev = keys_with_block(k_prev, TQ)
        kcat_diag = keys_with_block(k_diag, TQ)
        v["vt_near"] = jnp.concatenate([vst_ref[0, :, pl.ds(k_prev, TQ)], vst_ref[0, :, pl.ds(k_diag, TQ)]], axis=1)
        v["s_near"] = []
        for cs in chains:
            s_prev = masked_scores(kcat_prev, qa_s, a, cs) + delta_ref[DD0:DD0 + TQ, cs] + no_prev
            s_diag = masked_scores(kcat_diag, qa_s, a, cs) + delta_ref[DD0 + TQ:DD0 + 2 * TQ, cs]
            v["s_near"].append(jnp.concatenate([s_prev, s_diag], axis=0))

    def stage_near_update(a):
        v = val[a]
        for cs, s in zip(chains, v["s_near"]):
            online_update(a, cs, s, v["vt_near"])

    def stage_window(a):
        v = val[a]
        o_w = []
        for s_w in v["s_win"]:
            m_w = s_w[0].max(axis=0, keepdims=True)
            for s in s_w[1:]:
                m_w = jnp.maximum(m_w, s.max(axis=0, keepdims=True))
            o_c = jnp.zeros((VT_ROWS, CHAIN_COLS), F32)
            for s, vt in zip(s_w, v["w_vt"]):
                o_c = o_c + jnp.dot(vt, jnp.exp2(s - m_w).astype(BF16), preferred_element_type=F32)
            o_w.append(o_c)
        v["o_win"] = jnp.concatenate(o_w, axis=1)

    def stage_output(a):
        v, qb = val[a], qb_of(a)
        l_sel = acc_s[a, KV_LANES:KV_LANES + 1, :]
        o_sel = acc_s[a, 0:KV_LANES, :]
        l_win = v["o_win"][KV_LANES:KV_LANES + 1, :]
        o_win = v["o_win"][0:KV_LANES, :]
        gt = gate_ref[0, tok(a), :].T
        gate_row = lambda x: jnp.concatenate([gt[N_BRANCH * h + x:N_BRANCH * h + x + 1, :] for h in range(N_HEADS)],
                                             axis=1)
        o_t = v["o_cmp"] * gate_row(0) + o_sel * (gate_row(1) / l_sel) + o_win * (gate_row(2) / l_win)
        pieces = []
        for h in range(N_HEADS):
            g = h // GROUP
            pieces.append(o_t[g * HEAD_DIM:(g + 1) * HEAD_DIM, h * TQ:(h + 1) * TQ])
        mix_ref[0, tok(a), 0:ATT_WIDTH] = jnp.concatenate(pieces, axis=0).T.astype(BF16)
        u = u_ref[0, tok(a), :]
        ext = ext_s.at[a]
        if a == 0:
            ext[0:POOL_HALO, :] = jnp.where(qb == 0, pst_ref[0], uh_ref[0])
        else:
            ext[0:POOL_HALO, :] = u_ref[0, a * TQ - POOL_HALO:a * TQ, :]
        ext[POOL_HALO:POOL_HALO + TQ, :] = u
        pos = qb * TQ + lax.broadcasted_iota(jnp.int32, (TQ, 1), 0)
        for k, o in enumerate(_pool_mix(ext, u, pos, wpool_ref, pscale_ref, TQ)):
            mix_ref[0, tok(a), ATT_WIDTH + k * POOL_GROUP_DIM:ATT_WIDTH + (k + 1) * POOL_GROUP_DIM] = o

    def run(stage):
        for a in range(n_blk):
            stage(a)

    run(stage_compressed)
    run(stage_window_scores)
    run(stage_select)

    odd = lax.rem(n_far, 2)

    @pl.when(odd == 1)
    def _():
        far_tiles(0, 1)

    def pair_body(i, carry):
        far_tiles(odd + 2 * i, 2)
        return carry

    lax.fori_loop(0, lax.div(n_far, 2), pair_body, 0)

    run(stage_near_scores)
    run(stage_near_update)
    run(stage_window)
    run(stage_output)


def _prompt_attention(q, gates, u, pool_prev, kb, vst, vwt, ck, cvt, delta, mimp, eblk, wpool, pscale):
    b, t, _ = q.shape
    assert t % FAR_TILE == 0 and t >= WINDOW + TQ and t // SEL_BLOCK <= LANE
    n_cmp = ck.shape[1]
    cols = N_HEADS * TQ
    nq = PROMPT_QBLOCKS if (t // TQ) % PROMPT_QBLOCKS == 0 else 1
    rows = nq * TQ
    halo_per_step = rows // POOL_HALO
    full = lambda a: pl.BlockSpec(a.shape, lambda i, j: (0,) * a.ndim)
    qblk = lambda c: pl.BlockSpec((1, rows, c), lambda i, j: (i, j, 0))
    seq = lambda c: pl.BlockSpec((1, t, KV_LANES), lambda i, j: (i, 0, c))
    per_b = lambda a: pl.BlockSpec((1,) + a.shape[1:], lambda i, j: (i,) + (0,) * (a.ndim - 1))
    return pl.pallas_call(
        _prompt_attn_kernel,
        grid=(b, t // rows),
        in_specs=[qblk(ATT_WIDTH), qblk(LANE), qblk(POOL_WIDTH),
                  pl.BlockSpec((1, POOL_HALO, POOL_WIDTH), lambda i, j: (i, jnp.maximum(j * halo_per_step - 1, 0), 0)),
                  per_b(pool_prev), seq(0), seq(1), per_b(vst), per_b(vwt), per_b(ck), per_b(cvt),
                  full(delta), full(mimp), full(eblk), full(wpool), full(pscale)],
        out_specs=pl.BlockSpec((1, rows, D_MODEL), lambda i, j: (i, j, 0)),
        out_shape=jax.ShapeDtypeStruct((b, t, D_MODEL), BF16),
        scratch_shapes=[pltpu.VMEM((nq, 16 + n_cmp + 8, cols), F32),
                        pltpu.VMEM((nq, cols, 2 * LANE), BF16), pltpu.VMEM((nq, cols, 2 * LANE), BF16),
                        pltpu.VMEM((nq, VT_ROWS, cols), F32), pltpu.VMEM((nq, 1, cols), F32),
                        pltpu.VMEM((nq, POOL_HALO + TQ, POOL_WIDTH), F32)],
        compiler_params=pltpu.CompilerParams(dimension_semantics=("parallel", "arbitrary"),
                                             vmem_limit_bytes=VMEM_LIMIT),
    )(q, gates, u, u, pool_prev, kb, kb, vst, vwt, ck, cvt, delta, mimp, eblk, wpool, pscale)


SAMPLE_SEQS = 2
SD_LAST = 0
SD_NEW = LANE
SD_EDGE = 2 * LANE
SD_CMP = 3 * LANE


def _sample_attn_kernel(pt_ref, q_ref, gate_ref, u_ref, kvs_ref, kvw_ref, swin_ref, spool_ref,
                        ccache_ref, scache_ref, *refs, layer):
    (wcmp_ref, pe_ref, delta_ref, mimp_ref, expand_ref, wpool_ref, pscale_ref,
     mix_ref, nwin_ref, npool_ref, cbuf, sbuf, xk_s, xv_s, sem, ext_s) = refs[-16:]
    b = pl.program_id(0)
    nb = pl.num_programs(0)
    n_pages = pt_ref.shape[1]
    past = n_pages * PAGE_SIZE
    n_sub = past // CMP_STRIDE
    n_seq, tq = q_ref.shape[0], q_ref.shape[1]
    wb = swin_ref.shape[3]
    n_sel = past // SEL_BLOCK + 1
    sub_per_page = PAGE_SIZE // CMP_STRIDE
    slot = lax.rem(b, 2)

    def page_copies(step, slt):
        cps = []
        for s in range(n_seq):
            for p in range(n_pages):
                pg = pt_ref[step * n_seq + s, p]
                dst = (slice(None), pl.ds(p * PAGE_SIZE, PAGE_SIZE))
                cps.append(pltpu.make_async_copy(ccache_ref.at[layer, pg], cbuf.at[slt, s].at[dst], sem.at[0, slt]))
                cps.append(pltpu.make_async_copy(scache_ref.at[layer, pg], sbuf.at[slt, s].at[dst], sem.at[1, slt]))
        return cps

    @pl.when(b == 0)
    def _():
        for cp in page_copies(0, 0):
            cp.start()

    @pl.when(b + 1 < nb)
    def _():
        for cp in page_copies(b + 1, 1 - slot):
            cp.start()

    for cp in page_copies(b, slot):
        cp.wait()

    val = [dict() for _ in range(n_seq)]

    def scores(qbd, k_rows):
        return lax.dot_general(qbd, k_rows, NT_DIMS, preferred_element_type=F32)

    def scores_t(qbd, k_t):
        return jnp.dot(qbd, k_t.astype(BF16), preferred_element_type=F32)

    def pad_rows(x):
        return jnp.concatenate([x, jnp.zeros((LANE - tq, KV_LANES), F32)], axis=0)

    pad_new = lambda x: pad_rows(x).astype(BF16)
    d_last = delta_ref[:, SD_LAST:SD_LAST + LANE]
    d_new = delta_ref[:, SD_NEW:SD_NEW + LANE]

    def softmax_pv(s_old, vt_old, s_new, v_new):
        m = jnp.maximum(s_old.max(axis=1, keepdims=True), s_new.max(axis=1, keepdims=True))
        p_old = jnp.exp2(s_old - m)
        p_new = jnp.exp2(s_new - m)
        l = jnp.sum(p_old, axis=1, keepdims=True) + jnp.sum(p_new, axis=1, keepdims=True)
        o = (lax.dot_general(p_old.astype(BF16), vt_old.astype(BF16), NT_DIMS, preferred_element_type=F32)
             + jnp.dot(p_new.astype(BF16), v_new, preferred_element_type=F32))
        return o, l

    def stage_scores(s):
        v = val[s]
        q = q_ref[s].astype(F32)
        zeros = jnp.zeros((tq, HEAD_DIM), F32)
        blocks = []
        for h in range(N_HEADS):
            halves = [zeros] * N_KV
            halves[h // GROUP] = q[:, h * HEAD_DIM:(h + 1) * HEAD_DIM]
            blocks.append(jnp.concatenate(halves, axis=1))
        qbd = jnp.concatenate(blocks, axis=0).astype(BF16)
        v["qbd"] = qbd
        kvw_new = kvw_ref[s]
        s_first = scores_t(qbd, swin_ref[0, s, 0:KV_LANES, 0:LANE]) + delta_ref[:, SD_EDGE:SD_EDGE + LANE]
        s_mid = scores_t(qbd, swin_ref[0, s, 0:KV_LANES, LANE:wb - LANE])
        s_lastw = scores_t(qbd, swin_ref[0, s, 0:KV_LANES, wb - LANE:wb]) + d_last
        v["s_wold"] = jnp.concatenate([s_first, s_mid, s_lastw], axis=1)
        v["s_wnew"] = scores(qbd, pad_new(kvw_new[:, 0:KV_LANES])) + d_new
        kvs_new = kvs_ref[s]
        s_far = scores_t(qbd, sbuf[slot, s, 0:KV_LANES, 0:past - LANE])
        s_last = scores_t(qbd, sbuf[slot, s, 0:KV_LANES, past - LANE:past]) + d_last
        v["s_past"] = jnp.concatenate([s_far, s_last], axis=1)
        v["s_new"] = scores(qbd, pad_new(kvs_new[:, 0:KV_LANES])) + d_new

    def stage_regroup(s):
        for p in range(n_pages):
            tok = slice(p * PAGE_SIZE, (p + 1) * PAGE_SIZE)
            _regroup_tokens(cbuf[slot, s, 0:KV_LANES, tok], xk_s.at[s], p * sub_per_page)
            _regroup_tokens(cbuf[slot, s, KV_LANES:KV_COLS, tok], xv_s.at[s], p * sub_per_page)

    def stage_window(s):
        v = val[s]
        kvw_new = kvw_ref[s]
        v["o_win"], v["l_win"] = softmax_pv(v["s_wold"], swin_ref[0, s, KV_LANES:KV_COLS, :],
                                            v["s_wnew"], pad_new(kvw_new[:, KV_LANES:KV_COLS]))
        shifted = pltpu.roll(swin_ref[0, s], wb - tq, axis=1)
        new_t = jnp.concatenate([pad_rows(kvw_new[:, 0:KV_LANES]).T, pad_rows(kvw_new[:, KV_LANES:KV_COLS]).T], axis=0)
        new_t = pltpu.roll(new_t, LANE - tq, axis=1)
        lane = lax.broadcasted_iota(jnp.int32, (KV_COLS, LANE), 1)
        nwin_ref[0, s, :, 0:wb - LANE] = shifted[:, 0:wb - LANE]
        nwin_ref[0, s, :, wb - LANE:wb] = jnp.where(lane >= LANE - tq, new_t, shifted[:, wb - LANE:wb])
        for later in range(1, nwin_ref.shape[0]):
            nwin_ref[later, s] = jnp.zeros((KV_COLS, wb), F32)

    def stage_compress(s):
        v = val[s]
        ck, cv = _compress_tokens(_regrouped_rows((xk_s.at[s], xv_s.at[s]), n_sub), wcmp_ref, pe_ref, n_sub)
        v["cv"] = cv.astype(BF16)
        v["sc"] = scores(v["qbd"], ck.astype(BF16)) + delta_ref[:, SD_CMP:SD_CMP + n_sub]

    def stage_compressed(s):
        v = val[s]
        sc = v["sc"]
        m_c = jnp.max(sc, axis=1, keepdims=True)
        p = jnp.exp2(sc - m_c)
        l_c = jnp.sum(p, axis=1, keepdims=True)
        pn = p * jnp.where(m_c > VALID_MIN, 1.0 / l_c, 0.0)
        v["o_cmp"] = jnp.dot(pn.astype(BF16), v["cv"], preferred_element_type=F32)
        v["imp"] = []
        for g in range(N_KV):
            ps = pn[g * GROUP * tq:(g * GROUP + 1) * tq, :]
            for r in range(1, GROUP):
                ps = ps + pn[(g * GROUP + r) * tq:(g * GROUP + r + 1) * tq, :]
            hi = ps.astype(BF16)
            lo = (ps - hi.astype(F32)).astype(BF16)
            v["imp"].append(jnp.dot(hi, mimp_ref[...], preferred_element_type=F32)
                            + jnp.dot(lo, mimp_ref[...], preferred_element_type=F32))

    jidx = lax.broadcasted_iota(jnp.int32, (tq, LANE), 1)
    cur = (past + lax.broadcasted_iota(jnp.int32, (tq, LANE), 0)) // SEL_BLOCK
    forced = (jidx == 0) | (jidx == cur) | (jidx == cur - 1)

    def stage_select(s):
        v = val[s]
        sel_g = []
        for imp in v["imp"]:
            score = jnp.where(forced, jnp.inf, imp)
            score = jnp.where((jidx > cur) | (jidx >= n_sel), -jnp.inf, score)
            cnt = jnp.zeros((tq, LANE), jnp.int32)
            for jp in range(n_sel):
                col = score[:, jp:jp + 1]
                cnt = cnt + jnp.where(jidx > jp, jnp.where(col >= score, 1, 0), jnp.where(col > score, 1, 0))
            sel = (cnt < min(N_SELECT, n_sel)) & (jidx <= cur) & (jidx < n_sel)
            sel_g.append(jnp.where(sel, 1.0, 0.0))
        sel_rows = jnp.concatenate([sel_g[h // GROUP] for h in range(N_HEADS)], axis=0).astype(BF16)
        v["mask"] = jnp.dot(sel_rows, expand_ref[...], preferred_element_type=F32)

    def stage_selected(s):
        v = val[s]
        s_past = jnp.where(v["mask"] > 0.5, v["s_past"], NEG)
        v["o_sel"], v["l_sel"] = softmax_pv(s_past, sbuf[slot, s, KV_LANES:KV_COLS, :], v["s_new"],
                                            pad_new(kvs_ref[s][:, KV_LANES:KV_COLS]))

    def stage_output(s):
        v = val[s]
        gates = gate_ref[s]
        gate_col = lambda x: jnp.concatenate([gates[:, N_BRANCH * h + x:N_BRANCH * h + x + 1] for h in range(N_HEADS)],
                                             axis=0)
        o = (v["o_cmp"] * gate_col(0) + v["o_sel"] * (gate_col(1) / v["l_sel"])
             + v["o_win"] * (gate_col(2) / v["l_win"]))
        pieces = []
        for h in range(N_HEADS):
            g = h // GROUP
            pieces.append(o[h * tq:(h + 1) * tq, g * HEAD_DIM:(g + 1) * HEAD_DIM])
        mix_ref[s, :, 0:ATT_WIDTH] = jnp.concatenate(pieces, axis=1).astype(BF16)
        u = u_ref[s]
        ext = ext_s.at[s]
        ext[0:1, :] = jnp.zeros((1, POOL_WIDTH), F32)
        ext[1:POOL_HALO, :] = spool_ref[0, s]
        ext[POOL_HALO:POOL_HALO + tq, :] = u
        pos = past + lax.broadcasted_iota(jnp.int32, (tq, 1), 0)
        for k, o_k in enumerate(_pool_mix(ext, u, pos, wpool_ref, pscale_ref, tq)):
            mix_ref[s, :, ATT_WIDTH + k * POOL_GROUP_DIM:ATT_WIDTH + (k + 1) * POOL_GROUP_DIM] = o_k
        npool_ref[s] = ext[POOL_HALO + tq - POOL_STATE:POOL_HALO + tq, :]

    for stage in (stage_scores, stage_regroup, stage_window, stage_compress, stage_compressed, stage_select,
                  stage_selected, stage_output):
        for s in range(n_seq):
            stage(s)


def _sample_attention(layer, page_table, q, gates, u, kvs, kvw, swin_t, state_pool, ccache_t, scache_t, nwin_all,
                      wcmp, pe_rows, delta, mimp, expand, wpool, pscale):
    nb, tq, _ = q.shape
    assert (nwin_all is None) == (layer == 0)
    carried = [] if layer == 0 else [nwin_all]
    depth = swin_t.shape[0]
    n_pages = page_table.shape[1]
    past = n_pages * PAGE_SIZE
    n_sub = past // CMP_STRIDE
    wb = swin_t.shape[3]
    assert wb == WINDOW and wb >= 3 * LANE and tq <= CMP_STRIDE and POOL_HALO + tq - POOL_STATE >= 0
    assert past // SEL_BLOCK + 1 <= LANE and past % LANE == 0 and n_sub <= LANE
    ns = SAMPLE_SEQS if nb % SAMPLE_SEQS == 0 else 1
    full = lambda a: pl.BlockSpec(a.shape, lambda i, pt: (0,) * a.ndim)
    per_b = lambda c: pl.BlockSpec((ns, tq, c), lambda i, pt: (i, 0, 0))
    layer_b = lambda a: pl.BlockSpec((1, ns) + a.shape[2:], lambda i, pt: (layer, i) + (0,) * (a.ndim - 2))
    hbm = pl.BlockSpec(memory_space=pl.ANY)
    grid_spec = pltpu.PrefetchScalarGridSpec(
        num_scalar_prefetch=1,
        grid=(nb // ns,),
        in_specs=[per_b(ATT_WIDTH), per_b(LANE), per_b(POOL_WIDTH), per_b(KV_COLS), per_b(KV_COLS),
                  layer_b(swin_t), layer_b(state_pool), hbm, hbm] + [hbm] * len(carried)
        + [full(wcmp), full(pe_rows), full(delta), full(mimp), full(expand), full(wpool), full(pscale)],
        out_specs=[pl.BlockSpec((ns, tq, D_MODEL), lambda i, pt: (i, 0, 0)),
                   pl.BlockSpec((depth if layer == 0 else 1, ns, KV_COLS, wb), lambda i, pt: (layer, i, 0, 0)),
                   pl.BlockSpec((ns, POOL_STATE, POOL_WIDTH), lambda i, pt: (i, 0, 0))],
        scratch_shapes=[pltpu.VMEM((2, ns, KV_COLS, past), F32), pltpu.VMEM((2, ns, KV_COLS, past), F32),
                        pltpu.VMEM((ns, CMP_STRIDE * (n_sub + 8), KV_LANES), F32),
                        pltpu.VMEM((ns, CMP_STRIDE * (n_sub + 8), KV_LANES), F32),
                        pltpu.SemaphoreType.DMA((2, 2)), pltpu.VMEM((ns, POOL_HALO + tq, POOL_WIDTH), F32)],
    )
    return pl.pallas_call(
        functools.partial(_sample_attn_kernel, layer=layer),
        grid_spec=grid_spec,
        out_shape=[jax.ShapeDtypeStruct((nb, tq, D_MODEL), BF16),
                   jax.ShapeDtypeStruct(swin_t.shape, F32),
                   jax.ShapeDtypeStruct((nb, POOL_STATE, POOL_WIDTH), F32)],
        input_output_aliases={10: 1} if carried else {},
        compiler_params=pltpu.CompilerParams(dimension_semantics=("arbitrary",), vmem_limit_bytes=VMEM_LIMIT),
    )(page_table, q, gates, u, kvs, kvw, swin_t, state_pool, ccache_t, scache_t, *carried,
      wcmp, pe_rows, delta, mimp, expand, wpool, pscale)


def _ffn_kernel(x_ref, mix_ref, wout_ref, gn_ref, win_ref, wd_ref, gf_ref, o_ref, *, final_norm):
    xm = x_ref[...] + jnp.dot(mix_ref[...], wout_ref[...], preferred_element_type=F32)
    h = _rms(xm, gn_ref[...]).astype(BF16)
    d_ff = wd_ref.shape[0]
    cuts = list(range(0, d_ff, FFN_SLICE)) + [d_ff]
    slices = list(zip(cuts[:-1], cuts[1:]))

    def gate_up(a, b):
        return (jnp.dot(h, win_ref[:, a:b], preferred_element_type=F32),
                jnp.dot(h, win_ref[:, d_ff + a:d_ff + b], preferred_element_type=F32))

    y = xm
    pending = gate_up(*slices[0])
    for j, (a, b) in enumerate(slices):
        gate, up = pending
        if j + 1 < len(slices):
            pending = gate_up(*slices[j + 1])
        act = (gate * _sigmoid(gate) * up).astype(BF16)
        y = y + jnp.dot(act, wd_ref[a:b, :], preferred_element_type=F32)
    o_ref[...] = _rms(y, gf_ref[...]) if final_norm else y


def _ffn(x, mix, wout, gn, wffn_in, wffn_out, gf, final_norm):
    n = x.shape[0]
    tm = min(ROW_TILE, n)
    assert n % tm == 0 and wffn_out.shape[0] % LANE == 0
    once = lambda a: pl.BlockSpec(a.shape, lambda i: (0,) * a.ndim, pipeline_mode=pl.Buffered(1))
    row = lambda cdim: pl.BlockSpec((tm, cdim), lambda i: (i, 0))
    return pl.pallas_call(
        functools.partial(_ffn_kernel, final_norm=final_norm),
        grid=(n // tm,),
        in_specs=[row(D_MODEL), row(D_MODEL), once(wout), once(gn), once(wffn_in), once(wffn_out), once(gf)],
        out_specs=row(D_MODEL),
        out_shape=jax.ShapeDtypeStruct((n, D_MODEL), F32),
        compiler_params=pltpu.CompilerParams(dimension_semantics=("parallel",), vmem_limit_bytes=VMEM_LIMIT),
    )(x, mix, wout, gn, wffn_in, wffn_out, gf)


def _importance_matrix(n_sel, n_cmp_rows, n_cmp):
    spb = SEL_BLOCK // CMP_STRIDE
    j = np.arange(n_sel)[:, None]
    c = np.arange(n_cmp_rows)[None, :]
    return ((c >= spb * j - (CMP_RATIO - 1)) & (c <= spb * j + spb - 1) & (c < n_cmp)).astype(np.float32)


def kernel(x_prompt, x_sample, cache_cmp, cache_sel, state_win, state_pool, page_table, rel_bias, norm_mix, norm_ffn,
           norm_final, w_in, w_out, cmp_pos, w_cmp, w_pool, pool_scale, w_ffn_in, w_ffn_out):
    bp, t, _ = x_prompt.shape
    bs, tq, _ = x_sample.shape
    depth = w_in.shape[0]
    n_pages = page_table.shape[1]
    past = n_pages * PAGE_SIZE
    wb = state_win.shape[2]
    _check_far_bucket(max(t, past + tq) + WINDOW)

    rb_prompt = jnp.repeat(rel_bias.astype(F32), TQ, axis=1)
    delta_p = _bias_tiles(_prompt_bucket_table(), rb_prompt)
    rb_sample = jnp.pad(jnp.repeat(rel_bias.astype(F32), tq, axis=1), ((0, 0), (0, LANE - N_HEADS * tq)))
    delta_s = _bias_tiles(_sample_bucket_table(past, tq, wb), rb_sample).T[:N_HEADS * tq]

    n_sub_p = t // CMP_STRIDE
    mimp_p = jnp.asarray(_importance_matrix(t // SEL_BLOCK, n_sub_p, n_sub_p - CMP_RATIO + 1), BF16)
    eblk_p = jnp.asarray((np.arange(t)[:, None] // SEL_BLOCK == np.arange(LANE)[None, :]).astype(np.float32), BF16)
    n_sub_s = past // CMP_STRIDE
    mimp_s = jnp.asarray(_importance_matrix(LANE, n_sub_s, n_sub_s - CMP_RATIO + 1).T, BF16)
    expand = jnp.asarray((np.arange(LANE)[:, None] == np.arange(past)[None, :] // SEL_BLOCK).astype(np.float32), BF16)

    pool_zero = jnp.zeros((bp, POOL_HALO, POOL_WIDTH), F32)
    feature_major = lambda a: jnp.moveaxis(a, 2, -1).reshape(a.shape[0], a.shape[1], KV_COLS, a.shape[2])
    token_major = lambda a: jnp.moveaxis(a.reshape(a.shape[:2] + (2, N_KV, HEAD_DIM, a.shape[3])), -1, 2)
    ccache_t = feature_major(cache_cmp)
    scache_t = feature_major(cache_sel)
    swin_t = feature_major(state_win)
    kvct_all = kvst_all = nwin_all = None

    xp = x_prompt.reshape(bp * t, D_MODEL)
    xs = x_sample.reshape(bs * tq, D_MODEL)
    outs = {k: [] for k in ("p_win", "p_pool", "s_cmp", "s_sel", "s_pool")}
    kv_shape = lambda b, n: (b, n, 2, N_KV, HEAD_DIM)
    off_g = ATT_WIDTH + N_BRANCH * KV_COLS
    for l in range(depth):
        wl = w_in[l]
        w_all = jnp.concatenate([wl[:, :ATT_WIDTH] * (HEAD_DIM ** -0.5 * LOG2E), wl[:, ATT_WIDTH:off_g],
                                 wl[:, off_g + GATE_COLS:], wl[:, off_g:off_g + GATE_COLS],
                                 jnp.zeros((D_MODEL, LANE - GATE_COLS), F32)], axis=1).astype(BF16)
        g_mix = norm_mix[l].reshape(1, D_MODEL)
        g_ffn = norm_ffn[l].reshape(1, D_MODEL)
        g_fin = norm_final.reshape(1, D_MODEL)
        wc = w_cmp[l].reshape(2, CMP_RATIO, CMP_STRIDE, HEAD_DIM, HEAD_DIM)
        zero = jnp.zeros_like(wc)
        w_bd = jnp.concatenate([jnp.concatenate([wc, zero], axis=-1), jnp.concatenate([zero, wc], axis=-1)],
                               axis=-2).astype(BF16)
        pe = cmp_pos[l].reshape(CMP_RATIO, CMP_STRIDE, 2, HEAD_DIM).transpose(2, 0, 1, 3)
        pe_rows = jnp.tile(pe.reshape(2 * CMP_RATIO * CMP_STRIDE, HEAD_DIM), (1, N_KV)).astype(F32)
        wpool = w_pool[l].astype(BF16)
        pscale = pool_scale[l].reshape(1, POOL_WIDTH)
        wout = w_out[l].astype(BF16)
        wffn_in = w_ffn_in[l].astype(BF16)
        wffn_out = w_ffn_out[l].astype(BF16)
        last = l == depth - 1

        q, u, gates, kb, vst, vwt, kvct_all, kvst_all, kvwt = _inproj_seq(
            l, xp, g_mix, w_all, (depth, bp, KV_COLS, t), kvct_all, kvst_all, wb)
        ck, cvt = _compress_prompt(l, kvct_all, w_bd, pe_rows)
        mix = _prompt_attention(q.reshape(bp, t, ATT_WIDTH), gates.reshape(bp, t, LANE), u.reshape(bp, t, POOL_WIDTH),
                                pool_zero, kb.reshape(bp, t, KV_COLS), vst, vwt, ck, cvt, delta_p, mimp_p, eblk_p,
                                wpool, pscale)
        xp = _ffn(xp, mix.reshape(bp * t, D_MODEL), wout, g_ffn, wffn_in, wffn_out, g_fin, last)
        outs["p_win"].append(kvwt)
        outs["p_pool"].append(u.reshape(bp, t, POOL_WIDTH)[:, t - POOL_STATE:])

        q, u, gates, kvc, kvs, kvw = _inproj_rows(xs, g_mix, w_all)
        mix, nwin_all, npool = _sample_attention(
            l, page_table, q.reshape(bs, tq, ATT_WIDTH), gates.reshape(bs, tq, LANE), u.reshape(bs, tq, POOL_WIDTH),
            kvs.reshape(bs, tq, KV_COLS), kvw.reshape(bs, tq, KV_COLS), swin_t, state_pool, ccache_t, scache_t,
            nwin_all, w_bd, pe_rows, delta_s, mimp_s, expand, wpool, pscale)
        xs = _ffn(xs, mix.reshape(bs * tq, D_MODEL), wout, g_ffn, wffn_in, wffn_out, g_fin, last)
        outs["s_cmp"].append(kvc.reshape(kv_shape(bs, tq)))
        outs["s_sel"].append(kvs.reshape(kv_shape(bs, tq)))
        outs["s_pool"].append(npool)

    return (xp.reshape(bp, t, D_MODEL), xs.reshape(bs, tq, D_MODEL),
            token_major(kvct_all), token_major(kvst_all), token_major(jnp.stack(outs["p_win"])),
            jnp.stack(outs["p_pool"]), jnp.stack(outs["s_cmp"]), jnp.stack(outs["s_sel"]), token_major(nwin_all),
            jnp.stack(outs["s_pool"]))
```
